```python
import math
import jax, jax.numpy as jnp
from jax import lax
import numpy as np

D_MODEL = 1024
BATCH = 4
SEQ = 4096
DEPTH = 4

CHUNK = 64
N_MIXERS = 3
EPS = 1e-6
N_FOX = (DEPTH + 2) // 3
N_S5 = (DEPTH + 1) // 3
N_POOL = DEPTH // 3
FOX_HEADS = 16
FOX_HEAD_DIM = D_MODEL // FOX_HEADS
Q_BLOCK = 128
FOX_IN = 3 * D_MODEL + FOX_HEADS
FORGET_BIAS_CENTER = 3.0
S5_GROUP = 16
S5_GROUPS = D_MODEL // S5_GROUP
S5_STATE = 64
S5_DT_MIN = 1e-3
S5_DT_MAX = 1e-1
POOL_WINDOWS = (2, 4, 8, 16)
POOL_GROUPS = len(POOL_WINDOWS)
POOL_WIDTH = D_MODEL // POOL_GROUPS
D_FF = -(-8 * D_MODEL // (3 * 256)) * 256

kernel_name = "interleaved_fox_s5_pool_hybrid"

F32 = jnp.float32


def rms_norm(x, g):
    xf = x.astype(F32)
    y = xf * lax.rsqrt(jnp.mean(xf * xf, axis=-1, keepdims=True) + EPS)
    return (y * g.astype(F32)).astype(x.dtype)


def forgetting_attention(h, w_in, b_f, w_out):
    B, S, _ = h.shape
    proj = h @ w_in
    q, k, v, f_logit = jnp.split(proj, [D_MODEL, 2 * D_MODEL, 3 * D_MODEL], axis=-1)

    def heads(t):
        return t.reshape(B, S, FOX_HEADS, FOX_HEAD_DIM).transpose(0, 2, 1, 3).astype(F32)

    q, k, v = heads(q), heads(k), heads(v)
    log_f = jax.nn.log_sigmoid(f_logit.astype(F32) + b_f.astype(F32))
    c = jnp.cumsum(log_f, axis=1).transpose(0, 2, 1)
    n_blk = S // Q_BLOCK
    q_blocks = q.reshape(B, FOX_HEADS, n_blk, Q_BLOCK, FOX_HEAD_DIM).transpose(2, 0, 1, 3, 4)
    c_blocks = c.reshape(B, FOX_HEADS, n_blk, Q_BLOCK).transpose(2, 0, 1, 3)
    starts = jnp.arange(n_blk, dtype=jnp.int32) * Q_BLOCK
    key_pos = jnp.arange(S, dtype=jnp.int32)
    scale = FOX_HEAD_DIM ** -0.5

    def one_block(args):
        q_b, c_b, start = args
        logits = jnp.einsum('bhqd,bhkd->bhqk', q_b, k) * scale
        logits = logits + c_b[..., :, None] - c[..., None, :]
        q_pos = start + jnp.arange(Q_BLOCK, dtype=jnp.int32)
        mask = key_pos[None, :] <= q_pos[:, None]
        p = jax.nn.softmax(jnp.where(mask, logits, -jnp.inf), axis=-1)
        return jnp.einsum('bhqk,bhkd->bhqd', p, v)

    out = lax.map(one_block, (q_blocks, c_blocks, starts))
    out = out.transpose(1, 0, 3, 2, 4).reshape(B, S, D_MODEL).astype(h.dtype)
    return out @ w_out


def s5_layer(h, w_in, a_re, a_im, log_dt, b_re, b_im, c_re, c_im, d, w_glu):
    B, S, _ = h.shape
    u = (h @ w_in).astype(F32)
    u_g = u.reshape(B, S, S5_GROUPS, S5_GROUP).astype(jnp.complex64)
    lam = lax.complex(a_re.astype(F32), a_im.astype(F32))
    dt = jnp.exp(log_dt.astype(F32))[:, None]
    lam_bar = jnp.exp(lam * dt)
    b_mat = lax.complex(b_re.astype(F32), b_im.astype(F32))
    b_bar = ((lam_bar - 1.0) / lam)[..., None] * b_mat
    bu = jnp.einsum('gpc,bsgc->bsgp', b_bar, u_g)
    lam_seq = jnp.broadcast_to(lam_bar, bu.shape)

    def combine(left, right):
        a_l, s_l = left
        a_r, s_r = right
        return a_r * a_l, a_r * s_l + s_r

    _, states = lax.associative_scan(combine, (lam_seq, bu), axis=1)
    c_mat = lax.complex(c_re.astype(F32), c_im.astype(F32))
    y = jnp.einsum('gcp,bsgp->bsgc', c_mat, states).real.reshape(B, S, D_MODEL)
    y = y + d.astype(F32) * u
    g = jax.nn.gelu(y).astype(h.dtype)
    val, gate = jnp.split(g @ w_glu, 2, axis=-1)
    return val * jax.nn.sigmoid(gate)


def multiscale_pool(h, w_grp, b_grp, scale):
    B, S, _ = h.shape
    hf = h.astype(F32).reshape(B, S, POOL_GROUPS, POOL_WIDTH)
    count_base = jnp.arange(1, S + 1, dtype=F32)[None, :, None]
    outs = []
    for gi, w in enumerate(POOL_WINDOWS):
        xg = hf[:, :, gi]
        cs = jnp.cumsum(xg, axis=1)
        lag = jnp.pad(cs[:, :-w], ((0, 0), (w, 0), (0, 0)))
        mean = (cs - lag) / jnp.minimum(count_base, float(w))
        outs.append(jnp.einsum('bsc,cd->bsd', mean - xg, w_grp[gi].astype(F32)))
    y = jnp.concatenate(outs, axis=-1) + b_grp.astype(F32)
    return (y * scale.astype(F32)).astype(h.dtype)


def swiglu(h, w_gate_up, w_down):
    g, u = jnp.split(h @ w_gate_up, 2, axis=-1)
    return (jax.nn.silu(g) * u) @ w_down


def setup_inputs(seed: int = 0) -> dict:
    key = jax.random.key(seed)
    ks = jax.random.split(key, 24)
    D, H, G, P, C = D_MODEL, FOX_HEADS, S5_GROUPS, S5_STATE, S5_GROUP
    nrm = jax.random.normal
    x = nrm(ks[0], (BATCH, SEQ, D), F32)
    mix_norm_g = 1.0 + 0.05 * nrm(ks[1], (DEPTH, D), F32)
    ffn_norm_g = 1.0 + 0.05 * nrm(ks[2], (DEPTH, D), F32)
    final_norm_g = 1.0 + 0.05 * nrm(ks[3], (D,), F32)
    fox_w_in = nrm(ks[4], (N_FOX, D, FOX_IN), F32) * D ** -0.5
    fox_b_f = FORGET_BIAS_CENTER + 0.5 * nrm(ks[5], (N_FOX, H), F32)
    fox_w_out = nrm(ks[6], (N_FOX, D, D), F32) * D ** -0.5
    s5_w_in = nrm(ks[7], (N_S5, D, D), F32) * D ** -0.5
    s5_a_re = -0.5 + 0.01 * nrm(ks[8], (N_S5, G, P), F32)
    s5_a_im = math.pi * jnp.arange(P, dtype=F32)[None, None, :] + 0.01 * nrm(ks[9], (N_S5, G, P), F32)
    s5_log_dt = jax.random.uniform(ks[10], (N_S5, G), F32, math.log(S5_DT_MIN), math.log(S5_DT_MAX))
    s5_b_re = nrm(ks[11], (N_S5, G, P, C), F32) * (2 * C) ** -0.5
    s5_b_im = nrm(ks[12], (N_S5, G, P, C), F32) * (2 * C) ** -0.5
    s5_c_re = nrm(ks[13], (N_S5, G, C, P), F32) * P ** -0.5
    s5_c_im = nrm(ks[14], (N_S5, G, C, P), F32) * P ** -0.5
    s5_d = nrm(ks[15], (N_S5, D), F32)
    s5_w_glu = nrm(ks[16], (N_S5, D, 2 * D), F32) * D ** -0.5
    pool_w = nrm(ks[17], (N_POOL, POOL_GROUPS, POOL_WIDTH, POOL_WIDTH), F32) * POOL_WIDTH ** -0.5
    pool_b = 0.01 * nrm(ks[18], (N_POOL, D), F32)
    pool_scale = 0.5 + 0.05 * nrm(ks[19], (N_POOL, D), F32)
    ffn_w_gate_up = nrm(ks[20], (DEPTH, D, 2 * D_FF), F32) * D ** -0.5
    ffn_w_down = nrm(ks[21], (DEPTH, D_FF, D), F32) * D_FF ** -0.5
    return {"x": x, "mix_norm_g": mix_norm_g, "ffn_norm_g": ffn_norm_g, "final_norm_g": final_norm_g,
            "fox_w_in": fox_w_in, "fox_b_f": fox_b_f, "fox_w_out": fox_w_out,
            "s5_w_in": s5_w_in, "s5_a_re": s5_a_re, "s5_a_im": s5_a_im, "s5_log_dt": s5_log_dt,
            "s5_b_re": s5_b_re, "s5_b_im": s5_b_im, "s5_c_re": s5_c_re, "s5_c_im": s5_c_im,
            "s5_d": s5_d, "s5_w_glu": s5_w_glu,
            "pool_w": pool_w, "pool_b": pool_b, "pool_scale": pool_scale,
            "ffn_w_gate_up": ffn_w_gate_up, "ffn_w_down": ffn_w_down}


def reference(x, mix_norm_g, ffn_norm_g, final_norm_g, fox_w_in, fox_b_f, fox_w_out,
              s5_w_in, s5_a_re, s5_a_im, s5_log_dt, s5_b_re, s5_b_im, s5_c_re, s5_c_im,
              s5_d, s5_w_glu, pool_w, pool_b, pool_scale, ffn_w_gate_up, ffn_w_down):
    for i in range(DEPTH):
        h = rms_norm(x, mix_norm_g[i])
        kind, j = i % N_MIXERS, i // N_MIXERS
        if kind == 0:
            mix = forgetting_attention(h, fox_w_in[j], fox_b_f[j], fox_w_out[j])
        elif kind == 1:
            mix = s5_layer(h, s5_w_in[j], s5_a_re[j], s5_a_im[j], s5_log_dt[j], s5_b_re[j], s5_b_im[j],
                           s5_c_re[j], s5_c_im[j], s5_d[j], s5_w_glu[j])
        else:
            mix = multiscale_pool(h, pool_w[j], pool_b[j], pool_scale[j])
        x = x + mix.astype(x.dtype)
        x = x + swiglu(rms_norm(x, ffn_norm_g[i]), ffn_w_gate_up[i], ffn_w_down[i]).astype(x.dtype)
    return rms_norm(x, final_norm_g)
```

```python
import functools
import math

import jax
import jax.numpy as jnp
from jax import lax
from jax.experimental import pallas as pl
from jax.experimental.pallas import tpu as pltpu

F32 = jnp.float32
BF16 = jnp.bfloat16

D_MODEL = 1024
DEPTH = 4
EPS = 1e-6
FOX_HEADS = 16
FOX_HEAD_DIM = 64
S5_GROUP = 16
S5_GROUPS = 64
S5_STATE = 64
POOL_WINDOWS = (2, 4, 8, 16)
POOL_WIDTH = 256
D_FF = 2816

LANES = 128
SUBLANES = 8
MXU_DIM = 256
VMEM_LIMIT_BYTES = 56 * 1024 * 1024

LOG2E = 1.4426950408889634
NEG_BIG = -1e30

TM = 512
FF_CHUNK = MXU_DIM
TQ = 256
TKS = 256
TKB = 1024
S5_SEGS = SUBLANES
S5_TJ = 32
S5_CHUNK = S5_SEGS * S5_TJ
S5_SLABS = 4
S5_SLAB_IN = D_MODEL // S5_SLABS
S5_SLAB_HALF = (S5_GROUPS // S5_SLABS) * S5_STATE
S5_SLAB_W = 2 * S5_SLAB_HALF
S5_SW = S5_SLABS * S5_SLAB_W
S5_COLS_PER_STEP = 4


def _resident(shape):
    nd = len(shape)
    return pl.BlockSpec(shape, lambda *_: (0,) * nd, pipeline_mode=pl.Buffered(1))


def _rms(x, g):
    return x * lax.rsqrt(jnp.mean(x * x, axis=-1, keepdims=True) + EPS) * g


def _dot(a, b):
    return jnp.dot(a, b, preferred_element_type=F32)


def _fox_proj_kernel(x_ref, g_ref, wqkv_ref, wft_ref, bf_ref, tri_ref,
                     qkv_ref, c_ref, carry_ref):
    @pl.when(pl.program_id(1) == 0)
    def _():
        carry_ref[...] = jnp.zeros_like(carry_ref)

    h = _rms(x_ref[0], g_ref[...]).astype(BF16)
    qkv = _dot(h, wqkv_ref[...])
    q = qkv[:, :D_MODEL] * (FOX_HEAD_DIM ** -0.5 * LOG2E)
    qkv_ref[0, :, :D_MODEL] = q.astype(BF16)
    qkv_ref[0, :, D_MODEL:] = qkv[:, D_MODEL:].astype(BF16)

    z = lax.dot_general(wft_ref[...], h, (((1,), (1,)), ((), ())),
                        preferred_element_type=F32) + bf_ref[...]
    logf = jnp.minimum(z, 0.0) - jnp.log(1.0 + jnp.exp(-jnp.abs(z)))
    hi = logf.astype(BF16)
    r1 = logf - hi.astype(F32)
    mid = r1.astype(BF16)
    lo = (r1 - mid.astype(F32)).astype(BF16)
    tri = tri_ref[...]
    c = _dot(hi, tri) + _dot(mid, tri) + _dot(lo, tri) + carry_ref[:, 0:1]
    c_ref[0] = c * LOG2E
    carry_ref[...] = jnp.broadcast_to(c[:, TM - 1:TM], carry_ref.shape)


def _fox_proj(x, g, wqkv, wft, bf, tri):
    B, S, D = x.shape
    return pl.pallas_call(
        _fox_proj_kernel,
        grid=(B, S // TM),
        in_specs=[
            pl.BlockSpec((1, TM, D), lambda b, t: (b, t, 0)),
            _resident((1, D)),
            _resident((D, 3 * D)),
            _resident((FOX_HEADS, D)),
            _resident((FOX_HEADS, 1)),
            _resident((TM, TM)),
        ],
        out_specs=[
            pl.BlockSpec((1, TM, 3 * D), lambda b, t: (b, t, 0)),
            pl.BlockSpec((1, FOX_HEADS, TM), lambda b, t: (b, 0, t)),
        ],
        out_shape=[
            jax.ShapeDtypeStruct((B, S, 3 * D), BF16),
            jax.ShapeDtypeStruct((B, FOX_HEADS, S), F32),
        ],
        scratch_shapes=[pltpu.VMEM((FOX_HEADS, LANES), F32)],
        compiler_params=pltpu.CompilerParams(
            dimension_semantics=("parallel", "arbitrary"),
            vmem_limit_bytes=VMEM_LIMIT_BYTES),
        name="fox_proj",
    )(x, g, wqkv, wft, bf, tri)


def _fox_attn_kernel(q_ref, k_ref, v_ref, c_ref, o_ref, qs_ref, m_ref, l_ref, acc_ref):
    lane = lax.broadcasted_iota(jnp.int32, (TQ, LANES), 1)
    first = lane < FOX_HEAD_DIM

    def chunk(k0, c0, width, masked):
        k = k_ref[0, pl.ds(k0, width), :]
        v = v_ref[0, pl.ds(k0, width), :]
        s = lax.dot_general(qs_ref[...], k, (((1,), (1,)), ((), ())),
                            preferred_element_type=F32)
        pieces = [c_ref[0, 0, c0 + i] for i in range(width // TKS)]
        ck = pieces[0] if len(pieces) == 1 else jnp.concatenate(pieces, axis=1)
        s = s - jnp.concatenate(
            [jnp.broadcast_to(ck[0:1], (TQ, width)),
             jnp.broadcast_to(ck[1:2], (TQ, width))], axis=0)
        if masked:
            row = lax.broadcasted_iota(jnp.int32, s.shape, 0) & (TQ - 1)
            col = lax.broadcasted_iota(jnp.int32, s.shape, 1)
            s = jnp.where(col <= row, s, NEG_BIG)
        m_prev = m_ref[...]
        m_next = jnp.maximum(m_prev, jnp.max(s, axis=1, keepdims=True))
        alpha = jnp.exp2(m_prev - m_next)
        p = jnp.exp2(s - pltpu.repeat(m_next, width // LANES, axis=1))
        l_ref[...] = alpha * l_ref[...] + jnp.sum(p, axis=1, keepdims=True)
        acc_ref[...] = alpha * acc_ref[...] + _dot(p.astype(BF16), v)
        m_ref[...] = m_next

    def q_block(qb, carry):
        r0 = pl.multiple_of(qb * TQ, TQ)
        q = q_ref[0, pl.ds(r0, TQ), :]
        zero = jnp.zeros_like(q)
        qs_ref[0:TQ] = jnp.where(first, q, zero)
        qs_ref[TQ:2 * TQ] = jnp.where(first, zero, q)
        m_ref[...] = jnp.full(m_ref.shape, NEG_BIG, F32)
        l_ref[...] = jnp.zeros_like(l_ref)
        acc_ref[...] = jnp.zeros_like(acc_ref)

        ratio = TKB // TKS
        n_big = qb // ratio

        def big(i, c):
            chunk(pl.multiple_of(i * TKB, TKB), i * ratio, TKB, False)
            return c

        lax.fori_loop(0, n_big, big, 0)

        def small(i, c):
            chunk(pl.multiple_of(i * TKS, TKS), i, TKS, False)
            return c

        lax.fori_loop(n_big * ratio, qb, small, 0)
        chunk(r0, qb, TKS, True)

        o = acc_ref[...] / l_ref[...]
        o_ref[0, pl.ds(r0, TQ), :] = jnp.where(first, o[:TQ], o[TQ:]).astype(BF16)
        return carry

    lax.fori_loop(0, q_ref.shape[1] // TQ, q_block, 0)


def _fox_attn(qkv, c):
    B, S, _ = qkv.shape
    pairs = FOX_HEADS // 2
    blk = lambda off: pl.BlockSpec((1, S, LANES), lambda b, p: (b, 0, off + p))
    return pl.pallas_call(
        _fox_attn_kernel,
        grid=(B, pairs),
        in_specs=[
            blk(0), blk(pairs), blk(2 * pairs),
            pl.BlockSpec((1, 1, S // TKS, 2, TKS), lambda b, p: (b, p, 0, 0, 0)),
        ],
        out_specs=pl.BlockSpec((1, S, LANES), lambda b, p: (b, 0, p)),
        out_shape=jax.ShapeDtypeStruct((B, S, D_MODEL), BF16),
        scratch_shapes=[
            pltpu.VMEM((2 * TQ, LANES), BF16),
            pltpu.VMEM((2 * TQ, LANES), F32),
            pltpu.VMEM((2 * TQ, LANES), F32),
            pltpu.VMEM((2 * TQ, LANES), F32),
        ],
        compiler_params=pltpu.CompilerParams(
            dimension_semantics=("parallel", "parallel"),
            vmem_limit_bytes=VMEM_LIMIT_BYTES),
        name="fox_attn",
    )(qkv, qkv, qkv, c)


def _ffn_kernel(*refs, has_mix, final_norm):
    refs = list(refs)
    x_ref = refs.pop(0)
    if has_mix:
        a_ref = refs.pop(0)
        wmix_ref = refs.pop(0)
    g_ref, wgu_ref, wd_ref = refs[:3]
    refs = refs[3:]
    if final_norm:
        gf_ref = refs.pop(0)
    o_ref, act_ref = refs

    x = x_ref[...]
    if has_mix:
        x = x + _dot(a_ref[...], wmix_ref[...])
    h = _rms(x, g_ref[...]).astype(BF16)
    for c in range(D_FF // FF_CHUNK):
        lo = c * FF_CHUNK
        gate = _dot(h, wgu_ref[:, lo:lo + FF_CHUNK])
        up = _dot(h, wgu_ref[:, D_FF + lo:D_FF + lo + FF_CHUNK])
        act_ref[:, lo:lo + FF_CHUNK] = (gate * jax.nn.sigmoid(gate) * up).astype(BF16)
    y = x + _dot(act_ref[...], wd_ref[...])
    if final_norm:
        y = _rms(y, gf_ref[...])
    o_ref[...] = y


def _ffn(x2, g, wgu, wd, mix=None, final_g=None):
    N, D = x2.shape
    row = pl.BlockSpec((TM, D), lambda i: (i, 0))
    args, specs = [x2], [row]
    if mix is not None:
        a2, wmix = mix
        args += [a2, wmix]
        specs += [row, _resident(wmix.shape)]
    args += [g, wgu, wd]
    specs += [_resident(g.shape), _resident(wgu.shape), _resident(wd.shape)]
    if final_g is not None:
        args.append(final_g)
        specs.append(_resident(final_g.shape))
    return pl.pallas_call(
        functools.partial(_ffn_kernel, has_mix=mix is not None,
                          final_norm=final_g is not None),
        grid=(N // TM,),
        in_specs=specs,
        out_specs=row,
        out_shape=jax.ShapeDtypeStruct((N, D), F32),
        scratch_shapes=[pltpu.VMEM((TM, D_FF), BF16)],
        compiler_params=pltpu.CompilerParams(
            dimension_semantics=("parallel",),
            vmem_limit_bytes=VMEM_LIMIT_BYTES),
        name="ffn",
    )(*args)


def _s5_kernel(x_ref, g_ref, win_ref, bblk_ref, cblk_ref, lam_ref,
               d_ref, wglu_ref, o_ref, st_ref, carry_ref):
    @pl.when(pl.program_id(1) == 0)
    def _():
        carry_ref[...] = jnp.zeros_like(carry_ref)

    x = x_ref[0]
    h = _rms(x, g_ref[...]).astype(BF16)
    u = _dot(h, win_ref[...])
    ub = u.astype(BF16)
    for sl in range(S5_SLABS):
        st_ref[:, sl * S5_SLAB_W:(sl + 1) * S5_SLAB_W] = _dot(
            ub[:, sl * S5_SLAB_IN:(sl + 1) * S5_SLAB_IN], bblk_ref[sl])

    sub = lax.broadcasted_iota(jnp.int32, (SUBLANES, LANES), 0)
    vshape = (SUBLANES, LANES)
    blocks_per_slab = S5_SLAB_HALF // LANES
    n_blocks = S5_SLABS * blocks_per_slab

    def cmul(ar, ai, br, bi):
        return ar * br - ai * bi, ar * bi + ai * br

    for grp in range(n_blocks // S5_COLS_PER_STEP):
        offs = []
        for b in range(grp * S5_COLS_PER_STEP, (grp + 1) * S5_COLS_PER_STEP):
            re = (b // blocks_per_slab) * S5_SLAB_W + (b % blocks_per_slab) * LANES
            offs.append((re, re + S5_SLAB_HALF))
        lam = [(jnp.broadcast_to(lam_ref[:, re:re + LANES], vshape),
                jnp.broadcast_to(lam_ref[:, im:im + LANES], vshape)) for re, im in offs]

        def scan_body(j, state):
            r = pl.multiple_of(j * SUBLANES, SUBLANES)
            new = []
            for (re, im), (lr, li), (sr, si) in zip(offs, lam, state):
                pr, pi = cmul(lr, li, sr, si)
                nr = pr + st_ref[pl.ds(r, SUBLANES), re:re + LANES]
                ni = pi + st_ref[pl.ds(r, SUBLANES), im:im + LANES]
                st_ref[pl.ds(r, SUBLANES), re:re + LANES] = nr
                st_ref[pl.ds(r, SUBLANES), im:im + LANES] = ni
                new.append((nr, ni))
            return tuple(new)

        init = tuple((carry_ref[:, re:re + LANES], carry_ref[:, im:im + LANES])
                     for re, im in offs)
        fin = lax.fori_loop(0, S5_TJ, scan_body, init, unroll=4)

        carries = []
        for (re, im), (lr, li), (fr, fi) in zip(offs, lam, fin):
            for _ in range(S5_TJ.bit_length() - 1):
                lr, li = cmul(lr, li, lr, li)
            er, ei = fr, fi
            for k in range(1, S5_SEGS):
                pr, pi = cmul(lr, li, pltpu.roll(er, 1, 0), pltpu.roll(ei, 1, 0))
                er = jnp.where(sub == k, fr + pr, er)
                ei = jnp.where(sub == k, fi + pi, ei)
            sr, si = pltpu.roll(er, 1, 0), pltpu.roll(ei, 1, 0)
            carry_ref[:, re:re + LANES] = jnp.where(sub == 0, sr, 0.0)
            carry_ref[:, im:im + LANES] = jnp.where(sub == 0, si, 0.0)
            carries.append((jnp.where(sub == 0, 0.0, sr), jnp.where(sub == 0, 0.0, si)))

        def fix_body(j, f):
            r = pl.multiple_of(j * SUBLANES, SUBLANES)
            new = []
            for (re, im), (lr, li), (fr, fi) in zip(offs, lam, f):
                fr, fi = cmul(lr, li, fr, fi)
                st_ref[pl.ds(r, SUBLANES), re:re + LANES] += fr
                st_ref[pl.ds(r, SUBLANES), im:im + LANES] += fi
                new.append((fr, fi))
            return tuple(new)

        lax.fori_loop(0, S5_TJ, fix_body, tuple(carries), unroll=4)

    ys = [_dot(st_ref[:, sl * S5_SLAB_W:(sl + 1) * S5_SLAB_W].astype(BF16), cblk_ref[sl])
          for sl in range(S5_SLABS)]
    y = jnp.concatenate(ys, axis=1) + d_ref[...] * u
    gl = jax.nn.gelu(y).astype(BF16)
    vg = _dot(gl, wglu_ref[...])
    o_ref[0] = x + vg[:, :D_MODEL] * jax.nn.sigmoid(vg[:, D_MODEL:])


def _s5(xp, g, win, bblk, cblk, lam, d, wglu):
    B, S, D = xp.shape
    return pl.pallas_call(
        _s5_kernel,
        grid=(B, S // S5_CHUNK),
        in_specs=[
            pl.BlockSpec((1, S5_CHUNK, D), lambda b, c: (b, c, 0)),
            _resident(g.shape), _resident(win.shape), _resident(bblk.shape),
            _resident(cblk.shape), _resident(lam.shape),
            _resident(d.shape), _resident(wglu.shape),
        ],
        out_specs=pl.BlockSpec((1, S5_CHUNK, D), lambda b, c: (b, c, 0)),
        out_shape=jax.ShapeDtypeStruct((B, S, D), F32),
        scratch_shapes=[
            pltpu.VMEM((S5_CHUNK, S5_SW), F32),
            pltpu.VMEM((SUBLANES, S5_SW), F32),
        ],
        compiler_params=pltpu.CompilerParams(
            dimension_semantics=("parallel", "arbitrary"),
            vmem_limit_bytes=VMEM_LIMIT_BYTES),
        name="s5",
    )(xp, g, win, bblk, cblk, lam, d, wglu)


def _s5_params(a_re, a_im, log_dt, b_re, b_im, c_re, c_im):
    ar, ai = a_re.astype(F32), a_im.astype(F32)
    dt = jnp.exp(log_dt.astype(F32))[:, None]
    mag = jnp.exp(ar * dt)
    lr, li = mag * jnp.cos(ai * dt), mag * jnp.sin(ai * dt)
    den = ar * ar + ai * ai
    kr = ((lr - 1.0) * ar + li * ai) / den
    ki = (li * ar - (lr - 1.0) * ai) / den
    br, bi = b_re.astype(F32), b_im.astype(F32)
    bbr = kr[..., None] * br - ki[..., None] * bi
    bbi = kr[..., None] * bi + ki[..., None] * br

    gs = S5_GROUPS // S5_SLABS
    eye = jnp.eye(gs, dtype=F32)

    def state_row(zr, zi):
        zr = zr.reshape(S5_SLABS, S5_SLAB_HALF)
        zi = zi.reshape(S5_SLABS, S5_SLAB_HALF)
        return jnp.concatenate([zr, zi], axis=-1).reshape(S5_SW)

    def in_block(b):
        b4 = b.reshape(S5_SLABS, gs, S5_STATE, S5_GROUP)
        return jnp.einsum('sgpc,gh->sgchp', b4, eye).reshape(S5_SLABS, S5_SLAB_IN, S5_SLAB_HALF)

    def out_block(c):
        c4 = c.reshape(S5_SLABS, gs, S5_GROUP, S5_STATE)
        return jnp.einsum('sgcp,gh->shpgc', c4, eye).reshape(S5_SLABS, S5_SLAB_HALF, S5_SLAB_IN)

    bblk = jnp.concatenate([in_block(bbr), in_block(bbi)], axis=2)
    cblk = jnp.concatenate([out_block(c_re.astype(F32)), -out_block(c_im.astype(F32))], axis=1)
    return bblk.astype(BF16), cblk.astype(BF16), state_row(lr, li)[None]


POOL_HALO = 16


def _pool_kernel(x_ref, halo_ref, g_ref, w_ref, b_ref, sc_ref, o_ref, buf_ref):
    t = pl.program_id(1)
    x = x_ref[0]
    g = g_ref[...]
    h = _rms(x, g)
    hh = _rms(halo_ref[0], g)
    buf_ref[0:POOL_HALO] = jnp.where(t > 0, hh, 0.0)
    buf_ref[POOL_HALO:] = h
    tpos = t * TM + lax.broadcasted_iota(jnp.int32, (TM, POOL_WIDTH), 0) + 1
    outs = []
    for gi, w in enumerate(POOL_WINDOWS):
        lo = gi * POOL_WIDTH
        acc = h[:, lo:lo + POOL_WIDTH]
        for k in range(1, w):
            acc = acc + buf_ref[POOL_HALO - k:POOL_HALO - k + TM, lo:lo + POOL_WIDTH]
        mean = acc / jnp.minimum(tpos, w).astype(F32)
        outs.append(_dot((mean - h[:, lo:lo + POOL_WIDTH]).astype(BF16), w_ref[gi]))
    y = (jnp.concatenate(outs, axis=1) + b_ref[...]) * sc_ref[...]
    o_ref[0] = x + y


def _pool(x, g, w, b, sc):
    B, S, D = x.shape
    per = TM // POOL_HALO
    return pl.pallas_call(
        _pool_kernel,
        grid=(B, S // TM),
        in_specs=[
            pl.BlockSpec((1, TM, D), lambda b, t: (b, t, 0)),
            pl.BlockSpec((1, POOL_HALO, D), lambda b, t: (b, jnp.maximum(t * per - 1, 0), 0)),
            _resident(g.shape), _resident(w.shape), _resident(b.shape), _resident(sc.shape),
        ],
        out_specs=pl.BlockSpec((1, TM, D), lambda b, t: (b, t, 0)),
        out_shape=jax.ShapeDtypeStruct((B, S, D), F32),
        scratch_shapes=[pltpu.VMEM((POOL_HALO + TM, D), F32)],
        compiler_params=pltpu.CompilerParams(
            dimension_semantics=("parallel", "arbitrary"),
            vmem_limit_bytes=VMEM_LIMIT_BYTES),
        name="pool",
    )(x, x, g, w, b, sc)


def _to_segments(x):
    B, S, D = x.shape
    return x.reshape(B, S // S5_CHUNK, S5_SEGS, S5_TJ, D).transpose(0, 1, 3, 2, 4).reshape(B, S, D)


def _from_segments(x):
    B, S, D = x.shape
    return x.reshape(B, S // S5_CHUNK, S5_TJ, S5_SEGS, D).transpose(0, 1, 3, 2, 4).reshape(B, S, D)


def kernel(x, mix_norm_g, ffn_norm_g, final_norm_g, fox_w_in, fox_b_f, fox_w_out, s5_w_in, s5_a_re, s5_a_im, s5_log_dt, s5_b_re, s5_b_im, s5_c_re, s5_c_im, s5_d, s5_w_glu, pool_w, pool_b, pool_scale, ffn_w_gate_up, ffn_w_down):
    B, S, D = x.shape
    tri = jnp.triu(jnp.ones((TM, TM), F32)).astype(BF16)
    segmented = False
    for i in range(DEPTH):
        kind, j = i % 3, i // 3
        gm = mix_norm_g[i][None].astype(F32)
        gf = ffn_norm_g[i][None].astype(F32)
        wgu = ffn_w_gate_up[i].astype(BF16)
        wd = ffn_w_down[i].astype(BF16)
        final_g = final_norm_g[None].astype(F32) if i == DEPTH - 1 else None
        mix = None
        if kind == 0:
            if segmented:
                x, segmented = _from_segments(x), False
            w = fox_w_in[j]
            qkv, c = _fox_proj(x, gm, w[:, :3 * D].astype(BF16), w[:, 3 * D:].T.astype(BF16),
                               fox_b_f[j][:, None].astype(F32), tri)
            c = c.reshape(B, FOX_HEADS // 2, 2, S // TKS, TKS).transpose(0, 1, 3, 2, 4)
            o = _fox_attn(qkv, c)
            mix = (o.reshape(B * S, D), fox_w_out[j].astype(BF16))
        elif kind == 1:
            if not segmented:
                x, segmented = _to_segments(x), True
            bblk, cblk, lam = _s5_params(
                s5_a_re[j], s5_a_im[j], s5_log_dt[j], s5_b_re[j], s5_b_im[j], s5_c_re[j], s5_c_im[j])
            x = _s5(x, gm, s5_w_in[j].astype(BF16), bblk, cblk, lam,
                    s5_d[j][None].astype(F32), s5_w_glu[j].astype(BF16))
        else:
            if segmented:
                x, segmented = _from_segments(x), False
            x = _pool(x, gm, pool_w[j].astype(BF16), pool_b[j][None].astype(F32),
                      pool_scale[j][None].astype(F32))
        x = _ffn(x.reshape(B * S, D), gf, wgu, wd, mix=mix, final_g=final_g).reshape(B, S, D)
    if segmented:
        x = _from_segments(x)
    return x
```

```python
import functools
import math

import jax
import jax.numpy as jnp
from jax import lax
from jax.experimental import pallas as pl
from jax.experimental.pallas import tpu as pltpu

F32 = jnp.float32
BF16 = jnp.bfloat16

D_MODEL = 1024
DEPTH = 4
EPS = 1e-6
FOX_HEADS = 16
FOX_HEAD_DIM = 64
S5_GROUP = 16
S5_GROUPS = 64
S5_STATE = 64
POOL_WINDOWS = (2, 4, 8, 16)
POOL_WIDTH = 256
D_FF = 2816

LANES = 128
SUBLANES = 8
MXU_DIM = 256
VMEM_LIMIT_BYTES = 56 * 1024 * 1024

LOG2E = 1.4426950408889634
NEG_BIG = -1e30

TM = 512
FF_CHUNK = MXU_DIM
TQ = 1024
TK = 1024
S5_SEGS = SUBLANES
S5_TJ = 32
S5_CHUNK = S5_SEGS * S5_TJ
S5_SLABS = 4
S5_SLAB_IN = D_MODEL // S5_SLABS
S5_SLAB_HALF = (S5_GROUPS // S5_SLABS) * S5_STATE
S5_SLAB_W = 2 * S5_SLAB_HALF
S5_SW = S5_SLABS * S5_SLAB_W
S5_COLS_PER_STEP = 4


def _resident(shape):
    nd = len(shape)
    return pl.BlockSpec(shape, lambda *_: (0,) * nd, pipeline_mode=pl.Buffered(1))


def _rms(x, g):
    return x * lax.rsqrt(jnp.mean(x * x, axis=-1, keepdims=True) + EPS) * g


def _dot(a, b):
    return jnp.dot(a, b, preferred_element_type=F32)


def _fox_proj_kernel(x_ref, g_ref, wqkv_ref, wft_ref, bf_ref, tri_ref,
                     qkv_ref, c_ref, carry_ref):
    @pl.when(pl.program_id(1) == 0)
    def _():
        carry_ref[...] = jnp.zeros_like(carry_ref)

    h = _rms(x_ref[0], g_ref[...]).astype(BF16)
    qkv = _dot(h, wqkv_ref[...])
    q = qkv[:, :D_MODEL] * (FOX_HEAD_DIM ** -0.5 * LOG2E)
    qkv_ref[0, :, :D_MODEL] = q.astype(BF16)
    qkv_ref[0, :, D_MODEL:] = qkv[:, D_MODEL:].astype(BF16)

    z = lax.dot_general(wft_ref[...], h, (((1,), (1,)), ((), ())),
                        preferred_element_type=F32) + bf_ref[...]
    logf = jnp.minimum(z, 0.0) - jnp.log(1.0 + jnp.exp(-jnp.abs(z)))
    hi = logf.astype(BF16)
    r1 = logf - hi.astype(F32)
    mid = r1.astype(BF16)
    lo = (r1 - mid.astype(F32)).astype(BF16)
    tri = tri_ref[...]
    c = _dot(hi, tri) + _dot(mid, tri) + _dot(lo, tri) + carry_ref[:, 0:1]
    c_ref[0] = c * LOG2E
    carry_ref[...] = jnp.broadcast_to(c[:, TM - 1:TM], carry_ref.shape)


def _fox_proj(x, g, wqkv, wft, bf, tri):
    B, S, D = x.shape
    return pl.pallas_call(
        _fox_proj_kernel,
        grid=(B, S // TM),
        in_specs=[
            pl.BlockSpec((1, TM, D), lambda b, t: (b, t, 0)),
            _resident((1, D)),
            _resident((D, 3 * D)),
            _resident((FOX_HEADS, D)),
            _resident((FOX_HEADS, 1)),
            _resident((TM, TM)),
        ],
        out_specs=[
            pl.BlockSpec((1, TM, 3 * D), lambda b, t: (b, t, 0)),
            pl.BlockSpec((1, FOX_HEADS, TM), lambda b, t: (b, 0, t)),
        ],
        out_shape=[
            jax.ShapeDtypeStruct((B, S, 3 * D), BF16),
            jax.ShapeDtypeStruct((B, FOX_HEADS, S), F32),
        ],
        scratch_shapes=[pltpu.VMEM((FOX_HEADS, LANES), F32)],
        compiler_params=pltpu.CompilerParams(
            dimension_semantics=("parallel", "arbitrary"),
            vmem_limit_bytes=VMEM_LIMIT_BYTES),
        name="fox_proj",
    )(x, g, wqkv, wft, bf, tri)


def _fox_attn_kernel(q_ref, k_ref, v_ref, c_ref, o_ref,
                     qs_ref, va_ref, vb_ref, s0_ref, s1_ref, m_ref, alpha_ref, acc_ref):
    S = q_ref.shape[1]
    nq = S // TQ
    lane = lax.broadcasted_iota(jnp.int32, (TQ, LANES), 1)
    first = lane < FOX_HEAD_DIM

    def prep_v(i, c):
        r = pl.multiple_of(i * TK, TK)
        v = v_ref[0, pl.ds(r, TK), :].astype(F32)
        ln = lax.broadcasted_iota(jnp.int32, v.shape, 1)
        va = jnp.where(ln < FOX_HEAD_DIM, v, jnp.where(ln == FOX_HEAD_DIM, 1.0, 0.0))
        vb = jnp.where(ln >= FOX_HEAD_DIM, v, jnp.where(ln == 0, 1.0, 0.0))
        va_ref[pl.ds(r, TK), :] = va.astype(BF16)
        vb_ref[pl.ds(r, TK), :] = vb.astype(BF16)
        return c

    lax.fori_loop(0, S // TK, prep_v, 0)

    def prep_q(qb, slot):
        q = q_ref[0, pl.ds(pl.multiple_of(qb * TQ, TQ), TQ), :]
        zero = jnp.zeros_like(q)
        qs_ref[slot, 0:TQ] = jnp.where(first, q, zero)
        qs_ref[slot, TQ:2 * TQ] = jnp.where(first, zero, q)

    def logits(qslot, ci, diag_off):
        k = k_ref[0, pl.ds(pl.multiple_of(ci * TK, TK), TK), :]
        s = lax.dot_general(qs_ref[qslot], k, (((1,), (1,)), ((), ())),
                            preferred_element_type=F32)
        ck = c_ref[0, 0, ci]
        s = s - jnp.concatenate(
            [jnp.broadcast_to(ck[0:1], (TQ, TK)),
             jnp.broadcast_to(ck[1:2], (TQ, TK))], axis=0)
        if diag_off is not None:
            row = lax.broadcasted_iota(jnp.int32, s.shape, 0) & (TQ - 1)
            col = lax.broadcasted_iota(jnp.int32, s.shape, 1)
            s = jnp.where(col <= row + diag_off, s, NEG_BIG)
        return s

    def step_on(cur_ref, nxt_ref, ci, nxt_args, fresh):
        s_next = logits(*nxt_args)
        nxt_ref[...] = s_next
        s = cur_ref[...]
        m_cur = m_ref[...]
        alpha = alpha_ref[...]
        mx = jnp.max(s_next, axis=1, keepdims=True)
        if fresh:
            m_ref[...] = jnp.broadcast_to(mx, m_cur.shape)
            alpha_ref[...] = jnp.zeros_like(alpha)
        else:
            m_next = jnp.maximum(m_cur, mx)
            m_ref[...] = m_next
            alpha_ref[...] = jnp.exp2(m_cur - m_next)
        p = jnp.exp2(s - pltpu.repeat(m_cur, TK // LANES, axis=1)).astype(BF16)
        r = pl.multiple_of(ci * TK, TK)
        pv = jnp.concatenate([_dot(p[:TQ], va_ref[pl.ds(r, TK), :]),
                              _dot(p[TQ:], vb_ref[pl.ds(r, TK), :])], axis=0)
        acc_ref[...] = alpha * acc_ref[...] + pv

    def step(cnt, ci, nxt_args, fresh):
        qslot_n, ci_n, off = nxt_args

        def run(cur_ref, nxt_ref):
            if off is None:
                step_on(cur_ref, nxt_ref, ci, nxt_args, fresh)
            else:
                lax.cond(off < TK - 1,
                         lambda: step_on(cur_ref, nxt_ref, ci, nxt_args, fresh),
                         lambda: step_on(cur_ref, nxt_ref, ci, (qslot_n, ci_n, None), fresh))

        lax.cond((cnt & 1) == 0, lambda: run(s0_ref, s1_ref), lambda: run(s1_ref, s0_ref))
        return cnt + 1

    def q_block(qb, cnt):
        qslot = qb & 1
        prep_q(jnp.minimum(qb + 1, nq - 1), 1 - qslot)
        acc_ref[...] = jnp.zeros_like(acc_ref)
        last = (qb * TQ) // TK

        def inner(ci, cnt):
            return step(cnt, ci, (qslot, ci + 1, None), False)

        cnt = lax.fori_loop(0, last - 1, inner, cnt)

        def before_diag(ci, cnt):
            return step(cnt, ci, (qslot, ci + 1, qb * TQ - last * TK), False)

        cnt = lax.fori_loop(jnp.maximum(last - 1, 0), last, before_diag, cnt)
        cnt = step(cnt, last, (1 - qslot, 0, (qb + 1) * TQ), True)

        acc = acc_ref[...]
        la = jnp.broadcast_to(acc[:TQ, FOX_HEAD_DIM:FOX_HEAD_DIM + 1], (TQ, LANES))
        lb = jnp.broadcast_to(acc[TQ:, 0:1], (TQ, LANES))
        o = jnp.where(first, acc[:TQ] / la, acc[TQ:] / lb)
        o_ref[0, pl.ds(pl.multiple_of(qb * TQ, TQ), TQ), :] = o.astype(BF16)
        return cnt

    prep_q(0, 0)
    s0 = logits(0, 0, 0)
    s0_ref[...] = s0
    m_ref[...] = jnp.broadcast_to(jnp.max(s0, axis=1, keepdims=True), m_ref.shape)
    alpha_ref[...] = jnp.zeros_like(alpha_ref)
    lax.fori_loop(0, nq, q_block, 0)


def _fox_attn(qkv, c):
    B, S, _ = qkv.shape
    pairs = FOX_HEADS // 2
    blk = lambda off: pl.BlockSpec((1, S, LANES), lambda b, p: (b, 0, off + p))
    return pl.pallas_call(
        _fox_attn_kernel,
        grid=(B, pairs),
        in_specs=[
            blk(0), blk(pairs), blk(2 * pairs),
            pl.BlockSpec((1, 1, S // TK, 2, TK), lambda b, p: (b, p, 0, 0, 0)),
        ],
        out_specs=pl.BlockSpec((1, S, LANES), lambda b, p: (b, 0, p)),
        out_shape=jax.ShapeDtypeStruct((B, S, D_MODEL), BF16),
        scratch_shapes=[
            pltpu.VMEM((2, 2 * TQ, LANES), BF16),
            pltpu.VMEM((S, LANES), BF16),
            pltpu.VMEM((S, LANES), BF16),
            pltpu.VMEM((2 * TQ, TK), F32),
            pltpu.VMEM((2 * TQ, TK), F32),
            pltpu.VMEM((2 * TQ, LANES), F32),
            pltpu.VMEM((2 * TQ, LANES), F32),
            pltpu.VMEM((2 * TQ, LANES), F32),
        ],
        compiler_params=pltpu.CompilerParams(
            dimension_semantics=("parallel", "parallel"),
            vmem_limit_bytes=VMEM_LIMIT_BYTES),
        name="fox_attn",
    )(qkv, qkv, qkv, c)


def _ffn_kernel(*refs, has_mix, final_norm):
    refs = list(refs)
    x_ref = refs.pop(0)
    if has_mix:
        a_ref = refs.pop(0)
        wmix_ref = refs.pop(0)
    g_ref, wgu_ref, wd_ref = refs[:3]
    refs = refs[3:]
    if final_norm:
        gf_ref = refs.pop(0)
    o_ref, act_ref = refs

    x = x_ref[...]
    if has_mix:
        x = x + _dot(a_ref[...], wmix_ref[...])
    h = _rms(x, g_ref[...]).astype(BF16)
    for c in range(D_FF // FF_CHUNK):
        lo = c * FF_CHUNK
        gate = _dot(h, wgu_ref[:, lo:lo + FF_CHUNK])
        up = _dot(h, wgu_ref[:, D_FF + lo:D_FF + lo + FF_CHUNK])
        act_ref[:, lo:lo + FF_CHUNK] = (gate * jax.nn.sigmoid(gate) * up).astype(BF16)
    y = x + _dot(act_ref[...], wd_ref[...])
    if final_norm:
        y = _rms(y, gf_ref[...])
    o_ref[...] = y


def _ffn(x2, g, wgu, wd, mix=None, final_g=None):
    N, D = x2.shape
    row = pl.BlockSpec((TM, D), lambda i: (i, 0))
    args, specs = [x2], [row]
    if mix is not None:
        a2, wmix = mix
        args += [a2, wmix]
        specs += [row, _resident(wmix.shape)]
    args += [g, wgu, wd]
    specs += [_resident(g.shape), _resident(wgu.shape), _resident(wd.shape)]
    if final_g is not None:
        args.append(final_g)
        specs.append(_resident(final_g.shape))
    return pl.pallas_call(
        functools.partial(_ffn_kernel, has_mix=mix is not None,
                          final_norm=final_g is not None),
        grid=(N // TM,),
        in_specs=specs,
        out_specs=row,
        out_shape=jax.ShapeDtypeStruct((N, D), F32),
        scratch_shapes=[pltpu.VMEM((TM, D_FF), BF16)],
        compiler_params=pltpu.CompilerParams(
            dimension_semantics=("parallel",),
            vmem_limit_bytes=VMEM_LIMIT_BYTES),
        name="ffn",
    )(*args)


def _s5_kernel(x_ref, g_ref, win_ref, bblk_ref, cblk_ref, lam_ref,
               d_ref, wglu_ref, o_ref, st_ref, carry_ref):
    @pl.when(pl.program_id(1) == 0)
    def _():
        carry_ref[...] = jnp.zeros_like(carry_ref)

    x = x_ref[0]
    h = _rms(x, g_ref[...]).astype(BF16)
    u = _dot(h, win_ref[...])
    ub = u.astype(BF16)
    for sl in range(S5_SLABS):
        st_ref[:, sl * S5_SLAB_W:(sl + 1) * S5_SLAB_W] = _dot(
            ub[:, sl * S5_SLAB_IN:(sl + 1) * S5_SLAB_IN], bblk_ref[sl])

    sub = lax.broadcasted_iota(jnp.int32, (SUBLANES, LANES), 0)
    vshape = (SUBLANES, LANES)
    blocks_per_slab = S5_SLAB_HALF // LANES
    n_blocks = S5_SLABS * blocks_per_slab

    def cmul(ar, ai, br, bi):
        return ar * br - ai * bi, ar * bi + ai * br

    for grp in range(n_blocks // S5_COLS_PER_STEP):
        offs = []
        for b in range(grp * S5_COLS_PER_STEP, (grp + 1) * S5_COLS_PER_STEP):
            re = (b // blocks_per_slab) * S5_SLAB_W + (b % blocks_per_slab) * LANES
            offs.append((re, re + S5_SLAB_HALF))
        lam = [(jnp.broadcast_to(lam_ref[:, re:re + LANES], vshape),
                jnp.broadcast_to(lam_ref[:, im:im + LANES], vshape)) for re, im in offs]

        def scan_body(j, state):
            r = pl.multiple_of(j * SUBLANES, SUBLANES)
            new = []
            for (re, im), (lr, li), (sr, si) in zip(offs, lam, state):
                pr, pi = cmul(lr, li, sr, si)
                nr = pr + st_ref[pl.ds(r, SUBLANES), re:re + LANES]
                ni = pi + st_ref[pl.ds(r, SUBLANES), im:im + LANES]
                st_ref[pl.ds(r, SUBLANES), re:re + LANES] = nr
                st_ref[pl.ds(r, SUBLANES), im:im + LANES] = ni
                new.append((nr, ni))
            return tuple(new)

        init = tuple((carry_ref[:, re:re + LANES], carry_ref[:, im:im + LANES])
                     for re, im in offs)
        fin = lax.fori_loop(0, S5_TJ, scan_body, init, unroll=4)

        carries = []
        for (re, im), (lr, li), (fr, fi) in zip(offs, lam, fin):
            for _ in range(S5_TJ.bit_length() - 1):
                lr, li = cmul(lr, li, lr, li)
            er, ei = fr, fi
            for k in range(1, S5_SEGS):
                pr, pi = cmul(lr, li, pltpu.roll(er, 1, 0), pltpu.roll(ei, 1, 0))
                er = jnp.where(sub == k, fr + pr, er)
                ei = jnp.where(sub == k, fi + pi, ei)
            sr, si = pltpu.roll(er, 1, 0), pltpu.roll(ei, 1, 0)
            carry_ref[:, re:re + LANES] = jnp.where(sub == 0, sr, 0.0)
            carry_ref[:, im:im + LANES] = jnp.where(sub == 0, si, 0.0)
            carries.append((jnp.where(sub == 0, 0.0, sr), jnp.where(sub == 0, 0.0, si)))

        def fix_body(j, f):
            r = pl.multiple_of(j * SUBLANES, SUBLANES)
            new = []
            for (re, im), (lr, li), (fr, fi) in zip(offs, lam, f):
                fr, fi = cmul(lr, li, fr, fi)
                st_ref[pl.ds(r, SUBLANES), re:re + LANES] += fr
                st_ref[pl.ds(r, SUBLANES), im:im + LANES] += fi
                new.append((fr, fi))
            return tuple(new)

        lax.fori_loop(0, S5_TJ, fix_body, tuple(carries), unroll=4)

    ys = [_dot(st_ref[:, sl * S5_SLAB_W:(sl + 1) * S5_SLAB_W].astype(BF16), cblk_ref[sl])
          for sl in range(S5_SLABS)]
    y = jnp.concatenate(ys, axis=1) + d_ref[...] * u
    gl = jax.nn.gelu(y).astype(BF16)
    vg = _dot(gl, wglu_ref[...])
    o_ref[0] = x + vg[:, :D_MODEL] * jax.nn.sigmoid(vg[:, D_MODEL:])


def _s5(xp, g, win, bblk, cblk, lam, d, wglu):
    B, S, D = xp.shape
    return pl.pallas_call(
        _s5_kernel,
        grid=(B, S // S5_CHUNK),
        in_specs=[
            pl.BlockSpec((1, S5_CHUNK, D), lambda b, c: (b, c, 0)),
            _resident(g.shape), _resident(win.shape), _resident(bblk.shape),
            _resident(cblk.shape), _resident(lam.shape),
            _resident(d.shape), _resident(wglu.shape),
        ],
        out_specs=pl.BlockSpec((1, S5_CHUNK, D), lambda b, c: (b, c, 0)),
        out_shape=jax.ShapeDtypeStruct((B, S, D), F32),
        scratch_shapes=[
            pltpu.VMEM((S5_CHUNK, S5_SW), F32),
            pltpu.VMEM((SUBLANES, S5_SW), F32),
        ],
        compiler_params=pltpu.CompilerParams(
            dimension_semantics=("parallel", "arbitrary"),
            vmem_limit_bytes=VMEM_LIMIT_BYTES),
        name="s5",
    )(xp, g, win, bblk, cblk, lam, d, wglu)


def _s5_params(a_re, a_im, log_dt, b_re, b_im, c_re, c_im):
    ar, ai = a_re.astype(F32), a_im.astype(F32)
    dt = jnp.exp(log_dt.astype(F32))[:, None]
    mag = jnp.exp(ar * dt)
    lr, li = mag * jnp.cos(ai * dt), mag * jnp.sin(ai * dt)
    den = ar * ar + ai * ai
    kr = ((lr - 1.0) * ar + li * ai) / den
    ki = (li * ar - (lr - 1.0) * ai) / den
    br, bi = b_re.astype(F32), b_im.astype(F32)
    bbr = kr[..., None] * br - ki[..., None] * bi
    bbi = kr[..., None] * bi + ki[..., None] * br

    gs = S5_GROUPS // S5_SLABS
    eye = jnp.eye(gs, dtype=F32)

    def state_row(zr, zi):
        zr = zr.reshape(S5_SLABS, S5_SLAB_HALF)
        zi = zi.reshape(S5_SLABS, S5_SLAB_HALF)
        return jnp.concatenate([zr, zi], axis=-1).reshape(S5_SW)

    def in_block(b):
        b4 = b.reshape(S5_SLABS, gs, S5_STATE, S5_GROUP)
        return jnp.einsum('sgpc,gh->sgchp', b4, eye).reshape(S5_SLABS, S5_SLAB_IN, S5_SLAB_HALF)

    def out_block(c):
        c4 = c.reshape(S5_SLABS, gs, S5_GROUP, S5_STATE)
        return jnp.einsum('sgcp,gh->shpgc', c4, eye).reshape(S5_SLABS, S5_SLAB_HALF, S5_SLAB_IN)

    bblk = jnp.concatenate([in_block(bbr), in_block(bbi)], axis=2)
    cblk = jnp.concatenate([out_block(c_re.astype(F32)), -out_block(c_im.astype(F32))], axis=1)
    return bblk.astype(BF16), cblk.astype(BF16), state_row(lr, li)[None]


POOL_HALO = 16


def _pool_kernel(x_ref, halo_ref, g_ref, w_ref, b_ref, sc_ref, o_ref, buf_ref):
    t = pl.program_id(1)
    x = x_ref[0]
    g = g_ref[...]
    h = _rms(x, g)
    hh = _rms(halo_ref[0], g)
    buf_ref[0:POOL_HALO] = jnp.where(t > 0, hh, 0.0)
    buf_ref[POOL_HALO:] = h
    tpos = t * TM + lax.broadcasted_iota(jnp.int32, (TM, POOL_WIDTH), 0) + 1
    outs = []
    for gi, w in enumerate(POOL_WINDOWS):
        lo = gi * POOL_WIDTH
        acc = h[:, lo:lo + POOL_WIDTH]
        for k in range(1, w):
            acc = acc + buf_ref[POOL_HALO - k:POOL_HALO - k + TM, lo:lo + POOL_WIDTH]
        mean = acc / jnp.minimum(tpos, w).astype(F32)
        outs.append(_dot((mean - h[:, lo:lo + POOL_WIDTH]).astype(BF16), w_ref[gi]))
    y = (jnp.concatenate(outs, axis=1) + b_ref[...]) * sc_ref[...]
    o_ref[0] = x + y


def _pool(x, g, w, b, sc):
    B, S, D = x.shape
    per = TM // POOL_HALO
    return pl.pallas_call(
        _pool_kernel,
        grid=(B, S // TM),
        in_specs=[
            pl.BlockSpec((1, TM, D), lambda b, t: (b, t, 0)),
            pl.BlockSpec((1, POOL_HALO, D), lambda b, t: (b, jnp.maximum(t * per - 1, 0), 0)),
            _resident(g.shape), _resident(w.shape), _resident(b.shape), _resident(sc.shape),
        ],
        out_specs=pl.BlockSpec((1, TM, D), lambda b, t: (b, t, 0)),
        out_shape=jax.ShapeDtypeStruct((B, S, D), F32),
        scratch_shapes=[pltpu.VMEM((POOL_HALO + TM, D), F32)],
        compiler_params=pltpu.CompilerParams(
            dimension_semantics=("parallel", "arbitrary"),
            vmem_limit_bytes=VMEM_LIMIT_BYTES),
        name="pool",
    )(x, x, g, w, b, sc)


def _to_segments(x):
    B, S, D = x.shape
    return x.reshape(B, S // S5_CHUNK, S5_SEGS, S5_TJ, D).transpose(0, 1, 3, 2, 4).reshape(B, S, D)


def _from_segments(x):
    B, S, D = x.shape
    return x.reshape(B, S // S5_CHUNK, S5_TJ, S5_SEGS, D).transpose(0, 1, 3, 2, 4).reshape(B, S, D)


def kernel(x, mix_norm_g, ffn_norm_g, final_norm_g, fox_w_in, fox_b_f, fox_w_out, s5_w_in, s5_a_re, s5_a_im, s5_log_dt, s5_b_re, s5_b_im, s5_c_re, s5_c_im, s5_d, s5_w_glu, pool_w, pool_b, pool_scale, ffn_w_gate_up, ffn_w_down):
    B, S, D = x.shape
    tri = jnp.triu(jnp.ones((TM, TM), F32)).astype(BF16)
    segmented = False
    for i in range(DEPTH):
        kind, j = i % 3, i // 3
        gm = mix_norm_g[i][None].astype(F32)
        gf = ffn_norm_g[i][None].astype(F32)
        wgu = ffn_w_gate_up[i].astype(BF16)
        wd = ffn_w_down[i].astype(BF16)
        final_g = final_norm_g[None].astype(F32) if i == DEPTH - 1 else None
        mix = None
        if kind == 0:
            if segmented:
                x, segmented = _from_segments(x), False
            w = fox_w_in[j]
            qkv, c = _fox_proj(x, gm, w[:, :3 * D].astype(BF16), w[:, 3 * D:].T.astype(BF16),
                               fox_b_f[j][:, None].astype(F32), tri)
            c = c.reshape(B, FOX_HEADS // 2, 2, S // TK, TK).transpose(0, 1, 3, 2, 4)
            o = _fox_attn(qkv, c)
            mix = (o.reshape(B * S, D), fox_w_out[j].astype(BF16))
        elif kind == 1:
            if not segmented:
                x, segmented = _to_segments(x), True
            bblk, cblk, lam = _s5_params(
                s5_a_re[j], s5_a_im[j], s5_log_dt[j], s5_b_re[j], s5_b_im[j], s5_c_re[j], s5_c_im[j])
            x = _s5(x, gm, s5_w_in[j].astype(BF16), bblk, cblk, lam,
                    s5_d[j][None].astype(F32), s5_w_glu[j].astype(BF16))
        else:
            if segmented:
                x, segmented = _from_segments(x), False
            x = _pool(x, gm, pool_w[j].astype(BF16), pool_b[j][None].astype(F32),
                      pool_scale[j][None].astype(F32))
        x = _ffn(x.reshape(B * S, D), gf, wgu, wd, mix=mix, final_g=final_g).reshape(B, S, D)
    if segmented:
        x = _from_segments(x)
    return x
```

```python
import functools
import math

import jax
import jax.numpy as jnp
from jax import lax
from jax.experimental import pallas as pl
from jax.experimental.pallas import tpu as pltpu

F32 = jnp.float32
BF16 = jnp.bfloat16

D_MODEL = 1024
DEPTH = 4
EPS = 1e-6
FOX_HEADS = 16
FOX_HEAD_DIM = 64
S5_GROUP = 16
S5_GROUPS = 64
S5_STATE = 64
POOL_WINDOWS = (2, 4, 8, 16)
POOL_WIDTH = 256
D_FF = 2816

LANES = 128
SUBLANES = 8
MXU_DIM = 256
VMEM_LIMIT_BYTES = 56 * 1024 * 1024

LOG2E = 1.4426950408889634
NEG_BIG = -1e30

TM = 512
FF_CHUNK = MXU_DIM
TQ = 1024
TK = 1024
S5_SEGS = SUBLANES
S5_TJ = 32
S5_CHUNK = S5_SEGS * S5_TJ
S5_SLABS = 4
S5_SLAB_IN = D_MODEL // S5_SLABS
S5_SLAB_HALF = (S5_GROUPS // S5_SLABS) * S5_STATE
S5_SLAB_W = 2 * S5_SLAB_HALF
S5_SW = S5_SLABS * S5_SLAB_W
S5_COLS_PER_STEP = 4


def _resident(shape):
    nd = len(shape)
    return pl.BlockSpec(shape, lambda *_: (0,) * nd, pipeline_mode=pl.Buffered(1))


def _layer(arr, i):
    nd = arr.ndim
    return pl.BlockSpec((None,) + arr.shape[1:], lambda *_: (i,) + (0,) * (nd - 1),
                        pipeline_mode=pl.Buffered(1))


def _rms(x, g):
    return x * lax.rsqrt(jnp.mean(x * x, axis=-1, keepdims=True) + EPS) * g


def _dot(a, b):
    return jnp.dot(a, b, preferred_element_type=F32)


def _fox_proj_kernel(x_ref, g_ref, wqkv_ref, wft_ref, bf_ref, tri_ref,
                     qkv_ref, c_ref, carry_ref):
    @pl.when(pl.program_id(1) == 0)
    def _():
        carry_ref[...] = jnp.zeros_like(carry_ref)

    h = _rms(x_ref[0], g_ref[...]).astype(BF16)
    qkv = _dot(h, wqkv_ref[:, :3 * D_MODEL])
    q = qkv[:, :D_MODEL] * (FOX_HEAD_DIM ** -0.5 * LOG2E)
    qkv_ref[0, :, :D_MODEL] = q.astype(BF16)
    qkv_ref[0, :, D_MODEL:] = qkv[:, D_MODEL:].astype(BF16)

    z = lax.dot_general(wft_ref[...], h, (((1,), (1,)), ((), ())),
                        preferred_element_type=F32) + bf_ref[...]
    logf = jnp.minimum(z, 0.0) - jnp.log(1.0 + jnp.exp(-jnp.abs(z)))
    hi = logf.astype(BF16)
    r1 = logf - hi.astype(F32)
    mid = r1.astype(BF16)
    lo = (r1 - mid.astype(F32)).astype(BF16)
    tri = tri_ref[...]
    c = _dot(hi, tri) + _dot(mid, tri) + _dot(lo, tri) + carry_ref[:, 0:1]
    c_ref[0] = c * LOG2E
    carry_ref[...] = jnp.broadcast_to(c[:, TM - 1:TM], carry_ref.shape)


def _fox_proj(x, g, i, win, wft, bf, j, tri):
    B, S, D = x.shape
    return pl.pallas_call(
        _fox_proj_kernel,
        grid=(B, S // TM),
        in_specs=[
            pl.BlockSpec((1, TM, D), lambda b, t: (b, t, 0)),
            _layer(g, i), _layer(win, j), _layer(wft, j), _layer(bf, j),
            _resident((TM, TM)),
        ],
        out_specs=[
            pl.BlockSpec((1, TM, 3 * D), lambda b, t: (b, t, 0)),
            pl.BlockSpec((1, FOX_HEADS, TM), lambda b, t: (b, 0, t)),
        ],
        out_shape=[
            jax.ShapeDtypeStruct((B, S, 3 * D), BF16),
            jax.ShapeDtypeStruct((B, FOX_HEADS, S), F32),
        ],
        scratch_shapes=[pltpu.VMEM((FOX_HEADS, LANES), F32)],
        compiler_params=pltpu.CompilerParams(
            dimension_semantics=("parallel", "arbitrary"),
            vmem_limit_bytes=VMEM_LIMIT_BYTES),
        name="fox_proj",
    )(x, g, win, wft, bf, tri)


def _fox_attn_kernel(q_ref, k_ref, v_ref, c_ref, o_ref,
                     qs_ref, va_ref, vb_ref, s0_ref, s1_ref, m_ref, alpha_ref, acc_ref):
    S = q_ref.shape[1]
    nq = S // TQ
    lane = lax.broadcasted_iota(jnp.int32, (TQ, LANES), 1)
    first = lane < FOX_HEAD_DIM

    def prep_v(i, c):
        r = pl.multiple_of(i * TK, TK)
        v = v_ref[0, pl.ds(r, TK), :].astype(F32)
        ln = lax.broadcasted_iota(jnp.int32, v.shape, 1)
        va = jnp.where(ln < FOX_HEAD_DIM, v, jnp.where(ln == FOX_HEAD_DIM, 1.0, 0.0))
        vb = jnp.where(ln >= FOX_HEAD_DIM, v, jnp.where(ln == 0, 1.0, 0.0))
        va_ref[pl.ds(r, TK), :] = va.astype(BF16)
        vb_ref[pl.ds(r, TK), :] = vb.astype(BF16)
        return c

    lax.fori_loop(0, S // TK, prep_v, 0)

    def prep_q(qb, slot):
        q = q_ref[0, pl.ds(pl.multiple_of(qb * TQ, TQ), TQ), :]
        zero = jnp.zeros_like(q)
        qs_ref[slot, 0:TQ] = jnp.where(first, q, zero)
        qs_ref[slot, TQ:2 * TQ] = jnp.where(first, zero, q)

    def logits(qslot, ci, diag_off):
        k = k_ref[0, pl.ds(pl.multiple_of(ci * TK, TK), TK), :]
        s = lax.dot_general(qs_ref[qslot], k, (((1,), (1,)), ((), ())),
                            preferred_element_type=F32)
        ck = c_ref[0, 0, ci]
        s = s - jnp.concatenate(
            [jnp.broadcast_to(ck[0:1], (TQ, TK)),
             jnp.broadcast_to(ck[1:2], (TQ, TK))], axis=0)
        if diag_off is not None:
            row = lax.broadcasted_iota(jnp.int32, s.shape, 0) & (TQ - 1)
            col = lax.broadcasted_iota(jnp.int32, s.shape, 1)
            s = jnp.where(col <= row + diag_off, s, NEG_BIG)
        return s

    def step_on(cur_ref, nxt_ref, ci, nxt_args, fresh):
        s_next = logits(*nxt_args)
        nxt_ref[...] = s_next
        s = cur_ref[...]
        m_cur = m_ref[...]
        alpha = alpha_ref[...]
        mx = jnp.max(s_next, axis=1, keepdims=True)
        if fresh:
            m_ref[...] = jnp.broadcast_to(mx, m_cur.shape)
            alpha_ref[...] = jnp.zeros_like(alpha)
        else:
            m_next = jnp.maximum(m_cur, mx)
            m_ref[...] = m_next
            alpha_ref[...] = jnp.exp2(m_cur - m_next)
        p = jnp.exp2(s - pltpu.repeat(m_cur, TK // LANES, axis=1)).astype(BF16)
        r = pl.multiple_of(ci * TK, TK)
        pv = jnp.concatenate([_dot(p[:TQ], va_ref[pl.ds(r, TK), :]),
                              _dot(p[TQ:], vb_ref[pl.ds(r, TK), :])], axis=0)
        acc_ref[...] = alpha * acc_ref[...] + pv

    def step(cnt, ci, nxt_args, fresh):
        qslot_n, ci_n, off = nxt_args

        def run(cur_ref, nxt_ref):
            if off is None:
                step_on(cur_ref, nxt_ref, ci, nxt_args, fresh)
            else:
                lax.cond(off < TK - 1,
                         lambda: step_on(cur_ref, nxt_ref, ci, nxt_args, fresh),
                         lambda: step_on(cur_ref, nxt_ref, ci, (qslot_n, ci_n, None), fresh))

        lax.cond((cnt & 1) == 0, lambda: run(s0_ref, s1_ref), lambda: run(s1_ref, s0_ref))
        return cnt + 1

    def q_block(qb, cnt):
        qslot = qb & 1
        prep_q(jnp.minimum(qb + 1, nq - 1), 1 - qslot)
        acc_ref[...] = jnp.zeros_like(acc_ref)
        last = (qb * TQ) // TK

        def inner(ci, cnt):
            return step(cnt, ci, (qslot, ci + 1, None), False)

        cnt = lax.fori_loop(0, last - 1, inner, cnt)

        def before_diag(ci, cnt):
            return step(cnt, ci, (qslot, ci + 1, qb * TQ - last * TK), False)

        cnt = lax.fori_loop(jnp.maximum(last - 1, 0), last, before_diag, cnt)
        cnt = step(cnt, last, (1 - qslot, 0, (qb + 1) * TQ), True)

        acc = acc_ref[...]
        la = jnp.broadcast_to(acc[:TQ, FOX_HEAD_DIM:FOX_HEAD_DIM + 1], (TQ, LANES))
        lb = jnp.broadcast_to(acc[TQ:, 0:1], (TQ, LANES))
        o = jnp.where(first, acc[:TQ] / la, acc[TQ:] / lb)
        o_ref[0, pl.ds(pl.multiple_of(qb * TQ, TQ), TQ), :] = o.astype(BF16)
        return cnt

    prep_q(0, 0)
    s0 = logits(0, 0, 0)
    s0_ref[...] = s0
    m_ref[...] = jnp.broadcast_to(jnp.max(s0, axis=1, keepdims=True), m_ref.shape)
    alpha_ref[...] = jnp.zeros_like(alpha_ref)
    lax.fori_loop(0, nq, q_block, 0)


def _fox_attn(qkv, c):
    B, S, _ = qkv.shape
    pairs = FOX_HEADS // 2
    blk = lambda off: pl.BlockSpec((1, S, LANES), lambda b, p: (b, 0, off + p))
    return pl.pallas_call(
        _fox_attn_kernel,
        grid=(B, pairs),
        in_specs=[
            blk(0), blk(pairs), blk(2 * pairs),
            pl.BlockSpec((1, 1, S // TK, 2, TK), lambda b, p: (b, p, 0, 0, 0)),
        ],
        out_specs=pl.BlockSpec((1, S, LANES), lambda b, p: (b, 0, p)),
        out_shape=jax.ShapeDtypeStruct((B, S, D_MODEL), BF16),
        scratch_shapes=[
            pltpu.VMEM((2, 2 * TQ, LANES), BF16),
            pltpu.VMEM((S, LANES), BF16),
            pltpu.VMEM((S, LANES), BF16),
            pltpu.VMEM((2 * TQ, TK), F32),
            pltpu.VMEM((2 * TQ, TK), F32),
            pltpu.VMEM((2 * TQ, LANES), F32),
            pltpu.VMEM((2 * TQ, LANES), F32),
            pltpu.VMEM((2 * TQ, LANES), F32),
        ],
        compiler_params=pltpu.CompilerParams(
            dimension_semantics=("parallel", "parallel"),
            vmem_limit_bytes=VMEM_LIMIT_BYTES),
        name="fox_attn",
    )(qkv, qkv, qkv, c)


def _ffn_kernel(*refs, has_mix, final_norm, seg_in, seg_out):
    refs = list(refs)
    x_ref = refs.pop(0)
    if has_mix:
        a_ref = refs.pop(0)
        wmix_ref = refs.pop(0)
    g_ref, wgu_ref, wd_ref = refs[:3]
    refs = refs[3:]
    if final_norm:
        gf_ref = refs.pop(0)
    o_ref, act_ref = refs
    seg_tiles = [(ch, sg) for ch in range(TM // S5_CHUNK) for sg in range(S5_SEGS)]

    if seg_in:
        x = jnp.concatenate([x_ref[ch, :, sg * D_MODEL:(sg + 1) * D_MODEL]
                             for ch, sg in seg_tiles], axis=0)
    else:
        x = x_ref[...]
    if has_mix:
        x = x + _dot(a_ref[...], wmix_ref[...])
    h = _rms(x, g_ref[...]).astype(BF16)
    for c in range(D_FF // FF_CHUNK):
        lo = c * FF_CHUNK
        gate = _dot(h, wgu_ref[:, lo:lo + FF_CHUNK])
        up = _dot(h, wgu_ref[:, D_FF + lo:D_FF + lo + FF_CHUNK])
        act_ref[:, lo:lo + FF_CHUNK] = (gate * jax.nn.sigmoid(gate) * up).astype(BF16)
    y = x + _dot(act_ref[...], wd_ref[...])
    if final_norm:
        y = _rms(y, gf_ref[...])
    if seg_out:
        for t, (ch, sg) in enumerate(seg_tiles):
            o_ref[ch, :, sg * D_MODEL:(sg + 1) * D_MODEL] = y[t * S5_TJ:(t + 1) * S5_TJ]
    else:
        o_ref[...] = y


def _ffn(x, n_tokens, g, wgu, wd, i, mix=None, final_g=None, seg_in=False, seg_out=False):
    N, D = n_tokens, D_MODEL
    row = pl.BlockSpec((TM, D), lambda t: (t, 0))
    seg = pl.BlockSpec((TM // S5_CHUNK, S5_TJ, S5_SEGS * D), lambda t: (t, 0, 0))
    seg_shape = (N // S5_CHUNK, S5_TJ, S5_SEGS * D)
    args, specs = [x], [seg if seg_in else row]
    if mix is not None:
        a2, wmix, j = mix
        args += [a2, wmix]
        specs += [row, _layer(wmix, j)]
    args += [g, wgu, wd]
    specs += [_layer(g, i), _layer(wgu, i), _layer(wd, i)]
    if final_g is not None:
        args.append(final_g)
        specs.append(_resident(final_g.shape))
    return pl.pallas_call(
        functools.partial(_ffn_kernel, has_mix=mix is not None,
                          final_norm=final_g is not None, seg_in=seg_in, seg_out=seg_out),
        grid=(N // TM,),
        in_specs=specs,
        out_specs=seg if seg_out else row,
        out_shape=jax.ShapeDtypeStruct(seg_shape if seg_out else (N, D), F32),
        scratch_shapes=[pltpu.VMEM((TM, D_FF), BF16)],
        compiler_params=pltpu.CompilerParams(
            dimension_semantics=("parallel",),
            vmem_limit_bytes=VMEM_LIMIT_BYTES),
        name="ffn",
    )(*args)


def _s5_kernel(x_ref, g_ref, win_ref, bblk_ref, cblk_ref, lam_ref,
               d_ref, wglu_ref, o_ref, st_ref, carry_ref):
    @pl.when(pl.program_id(1) == 0)
    def _():
        carry_ref[...] = jnp.zeros_like(carry_ref)

    x = x_ref[0]
    h = _rms(x, g_ref[...]).astype(BF16)
    u = _dot(h, win_ref[...])
    ub = u.astype(BF16)

    sub = lax.broadcasted_iota(jnp.int32, (SUBLANES, LANES), 0)
    vshape = (SUBLANES, LANES)
    blocks_per_slab = S5_SLAB_HALF // LANES
    n_blocks = S5_SLABS * blocks_per_slab

    def cmul(ar, ai, br, bi):
        return ar * br - ai * bi, ar * bi + ai * br

    groups_per_slab = blocks_per_slab // S5_COLS_PER_STEP
    ys = []
    for grp in range(n_blocks // S5_COLS_PER_STEP):
        sl = grp // groups_per_slab
        if grp % groups_per_slab == 0:
            st_ref[:, sl * S5_SLAB_W:(sl + 1) * S5_SLAB_W] = _dot(
                ub[:, sl * S5_SLAB_IN:(sl + 1) * S5_SLAB_IN], bblk_ref[sl])
        offs = []
        for b in range(grp * S5_COLS_PER_STEP, (grp + 1) * S5_COLS_PER_STEP):
            re = (b // blocks_per_slab) * S5_SLAB_W + (b % blocks_per_slab) * LANES
            offs.append((re, re + S5_SLAB_HALF))
        lam = [(jnp.broadcast_to(lam_ref[:, re:re + LANES], vshape),
                jnp.broadcast_to(lam_ref[:, im:im + LANES], vshape)) for re, im in offs]

        def scan_body(j, state):
            r = j * SUBLANES
            new = []
            for (re, im), (lr, li), (sr, si) in zip(offs, lam, state):
                pr, pi = cmul(lr, li, sr, si)
                nr = pr + st_ref[pl.ds(r, SUBLANES), re:re + LANES]
                ni = pi + st_ref[pl.ds(r, SUBLANES), im:im + LANES]
                st_ref[pl.ds(r, SUBLANES), re:re + LANES] = nr
                st_ref[pl.ds(r, SUBLANES), im:im + LANES] = ni
                new.append((nr, ni))
            return tuple(new)

        init = tuple((carry_ref[:, re:re + LANES], carry_ref[:, im:im + LANES])
                     for re, im in offs)
        fin = init
        for j in range(S5_TJ):
            fin = scan_body(j, fin)

        carries = []
        for (re, im), (lr, li), (fr, fi) in zip(offs, lam, fin):
            for _ in range(S5_TJ.bit_length() - 1):
                lr, li = cmul(lr, li, lr, li)
            er, ei = fr, fi
            for k in range(1, S5_SEGS):
                pr, pi = cmul(lr, li, pltpu.roll(er, 1, 0), pltpu.roll(ei, 1, 0))
                er = jnp.where(sub == k, fr + pr, er)
                ei = jnp.where(sub == k, fi + pi, ei)
            sr, si = pltpu.roll(er, 1, 0), pltpu.roll(ei, 1, 0)
            carry_ref[:, re:re + LANES] = jnp.where(sub == 0, sr, 0.0)
            carry_ref[:, im:im + LANES] = jnp.where(sub == 0, si, 0.0)
            carries.append((jnp.where(sub == 0, 0.0, sr), jnp.where(sub == 0, 0.0, si)))

        def fix_body(j, f):
            r = j * SUBLANES
            new = []
            for (re, im), (lr, li), (fr, fi) in zip(offs, lam, f):
                fr, fi = cmul(lr, li, fr, fi)
                st_ref[pl.ds(r, SUBLANES), re:re + LANES] += fr
                st_ref[pl.ds(r, SUBLANES), im:im + LANES] += fi
                new.append((fr, fi))
            return tuple(new)

        f = tuple(carries)
        for j in range(S5_TJ):
            f = fix_body(j, f)

        if grp % groups_per_slab == groups_per_slab - 1:
            ys.append(_dot(st_ref[:, sl * S5_SLAB_W:(sl + 1) * S5_SLAB_W].astype(BF16),
                           cblk_ref[sl]))

    y = jnp.concatenate(ys, axis=1) + d_ref[...] * u
    gl = jax.nn.gelu(y).astype(BF16)
    vg = _dot(gl, wglu_ref[...])
    o_ref[0] = x + vg[:, :D_MODEL] * jax.nn.sigmoid(vg[:, D_MODEL:])


def _s5(xp, g, i, win, bblk, cblk, lam, d, wglu, j):
    B, S, D = xp.shape
    return pl.pallas_call(
        _s5_kernel,
        grid=(B, S // S5_CHUNK),
        in_specs=[
            pl.BlockSpec((1, S5_CHUNK, D), lambda b, c: (b, c, 0)),
            _layer(g, i), _layer(win, j), _resident(bblk.shape),
            _resident(cblk.shape), _resident(lam.shape),
            _layer(d, j), _layer(wglu, j),
        ],
        out_specs=pl.BlockSpec((1, S5_CHUNK, D), lambda b, c: (b, c, 0)),
        out_shape=jax.ShapeDtypeStruct((B, S, D), F32),
        scratch_shapes=[
            pltpu.VMEM((S5_CHUNK, S5_SW), F32),
            pltpu.VMEM((SUBLANES, S5_SW), F32),
        ],
        compiler_params=pltpu.CompilerParams(
            dimension_semantics=("parallel", "arbitrary"),
            vmem_limit_bytes=VMEM_LIMIT_BYTES),
        name="s5",
    )(xp, g, win, bblk, cblk, lam, d, wglu)


def _s5_params(a_re, a_im, log_dt, b_re, b_im, c_re, c_im):
    ar, ai = a_re.astype(F32), a_im.astype(F32)
    dt = jnp.exp(log_dt.astype(F32))[:, None]
    mag = jnp.exp(ar * dt)
    lr, li = mag * jnp.cos(ai * dt), mag * jnp.sin(ai * dt)
    den = ar * ar + ai * ai
    kr = ((lr - 1.0) * ar + li * ai) / den
    ki = (li * ar - (lr - 1.0) * ai) / den
    br, bi = b_re.astype(F32), b_im.astype(F32)
    bbr = kr[..., None] * br - ki[..., None] * bi
    bbi = kr[..., None] * bi + ki[..., None] * br

    gs = S5_GROUPS // S5_SLABS
    eye = jnp.eye(gs, dtype=F32)

    def state_row(zr, zi):
        zr = zr.reshape(S5_SLABS, S5_SLAB_HALF)
        zi = zi.reshape(S5_SLABS, S5_SLAB_HALF)
        return jnp.concatenate([zr, zi], axis=-1).reshape(S5_SW)

    def in_block(b):
        b4 = b.reshape(S5_SLABS, gs, S5_STATE, S5_GROUP)
        return jnp.einsum('sgpc,gh->sgchp', b4, eye).reshape(S5_SLABS, S5_SLAB_IN, S5_SLAB_HALF)

    def out_block(c):
        c4 = c.reshape(S5_SLABS, gs, S5_GROUP, S5_STATE)
        return jnp.einsum('sgcp,gh->shpgc', c4, eye).reshape(S5_SLABS, S5_SLAB_HALF, S5_SLAB_IN)

    bblk = jnp.concatenate([in_block(bbr), in_block(bbi)], axis=2)
    cblk = jnp.concatenate([out_block(c_re.astype(F32)), -out_block(c_im.astype(F32))], axis=1)
    return bblk.astype(BF16), cblk.astype(BF16), state_row(lr, li)[None]


POOL_HALO = 16


def _pool_kernel(x_ref, halo_ref, g_ref, w_ref, b_ref, sc_ref, o_ref, buf_ref):
    t = pl.program_id(1)
    x = x_ref[0]
    g = g_ref[...]
    h = _rms(x, g)
    hh = _rms(halo_ref[0], g)
    buf_ref[0:POOL_HALO] = jnp.where(t > 0, hh, 0.0)
    buf_ref[POOL_HALO:] = h
    tpos = t * TM + lax.broadcasted_iota(jnp.int32, (TM, POOL_WIDTH), 0) + 1
    outs = []
    for gi, w in enumerate(POOL_WINDOWS):
        lo = gi * POOL_WIDTH
        acc = h[:, lo:lo + POOL_WIDTH]
        for k in range(1, w):
            acc = acc + buf_ref[POOL_HALO - k:POOL_HALO - k + TM, lo:lo + POOL_WIDTH]
        mean = acc / jnp.minimum(tpos, w).astype(F32)
        outs.append(_dot((mean - h[:, lo:lo + POOL_WIDTH]).astype(BF16), w_ref[gi]))
    y = (jnp.concatenate(outs, axis=1) + b_ref[...]) * sc_ref[...]
    o_ref[0] = x + y


def _pool(x, g, i, w, b, sc, j):
    B, S, D = x.shape
    per = TM // POOL_HALO
    return pl.pallas_call(
        _pool_kernel,
        grid=(B, S // TM),
        in_specs=[
            pl.BlockSpec((1, TM, D), lambda b, t: (b, t, 0)),
            pl.BlockSpec((1, POOL_HALO, D), lambda b, t: (b, jnp.maximum(t * per - 1, 0), 0)),
            _layer(g, i), _layer(w, j), _layer(b, j), _layer(sc, j),
        ],
        out_specs=pl.BlockSpec((1, TM, D), lambda b, t: (b, t, 0)),
        out_shape=jax.ShapeDtypeStruct((B, S, D), F32),
        scratch_shapes=[pltpu.VMEM((POOL_HALO + TM, D), F32)],
        compiler_params=pltpu.CompilerParams(
            dimension_semantics=("parallel", "arbitrary"),
            vmem_limit_bytes=VMEM_LIMIT_BYTES),
        name="pool",
    )(x, x, g, w, b, sc)


def kernel(x, mix_norm_g, ffn_norm_g, final_norm_g, fox_w_in, fox_b_f, fox_w_out, s5_w_in, s5_a_re, s5_a_im, s5_log_dt, s5_b_re, s5_b_im, s5_c_re, s5_c_im, s5_d, s5_w_glu, pool_w, pool_b, pool_scale, ffn_w_gate_up, ffn_w_down):
    B, S, D = x.shape
    N = B * S
    tri = jnp.triu(jnp.ones((TM, TM), F32)).astype(BF16)
    row = lambda a: a.astype(F32).reshape(a.shape[0], 1, a.shape[1])
    gm_all, gf_all = row(mix_norm_g), row(ffn_norm_g)
    wgu_all, wd_all = ffn_w_gate_up.astype(BF16), ffn_w_down.astype(BF16)
    fox_win = fox_w_in.astype(BF16)
    fox_wft = lax.optimization_barrier(fox_w_in[:, :, 3 * D:]).transpose(0, 2, 1).astype(BF16)
    fox_bf = fox_b_f.astype(F32)[:, :, None]
    fox_wout = fox_w_out.astype(BF16)
    s5_win, s5_wglu, s5_dd = s5_w_in.astype(BF16), s5_w_glu.astype(BF16), row(s5_d)
    pool_ww, pool_bb, pool_sc = pool_w.astype(BF16), row(pool_b), row(pool_scale)

    for i in range(DEPTH):
        kind, j = i % 3, i // 3
        next_is_s5 = i + 1 < DEPTH and (i + 1) % 3 == 1
        final_g = final_norm_g[None].astype(F32) if i == DEPTH - 1 else None
        mix = None
        if kind == 0:
            qkv, c = _fox_proj(x, gm_all, i, fox_win, fox_wft, fox_bf, j, tri)
            c = c.reshape(B, FOX_HEADS // 2, 2, S // TK, TK).transpose(0, 1, 3, 2, 4)
            o = _fox_attn(qkv, c)
            mix = (o.reshape(N, D), fox_wout, j)
        elif kind == 1:
            bblk, cblk, lam = _s5_params(
                s5_a_re[j], s5_a_im[j], s5_log_dt[j], s5_b_re[j], s5_b_im[j], s5_c_re[j], s5_c_im[j])
            x = _s5(x.reshape(B, S, D), gm_all, i, s5_win, bblk, cblk, lam, s5_dd, s5_wglu, j)
            x = x.reshape(N // S5_CHUNK, S5_TJ, S5_SEGS * D)
        else:
            x = _pool(x, gm_all, i, pool_ww, pool_bb, pool_sc, j)
        x = _ffn(x if kind == 1 else x.reshape(N, D), N, gf_all, wgu_all, wd_all, i, mix=mix,
                 final_g=final_g, seg_in=kind == 1, seg_out=next_is_s5)
        if not next_is_s5:
            x = x.reshape(B, S, D)
    return x
```

```python
import functools
import math

import jax
import jax.numpy as jnp
from jax import lax
from jax.experimental import pallas as pl
from jax.experimental.pallas import tpu as pltpu

F32 = jnp.float32
BF16 = jnp.bfloat16

D_MODEL = 1024
DEPTH = 4
EPS = 1e-6
FOX_HEADS = 16
FOX_HEAD_DIM = 64
S5_GROUP = 16
S5_GROUPS = 64
S5_STATE = 64
POOL_WINDOWS = (2, 4, 8, 16)
POOL_WIDTH = 256
D_FF = 2816

LANES = 128
SUBLANES = 8
MXU_DIM = 256
VMEM_LIMIT_BYTES = 56 * 1024 * 1024

LOG2E = 1.4426950408889634
NEG_BIG = -1e30

TM = 512
FF_CHUNK = MXU_DIM
TQ = 1024
TK = 1024
S5_SEGS = SUBLANES
S5_TJ = 32
S5_CHUNK = S5_SEGS * S5_TJ
S5_SLABS = 4
S5_SLAB_IN = D_MODEL // S5_SLABS
S5_SLAB_HALF = (S5_GROUPS // S5_SLABS) * S5_STATE
S5_SLAB_W = 2 * S5_SLAB_HALF
S5_SW = S5_SLABS * S5_SLAB_W
S5_COLS_PER_STEP = 4


def _resident(shape):
    nd = len(shape)
    return pl.BlockSpec(shape, lambda *_: (0,) * nd, pipeline_mode=pl.Buffered(1))


def _layer(arr, i):
    nd = arr.ndim
    return pl.BlockSpec((None,) + arr.shape[1:], lambda *_: (i,) + (0,) * (nd - 1),
                        pipeline_mode=pl.Buffered(1))


def _rms(x, g):
    return x * lax.rsqrt(jnp.mean(x * x, axis=-1, keepdims=True) + EPS) * g


def _dot(a, b):
    return jnp.dot(a, b, preferred_element_type=F32)


def _fox_proj_kernel(x_ref, g_ref, wqkv_ref, wft_ref, bf_ref, tri_ref,
                     qkv_ref, c_ref, carry_ref):
    @pl.when(pl.program_id(1) == 0)
    def _():
        carry_ref[...] = jnp.zeros_like(carry_ref)

    h = _rms(x_ref[0], g_ref[...]).astype(BF16)
    qkv = _dot(h, wqkv_ref[:, :3 * D_MODEL])
    q = qkv[:, :D_MODEL] * (FOX_HEAD_DIM ** -0.5 * LOG2E)
    qkv_ref[0, :, :D_MODEL] = q.astype(BF16)
    qkv_ref[0, :, D_MODEL:] = qkv[:, D_MODEL:].astype(BF16)

    z = lax.dot_general(wft_ref[...], h, (((1,), (1,)), ((), ())),
                        preferred_element_type=F32) + bf_ref[...]
    logf = jnp.minimum(z, 0.0) - jnp.log(1.0 + jnp.exp(-jnp.abs(z)))
    hi = logf.astype(BF16)
    r1 = logf - hi.astype(F32)
    mid = r1.astype(BF16)
    lo = (r1 - mid.astype(F32)).astype(BF16)
    tri = tri_ref[...]
    c = _dot(hi, tri) + _dot(mid, tri) + _dot(lo, tri) + carry_ref[:, 0:1]
    c_ref[0] = c * LOG2E
    carry_ref[...] = jnp.broadcast_to(c[:, TM - 1:TM], carry_ref.shape)


def _fox_proj(x, g, i, win, wft, bf, j, tri):
    B, S, D = x.shape
    return pl.pallas_call(
        _fox_proj_kernel,
        grid=(B, S // TM),
        in_specs=[
            pl.BlockSpec((1, TM, D), lambda b, t: (b, t, 0)),
            _layer(g, i), _layer(win, j), _layer(wft, j), _layer(bf, j),
            _resident((TM, TM)),
        ],
        out_specs=[
            pl.BlockSpec((1, TM, 3 * D), lambda b, t: (b, t, 0)),
            pl.BlockSpec((1, FOX_HEADS, TM), lambda b, t: (b, 0, t)),
        ],
        out_shape=[
            jax.ShapeDtypeStruct((B, S, 3 * D), BF16),
            jax.ShapeDtypeStruct((B, FOX_HEADS, S), F32),
        ],
        scratch_shapes=[pltpu.VMEM((FOX_HEADS, LANES), F32)],
        compiler_params=pltpu.CompilerParams(
            dimension_semantics=("parallel", "arbitrary"),
            vmem_limit_bytes=VMEM_LIMIT_BYTES),
        name="fox_proj",
    )(x, g, win, wft, bf, tri)


def _fox_attn_kernel(q_ref, k_ref, v_ref, c_ref, o_ref,
                     qs_ref, va_ref, vb_ref, s0_ref, s1_ref, m_ref, alpha_ref, acc_ref):
    S = q_ref.shape[1]
    nq = S // TQ
    lane = lax.broadcasted_iota(jnp.int32, (TQ, LANES), 1)
    first = lane < FOX_HEAD_DIM

    def prep_v(i, c):
        r = pl.multiple_of(i * TK, TK)
        v = v_ref[0, pl.ds(r, TK), :].astype(F32)
        ln = lax.broadcasted_iota(jnp.int32, v.shape, 1)
        va = jnp.where(ln < FOX_HEAD_DIM, v, jnp.where(ln == FOX_HEAD_DIM, 1.0, 0.0))
        vb = jnp.where(ln >= FOX_HEAD_DIM, v, jnp.where(ln == 0, 1.0, 0.0))
        va_ref[pl.ds(r, TK), :] = va.astype(BF16)
        vb_ref[pl.ds(r, TK), :] = vb.astype(BF16)
        return c

    lax.fori_loop(0, S // TK, prep_v, 0)

    def prep_q(qb, slot):
        q = q_ref[0, pl.ds(pl.multiple_of(qb * TQ, TQ), TQ), :]
        zero = jnp.zeros_like(q)
        qs_ref[slot, 0:TQ] = jnp.where(first, q, zero)
        qs_ref[slot, TQ:2 * TQ] = jnp.where(first, zero, q)

    def logits(qslot, ci, diag_off):
        k = k_ref[0, pl.ds(pl.multiple_of(ci * TK, TK), TK), :]
        s = lax.dot_general(qs_ref[qslot], k, (((1,), (1,)), ((), ())),
                            preferred_element_type=F32)
        ck = c_ref[0, 0, ci]
        s = s - jnp.concatenate(
            [jnp.broadcast_to(ck[0:1], (TQ, TK)),
             jnp.broadcast_to(ck[1:2], (TQ, TK))], axis=0)
        if diag_off is not None:
            row = lax.broadcasted_iota(jnp.int32, s.shape, 0) & (TQ - 1)
            col = lax.broadcasted_iota(jnp.int32, s.shape, 1)
            s = jnp.where(col <= row + diag_off, s, NEG_BIG)
        return s

    def step_on(cur_ref, nxt_ref, ci, nxt_args, fresh):
        s_next = logits(*nxt_args)
        nxt_ref[...] = s_next
        s = cur_ref[...]
        m_cur = m_ref[...]
        alpha = alpha_ref[...]
        mx = jnp.max(s_next, axis=1, keepdims=True)
        if fresh:
            m_ref[...] = jnp.broadcast_to(mx, m_cur.shape)
            alpha_ref[...] = jnp.zeros_like(alpha)
        else:
            m_next = jnp.maximum(m_cur, mx)
            m_ref[...] = m_next
            alpha_ref[...] = jnp.exp2(m_cur - m_next)
        p = jnp.exp2(s - pltpu.repeat(m_cur, TK // LANES, axis=1)).astype(BF16)
        r = pl.multiple_of(ci * TK, TK)
        pv = jnp.concatenate([_dot(p[:TQ], va_ref[pl.ds(r, TK), :]),
                              _dot(p[TQ:], vb_ref[pl.ds(r, TK), :])], axis=0)
        acc_ref[...] = alpha * acc_ref[...] + pv

    def step(cnt, ci, nxt_args, fresh):
        qslot_n, ci_n, off = nxt_args

        def run(cur_ref, nxt_ref):
            if off is None:
                step_on(cur_ref, nxt_ref, ci, nxt_args, fresh)
            else:
                lax.cond(off < TK - 1,
                         lambda: step_on(cur_ref, nxt_ref, ci, nxt_args, fresh),
                         lambda: step_on(cur_ref, nxt_ref, ci, (qslot_n, ci_n, None), fresh))

        lax.cond((cnt & 1) == 0, lambda: run(s0_ref, s1_ref), lambda: run(s1_ref, s0_ref))
        return cnt + 1

    def q_block(qb, cnt):
        qslot = qb & 1
        prep_q(jnp.minimum(qb + 1, nq - 1), 1 - qslot)
        acc_ref[...] = jnp.zeros_like(acc_ref)
        last = (qb * TQ) // TK

        def inner(ci, cnt):
            return step(cnt, ci, (qslot, ci + 1, None), False)

        cnt = lax.fori_loop(0, last - 1, inner, cnt)

        def before_diag(ci, cnt):
            return step(cnt, ci, (qslot, ci + 1, qb * TQ - last * TK), False)

        cnt = lax.fori_loop(jnp.maximum(last - 1, 0), last, before_diag, cnt)
        cnt = step(cnt, last, (1 - qslot, 0, (qb + 1) * TQ), True)

        acc = acc_ref[...]
        la = jnp.broadcast_to(acc[:TQ, FOX_HEAD_DIM:FOX_HEAD_DIM + 1], (TQ, LANES))
        lb = jnp.broadcast_to(acc[TQ:, 0:1], (TQ, LANES))
        o = jnp.where(first, acc[:TQ] / la, acc[TQ:] / lb)
        o_ref[0, pl.ds(pl.multiple_of(qb * TQ, TQ), TQ), :] = o.astype(BF16)
        return cnt

    prep_q(0, 0)
    s0 = logits(0, 0, 0)
    s0_ref[...] = s0
    m_ref[...] = jnp.broadcast_to(jnp.max(s0, axis=1, keepdims=True), m_ref.shape)
    alpha_ref[...] = jnp.zeros_like(alpha_ref)
    lax.fori_loop(0, nq, q_block, 0)


def _fox_attn(qkv, c):
    B, S, _ = qkv.shape
    pairs = FOX_HEADS // 2
    blk = lambda off: pl.BlockSpec((1, S, LANES), lambda b, p: (b, 0, off + p))
    return pl.pallas_call(
        _fox_attn_kernel,
        grid=(B, pairs),
        in_specs=[
            blk(0), blk(pairs), blk(2 * pairs),
            pl.BlockSpec((1, 1, S // TK, 2, TK), lambda b, p: (b, p, 0, 0, 0)),
        ],
        out_specs=pl.BlockSpec((1, S, LANES), lambda b, p: (b, 0, p)),
        out_shape=jax.ShapeDtypeStruct((B, S, D_MODEL), BF16),
        scratch_shapes=[
            pltpu.VMEM((2, 2 * TQ, LANES), BF16),
            pltpu.VMEM((S, LANES), BF16),
            pltpu.VMEM((S, LANES), BF16),
            pltpu.VMEM((2 * TQ, TK), F32),
            pltpu.VMEM((2 * TQ, TK), F32),
            pltpu.VMEM((2 * TQ, LANES), F32),
            pltpu.VMEM((2 * TQ, LANES), F32),
            pltpu.VMEM((2 * TQ, LANES), F32),
        ],
        compiler_params=pltpu.CompilerParams(
            dimension_semantics=("parallel", "parallel"),
            vmem_limit_bytes=VMEM_LIMIT_BYTES),
        name="fox_attn",
    )(qkv, qkv, qkv, c)


def _ffn_kernel(*refs, has_mix, final_norm, seg_in, seg_out):
    refs = list(refs)
    x_ref = refs.pop(0)
    if has_mix:
        a_ref = refs.pop(0)
        wmix_ref = refs.pop(0)
    g_ref, wgu_ref, wd_ref = refs[:3]
    refs = refs[3:]
    if final_norm:
        gf_ref = refs.pop(0)
    if seg_in or seg_out:
        o_ref, act_ref, slab_ref = refs
    else:
        o_ref, act_ref = refs
    n_slabs = D_MODEL // LANES
    seg_rows = [(ch * S5_CHUNK + sg, ch * S5_CHUNK + sg * S5_TJ)
                for ch in range(TM // S5_CHUNK) for sg in range(S5_SEGS)]

    if seg_in:
        cols = []
        for c in range(n_slabs):
            slab_ref[c] = x_ref[:, c * LANES:(c + 1) * LANES]
            cols.append(jnp.concatenate(
                [slab_ref[c, pl.ds(r_seg, S5_TJ, stride=S5_SEGS), :] for r_seg, _ in seg_rows],
                axis=0))
        x = jnp.concatenate(cols, axis=1)
    else:
        x = x_ref[...]
    if has_mix:
        x = x + _dot(a_ref[...], wmix_ref[...])
    h = _rms(x, g_ref[...]).astype(BF16)
    for c in range(D_FF // FF_CHUNK):
        lo = c * FF_CHUNK
        gate = _dot(h, wgu_ref[:, lo:lo + FF_CHUNK])
        up = _dot(h, wgu_ref[:, D_FF + lo:D_FF + lo + FF_CHUNK])
        act_ref[:, lo:lo + FF_CHUNK] = (gate * jax.nn.sigmoid(gate) * up).astype(BF16)
    y = x + _dot(act_ref[...], wd_ref[...])
    if final_norm:
        y = _rms(y, gf_ref[...])
    if seg_out:
        for c in range(n_slabs):
            for r_seg, r_time in seg_rows:
                slab_ref[c, pl.ds(r_seg, S5_TJ, stride=S5_SEGS), :] = (
                    y[r_time:r_time + S5_TJ, c * LANES:(c + 1) * LANES])
            o_ref[:, c * LANES:(c + 1) * LANES] = slab_ref[c]
    else:
        o_ref[...] = y


def _ffn(x, g, wgu, wd, i, mix=None, final_g=None, seg_in=False, seg_out=False):
    N, D = x.shape
    row = pl.BlockSpec((TM, D), lambda t: (t, 0))
    scratch = [pltpu.VMEM((TM, D_FF), BF16)]
    if seg_in or seg_out:
        scratch.append(pltpu.VMEM((D // LANES, TM, LANES), F32))
    args, specs = [x], [row]
    if mix is not None:
        a2, wmix, j = mix
        args += [a2, wmix]
        specs += [row, _layer(wmix, j)]
    args += [g, wgu, wd]
    specs += [_layer(g, i), _layer(wgu, i), _layer(wd, i)]
    if final_g is not None:
        args.append(final_g)
        specs.append(_resident(final_g.shape))
    return pl.pallas_call(
        functools.partial(_ffn_kernel, has_mix=mix is not None,
                          final_norm=final_g is not None, seg_in=seg_in, seg_out=seg_out),
        grid=(N // TM,),
        in_specs=specs,
        out_specs=row,
        out_shape=jax.ShapeDtypeStruct((N, D), F32),
        scratch_shapes=scratch,
        compiler_params=pltpu.CompilerParams(
            dimension_semantics=("parallel",),
            vmem_limit_bytes=VMEM_LIMIT_BYTES),
        name="ffn",
    )(*args)


def _s5_kernel(x_ref, g_ref, win_ref, bblk_ref, cblk_ref, lam_ref,
               d_ref, wglu_ref, o_ref, st_ref, carry_ref):
    @pl.when(pl.program_id(1) == 0)
    def _():
        carry_ref[...] = jnp.zeros_like(carry_ref)

    x = x_ref[0]
    h = _rms(x, g_ref[...]).astype(BF16)
    u = _dot(h, win_ref[...])
    ub = u.astype(BF16)

    sub = lax.broadcasted_iota(jnp.int32, (SUBLANES, LANES), 0)
    vshape = (SUBLANES, LANES)
    blocks_per_slab = S5_SLAB_HALF // LANES
    n_blocks = S5_SLABS * blocks_per_slab

    def cmul(ar, ai, br, bi):
        return ar * br - ai * bi, ar * bi + ai * br

    groups_per_slab = blocks_per_slab // S5_COLS_PER_STEP
    ys = []
    for grp in range(n_blocks // S5_COLS_PER_STEP):
        sl = grp // groups_per_slab
        if grp % groups_per_slab == 0:
            st_ref[:, sl * S5_SLAB_W:(sl + 1) * S5_SLAB_W] = _dot(
                ub[:, sl * S5_SLAB_IN:(sl + 1) * S5_SLAB_IN], bblk_ref[sl])
        offs = []
        for b in range(grp * S5_COLS_PER_STEP, (grp + 1) * S5_COLS_PER_STEP):
            re = (b // blocks_per_slab) * S5_SLAB_W + (b % blocks_per_slab) * LANES
            offs.append((re, re + S5_SLAB_HALF))
        lam = [(jnp.broadcast_to(lam_ref[:, re:re + LANES], vshape),
                jnp.broadcast_to(lam_ref[:, im:im + LANES], vshape)) for re, im in offs]

        def scan_body(j, state):
            r = j * SUBLANES
            new = []
            for (re, im), (lr, li), (sr, si) in zip(offs, lam, state):
                pr, pi = cmul(lr, li, sr, si)
                nr = pr + st_ref[pl.ds(r, SUBLANES), re:re + LANES]
                ni = pi + st_ref[pl.ds(r, SUBLANES), im:im + LANES]
                st_ref[pl.ds(r, SUBLANES), re:re + LANES] = nr
                st_ref[pl.ds(r, SUBLANES), im:im + LANES] = ni
                new.append((nr, ni))
            return tuple(new)

        init = tuple((carry_ref[:, re:re + LANES], carry_ref[:, im:im + LANES])
                     for re, im in offs)
        fin = init
        for j in range(S5_TJ):
            fin = scan_body(j, fin)

        carries = []
        for (re, im), (lr, li), (fr, fi) in zip(offs, lam, fin):
            for _ in range(S5_TJ.bit_length() - 1):
                lr, li = cmul(lr, li, lr, li)
            er, ei = fr, fi
            for k in range(1, S5_SEGS):
                pr, pi = cmul(lr, li, pltpu.roll(er, 1, 0), pltpu.roll(ei, 1, 0))
                er = jnp.where(sub == k, fr + pr, er)
                ei = jnp.where(sub == k, fi + pi, ei)
            sr, si = pltpu.roll(er, 1, 0), pltpu.roll(ei, 1, 0)
            carry_ref[:, re:re + LANES] = jnp.where(sub == 0, sr, 0.0)
            carry_ref[:, im:im + LANES] = jnp.where(sub == 0, si, 0.0)
            carries.append((jnp.where(sub == 0, 0.0, sr), jnp.where(sub == 0, 0.0, si)))

        def fix_body(j, f):
            r = j * SUBLANES
            new = []
            for (re, im), (lr, li), (fr, fi) in zip(offs, lam, f):
                fr, fi = cmul(lr, li, fr, fi)
                st_ref[pl.ds(r, SUBLANES), re:re + LANES] += fr
                st_ref[pl.ds(r, SUBLANES), im:im + LANES] += fi
                new.append((fr, fi))
            return tuple(new)

        f = tuple(carries)
        for j in range(S5_TJ):
            f = fix_body(j, f)

        if grp % groups_per_slab == groups_per_slab - 1:
            ys.append(_dot(st_ref[:, sl * S5_SLAB_W:(sl + 1) * S5_SLAB_W].astype(BF16),
                           cblk_ref[sl]))

    y = jnp.concatenate(ys, axis=1) + d_ref[...] * u
    gl = jax.nn.gelu(y).astype(BF16)
    vg = _dot(gl, wglu_ref[...])
    o_ref[0] = x + vg[:, :D_MODEL] * jax.nn.sigmoid(vg[:, D_MODEL:])


def _s5(xp, g, i, win, bblk, cblk, lam, d, wglu, j):
    B, S, D = xp.shape
    return pl.pallas_call(
        _s5_kernel,
        grid=(B, S // S5_CHUNK),
        in_specs=[
            pl.BlockSpec((1, S5_CHUNK, D), lambda b, c: (b, c, 0)),
            _layer(g, i), _layer(win, j), _resident(bblk.shape),
            _resident(cblk.shape), _resident(lam.shape),
            _layer(d, j), _layer(wglu, j),
        ],
        out_specs=pl.BlockSpec((1, S5_CHUNK, D), lambda b, c: (b, c, 0)),
        out_shape=jax.ShapeDtypeStruct((B, S, D), F32),
        scratch_shapes=[
            pltpu.VMEM((S5_CHUNK, S5_SW), F32),
            pltpu.VMEM((SUBLANES, S5_SW), F32),
        ],
        compiler_params=pltpu.CompilerParams(
            dimension_semantics=("parallel", "arbitrary"),
            vmem_limit_bytes=VMEM_LIMIT_BYTES),
        name="s5",
    )(xp, g, win, bblk, cblk, lam, d, wglu)


def _s5_params(a_re, a_im, log_dt, b_re, b_im, c_re, c_im):
    ar, ai = a_re.astype(F32), a_im.astype(F32)
    dt = jnp.exp(log_dt.astype(F32))[:, None]
    mag = jnp.exp(ar * dt)
    lr, li = mag * jnp.cos(ai * dt), mag * jnp.sin(ai * dt)
    den = ar * ar + ai * ai
    kr = ((lr - 1.0) * ar + li * ai) / den
    ki = (li * ar - (lr - 1.0) * ai) / den
    br, bi = b_re.astype(F32), b_im.astype(F32)
    bbr = kr[..., None] * br - ki[..., None] * bi
    bbi = kr[..., None] * bi + ki[..., None] * br

    gs = S5_GROUPS // S5_SLABS
    eye = jnp.eye(gs, dtype=F32)

    def state_row(zr, zi):
        zr = zr.reshape(S5_SLABS, S5_SLAB_HALF)
        zi = zi.reshape(S5_SLABS, S5_SLAB_HALF)
        return jnp.concatenate([zr, zi], axis=-1).reshape(S5_SW)

    def in_block(b):
        b4 = b.reshape(S5_SLABS, gs, S5_STATE, S5_GROUP)
        return jnp.einsum('sgpc,gh->sgchp', b4, eye).reshape(S5_SLABS, S5_SLAB_IN, S5_SLAB_HALF)

    def out_block(c):
        c4 = c.reshape(S5_SLABS, gs, S5_GROUP, S5_STATE)
        return jnp.einsum('sgcp,gh->shpgc', c4, eye).reshape(S5_SLABS, S5_SLAB_HALF, S5_SLAB_IN)

    bblk = jnp.concatenate([in_block(bbr), in_block(bbi)], axis=2)
    cblk = jnp.concatenate([out_block(c_re.astype(F32)), -out_block(c_im.astype(F32))], axis=1)
    return bblk.astype(BF16), cblk.astype(BF16), state_row(lr, li)[None]


POOL_HALO = 16


def _pool_kernel(x_ref, halo_ref, g_ref, w_ref, b_ref, sc_ref, o_ref, buf_ref):
    t = pl.program_id(1)
    x = x_ref[0]
    g = g_ref[...]
    h = _rms(x, g)
    hh = _rms(halo_ref[0], g)
    buf_ref[0:POOL_HALO] = jnp.where(t > 0, hh, 0.0)
    buf_ref[POOL_HALO:] = h
    tpos = t * TM + lax.broadcasted_iota(jnp.int32, (TM, POOL_WIDTH), 0) + 1
    outs = []
    for gi, w in enumerate(POOL_WINDOWS):
        lo = gi * POOL_WIDTH
        acc = h[:, lo:lo + POOL_WIDTH]
        for k in range(1, w):
            acc = acc + buf_ref[POOL_HALO - k:POOL_HALO - k + TM, lo:lo + POOL_WIDTH]
        mean = acc / jnp.minimum(tpos, w).astype(F32)
        outs.append(_dot((mean - h[:, lo:lo + POOL_WIDTH]).astype(BF16), w_ref[gi]))
    y = (jnp.concatenate(outs, axis=1) + b_ref[...]) * sc_ref[...]
    o_ref[0] = x + y


def _pool(x, g, i, w, b, sc, j):
    B, S, D = x.shape
    per = TM // POOL_HALO
    return pl.pallas_call(
        _pool_kernel,
        grid=(B, S // TM),
        in_specs=[
            pl.BlockSpec((1, TM, D), lambda b, t: (b, t, 0)),
            pl.BlockSpec((1, POOL_HALO, D), lambda b, t: (b, jnp.maximum(t * per - 1, 0), 0)),
            _layer(g, i), _layer(w, j), _layer(b, j), _layer(sc, j),
        ],
        out_specs=pl.BlockSpec((1, TM, D), lambda b, t: (b, t, 0)),
        out_shape=jax.ShapeDtypeStruct((B, S, D), F32),
        scratch_shapes=[pltpu.VMEM((POOL_HALO + TM, D), F32)],
        compiler_params=pltpu.CompilerParams(
            dimension_semantics=("parallel", "arbitrary"),
            vmem_limit_bytes=VMEM_LIMIT_BYTES),
        name="pool",
    )(x, x, g, w, b, sc)


def kernel(x, mix_norm_g, ffn_norm_g, final_norm_g, fox_w_in, fox_b_f, fox_w_out, s5_w_in, s5_a_re, s5_a_im, s5_log_dt, s5_b_re, s5_b_im, s5_c_re, s5_c_im, s5_d, s5_w_glu, pool_w, pool_b, pool_scale, ffn_w_gate_up, ffn_w_down):
    B, S, D = x.shape
    N = B * S
    tri = jnp.triu(jnp.ones((TM, TM), F32)).astype(BF16)
    row = lambda a: a.astype(F32).reshape(a.shape[0], 1, a.shape[1])
    gm_all, gf_all = row(mix_norm_g), row(ffn_norm_g)
    wgu_all, wd_all = ffn_w_gate_up.astype(BF16), ffn_w_down.astype(BF16)
    fox_win = fox_w_in.astype(BF16)
    fox_wft = lax.optimization_barrier(fox_w_in[:, :, 3 * D:]).transpose(0, 2, 1).astype(BF16)
    fox_bf = fox_b_f.astype(F32)[:, :, None]
    fox_wout = fox_w_out.astype(BF16)
    s5_win, s5_wglu, s5_dd = s5_w_in.astype(BF16), s5_w_glu.astype(BF16), row(s5_d)
    pool_ww, pool_bb, pool_sc = pool_w.astype(BF16), row(pool_b), row(pool_scale)

    for i in range(DEPTH):
        kind, j = i % 3, i // 3
        next_is_s5 = i + 1 < DEPTH and (i + 1) % 3 == 1
        final_g = final_norm_g[None].astype(F32) if i == DEPTH - 1 else None
        mix = None
        if kind == 0:
            qkv, c = _fox_proj(x, gm_all, i, fox_win, fox_wft, fox_bf, j, tri)
            c = c.reshape(B, FOX_HEADS // 2, 2, S // TK, TK).transpose(0, 1, 3, 2, 4)
            o = _fox_attn(qkv, c)
            mix = (o.reshape(N, D), fox_wout, j)
        elif kind == 1:
            bblk, cblk, lam = _s5_params(
                s5_a_re[j], s5_a_im[j], s5_log_dt[j], s5_b_re[j], s5_b_im[j], s5_c_re[j], s5_c_im[j])
            x = _s5(x, gm_all, i, s5_win, bblk, cblk, lam, s5_dd, s5_wglu, j)
        else:
            x = _pool(x, gm_all, i, pool_ww, pool_bb, pool_sc, j)
        x = _ffn(x.reshape(N, D), gf_all, wgu_all, wd_all, i, mix=mix, final_g=final_g,
                 seg_in=kind == 1, seg_out=next_is_s5).reshape(B, S, D)
    return x
```

```python
import functools
import math

import jax
import jax.numpy as jnp
from jax import lax
from jax.experimental import pallas as pl
from jax.experimental.pallas import tpu as pltpu

F32 = jnp.float32
BF16 = jnp.bfloat16

D_MODEL = 1024
DEPTH = 4
EPS = 1e-6
FOX_HEADS = 16
FOX_HEAD_DIM = 64
S5_GROUP = 16
S5_GROUPS = 64
S5_STATE = 64
POOL_WINDOWS = (2, 4, 8, 16)
POOL_WIDTH = 256
D_FF = 2816

LANES = 128
SUBLANES = 8
MXU_DIM = 256
VMEM_LIMIT_BYTES = 56 * 1024 * 1024

LOG2E = 1.4426950408889634
NEG_BIG = -1e30

TM = 512
FF_CHUNK = MXU_DIM
TQ = 1024
TK = 1024
S5_SEGS = SUBLANES
S5_TJ = 32
S5_CHUNK = S5_SEGS * S5_TJ
S5_SLABS = 4
S5_SLAB_IN = D_MODEL // S5_SLABS
S5_SLAB_HALF = (S5_GROUPS // S5_SLABS) * S5_STATE
S5_SLAB_W = 2 * S5_SLAB_HALF
S5_SW = S5_SLABS * S5_SLAB_W
S5_COLS_PER_STEP = 4


def _resident(shape):
    nd = len(shape)
    return pl.BlockSpec(shape, lambda *_: (0,) * nd, pipeline_mode=pl.Buffered(1))


def _layer(arr, i):
    nd = arr.ndim
    return pl.BlockSpec((None,) + arr.shape[1:], lambda *_: (i,) + (0,) * (nd - 1),
                        pipeline_mode=pl.Buffered(1))


def _rms(x, g):
    return x * lax.rsqrt(jnp.mean(x * x, axis=-1, keepdims=True) + EPS) * g


def _dot(a, b):
    return jnp.dot(a, b, preferred_element_type=F32)


def _fox_proj_kernel(x_ref, g_ref, wqkv_ref, wft_ref, bf_ref, tri_ref,
                     qkv_ref, c_ref, carry_ref):
    @pl.when(pl.program_id(1) == 0)
    def _():
        carry_ref[...] = jnp.zeros_like(carry_ref)

    h = _rms(x_ref[0], g_ref[...]).astype(BF16)
    qkv = _dot(h, wqkv_ref[:, :3 * D_MODEL])
    q = qkv[:, :D_MODEL] * (FOX_HEAD_DIM ** -0.5 * LOG2E)
    qkv_ref[0, :, :D_MODEL] = q.astype(BF16)
    qkv_ref[0, :, D_MODEL:] = qkv[:, D_MODEL:].astype(BF16)

    z = lax.dot_general(wft_ref[...], h, (((1,), (1,)), ((), ())),
                        preferred_element_type=F32) + bf_ref[...]
    logf = jnp.minimum(z, 0.0) - jnp.log(1.0 + jnp.exp(-jnp.abs(z)))
    hi = logf.astype(BF16)
    r1 = logf - hi.astype(F32)
    mid = r1.astype(BF16)
    lo = (r1 - mid.astype(F32)).astype(BF16)
    tri = tri_ref[...]
    c = _dot(hi, tri) + _dot(mid, tri) + _dot(lo, tri) + carry_ref[:, 0:1]
    c_ref[0] = c * LOG2E
    carry_ref[...] = jnp.broadcast_to(c[:, TM - 1:TM], carry_ref.shape)


def _fox_proj(x, g, i, win, wft, bf, j, tri):
    B, S, D = x.shape
    return pl.pallas_call(
        _fox_proj_kernel,
        grid=(B, S // TM),
        in_specs=[
            pl.BlockSpec((1, TM, D), lambda b, t: (b, t, 0)),
            _layer(g, i), _layer(win, j), _layer(wft, j), _layer(bf, j),
            _resident((TM, TM)),
        ],
        out_specs=[
            pl.BlockSpec((1, TM, 3 * D), lambda b, t: (b, t, 0)),
            pl.BlockSpec((1, FOX_HEADS, TM), lambda b, t: (b, 0, t)),
        ],
        out_shape=[
            jax.ShapeDtypeStruct((B, S, 3 * D), BF16),
            jax.ShapeDtypeStruct((B, FOX_HEADS, S), F32),
        ],
        scratch_shapes=[pltpu.VMEM((FOX_HEADS, LANES), F32)],
        compiler_params=pltpu.CompilerParams(
            dimension_semantics=("parallel", "arbitrary"),
            vmem_limit_bytes=VMEM_LIMIT_BYTES),
        name="fox_proj",
    )(x, g, win, wft, bf, tri)


def _fox_attn_kernel(q_ref, k_ref, v_ref, c_ref, o_ref,
                     qs_ref, va_ref, vb_ref, s0_ref, s1_ref, m_ref, alpha_ref, acc_ref):
    assert TQ == TK
    S = q_ref.shape[1]
    nq = S // TQ
    lane = lax.broadcasted_iota(jnp.int32, (TQ, LANES), 1)
    first = lane < FOX_HEAD_DIM

    def prep_v(i, c):
        r = pl.multiple_of(i * TK, TK)
        v = v_ref[0, pl.ds(r, TK), :].astype(F32)
        ln = lax.broadcasted_iota(jnp.int32, v.shape, 1)
        va = jnp.where(ln < FOX_HEAD_DIM, v, jnp.where(ln == FOX_HEAD_DIM, 1.0, 0.0))
        vb = jnp.where(ln >= FOX_HEAD_DIM, v, jnp.where(ln == 0, 1.0, 0.0))
        va_ref[pl.ds(r, TK), :] = va.astype(BF16)
        vb_ref[pl.ds(r, TK), :] = vb.astype(BF16)
        return c

    lax.fori_loop(0, S // TK, prep_v, 0)

    def prep_q(qb, slot):
        q = q_ref[0, pl.ds(pl.multiple_of(qb * TQ, TQ), TQ), :]
        zero = jnp.zeros_like(q)
        qs_ref[slot, 0:TQ] = jnp.where(first, q, zero)
        qs_ref[slot, TQ:2 * TQ] = jnp.where(first, zero, q)

    H = TQ // 2

    def qk(q, k):
        return lax.dot_general(q, k, (((1,), (1,)), ((), ())), preferred_element_type=F32)

    def key_bias(ck, lo, width, rows):
        return jnp.concatenate(
            [jnp.broadcast_to(ck[0:1, lo:lo + width], (rows, width)),
             jnp.broadcast_to(ck[1:2, lo:lo + width], (rows, width))], axis=0)

    def causal(s):
        row = lax.broadcasted_iota(jnp.int32, s.shape, 0)
        col = lax.broadcasted_iota(jnp.int32, s.shape, 1)
        return jnp.where(col <= row, s, NEG_BIG)

    def lanes(m, width):
        return jnp.concatenate([m] * (width // LANES), axis=1)

    def next_full(nxt_ref, qslot, ci):
        k = k_ref[0, pl.ds(pl.multiple_of(ci * TK, TK), TK), :]
        s = qk(qs_ref[qslot], k) - key_bias(c_ref[0, 0, ci], 0, TK, TQ)
        nxt_ref[...] = s
        return [(0, 2 * TQ, jnp.max(s, axis=1, keepdims=True))]

    def next_diag(nxt_ref, qslot, ci):
        r = pl.multiple_of(ci * TK, TK)
        q = qs_ref[qslot]
        ck = c_ref[0, 0, ci]
        left = qk(q, k_ref[0, pl.ds(r, H), :]) - key_bias(ck, 0, H, TQ)
        left = jnp.concatenate([causal(left[0:H]), left[H:TQ],
                                causal(left[TQ:TQ + H]), left[TQ + H:]], axis=0)
        q_hi = jnp.concatenate([q[H:TQ], q[TQ + H:]], axis=0)
        right = qk(q_hi, k_ref[0, pl.ds(pl.multiple_of(r + H, H), H), :])
        right = right - key_bias(ck, H, H, H)
        right = jnp.concatenate([causal(right[:H]), causal(right[H:])], axis=0)
        nxt_ref[:, 0:H] = left
        nxt_ref[H:TQ, H:TK] = right[:H]
        nxt_ref[TQ + H:, H:TK] = right[H:]
        rowmax = lambda a: jnp.max(a, axis=1, keepdims=True)
        return [(0, H, rowmax(left[0:H])),
                (H, TQ, jnp.maximum(rowmax(left[H:TQ]), rowmax(right[:H]))),
                (TQ, TQ + H, rowmax(left[TQ:TQ + H])),
                (TQ + H, 2 * TQ, jnp.maximum(rowmax(left[TQ + H:]), rowmax(right[H:])))]

    def pv_full(cur_ref, ci, m_cur):
        p = jnp.exp2(cur_ref[...] - lanes(m_cur, TK)).astype(BF16)
        r = pl.multiple_of(ci * TK, TK)
        return jnp.concatenate([_dot(p[:TQ], va_ref[pl.ds(r, TK), :]),
                                _dot(p[TQ:], vb_ref[pl.ds(r, TK), :])], axis=0)

    def pv_diag(cur_ref, ci, m_cur):
        r = pl.multiple_of(ci * TK, TK)
        r_hi = pl.multiple_of(r + H, H)
        p_left = jnp.exp2(cur_ref[:, 0:H] - lanes(m_cur, H)).astype(BF16)
        out = []
        for head, vh_ref in enumerate((va_ref, vb_ref)):
            lo, hi = head * TQ, head * TQ + H
            pv = _dot(p_left[lo:lo + TQ], vh_ref[pl.ds(r, H), :])
            p_right = jnp.exp2(cur_ref[hi:hi + H, H:TK] - lanes(m_cur[hi:hi + H], H))
            pv_hi = _dot(p_right.astype(BF16), vh_ref[pl.ds(r_hi, H), :])
            out += [pv[:H], pv[H:] + pv_hi]
        return jnp.concatenate(out, axis=0)

    def step_on(cur_ref, nxt_ref, ci, qslot_n, ci_n, cur_diag, nxt_diag):
        row_maxes = (next_diag if nxt_diag else next_full)(nxt_ref, qslot_n, ci_n)
        m_cur = m_ref[...]
        alpha = alpha_ref[...]
        for lo, hi, mx in row_maxes:
            if cur_diag:
                m_ref[lo:hi] = jnp.broadcast_to(mx, (hi - lo, LANES))
                alpha_ref[lo:hi] = jnp.zeros((hi - lo, LANES), F32)
            else:
                m_next = jnp.maximum(m_cur[lo:hi], mx)
                m_ref[lo:hi] = m_next
                alpha_ref[lo:hi] = jnp.exp2(m_cur[lo:hi] - m_next)
        pv = (pv_diag if cur_diag else pv_full)(cur_ref, ci, m_cur)
        acc_ref[...] = alpha * acc_ref[...] + pv

    def step(cnt, *args):
        lax.cond((cnt & 1) == 0,
                 lambda: step_on(s0_ref, s1_ref, *args),
                 lambda: step_on(s1_ref, s0_ref, *args))
        return cnt + 1

    def q_block(qb, cnt):
        qslot = qb & 1
        prep_q(jnp.minimum(qb + 1, nq - 1), 1 - qslot)
        acc_ref[...] = jnp.zeros_like(acc_ref)
        last = qb

        def inner(ci, cnt):
            return step(cnt, ci, qslot, ci + 1, False, False)

        cnt = lax.fori_loop(0, last - 1, inner, cnt)

        def before_diag(ci, cnt):
            return step(cnt, ci, qslot, ci + 1, False, True)

        cnt = lax.fori_loop(jnp.maximum(last - 1, 0), last, before_diag, cnt)
        cnt = step(cnt, last, 1 - qslot, 0, True, False)

        acc = acc_ref[...]
        la = jnp.broadcast_to(acc[:TQ, FOX_HEAD_DIM:FOX_HEAD_DIM + 1], (TQ, LANES))
        lb = jnp.broadcast_to(acc[TQ:, 0:1], (TQ, LANES))
        o = jnp.where(first, acc[:TQ] / la, acc[TQ:] / lb)
        o_ref[0, pl.ds(pl.multiple_of(qb * TQ, TQ), TQ), :] = o.astype(BF16)
        return cnt

    prep_q(0, 0)
    for lo, hi, mx in next_diag(s0_ref, 0, 0):
        m_ref[lo:hi] = jnp.broadcast_to(mx, (hi - lo, LANES))
    alpha_ref[...] = jnp.zeros_like(alpha_ref)
    lax.fori_loop(0, nq, q_block, 0)


def _fox_attn(qkv, c):
    B, S, _ = qkv.shape
    pairs = FOX_HEADS // 2
    blk = lambda off: pl.BlockSpec((1, S, LANES), lambda b, p: (b, 0, off + p))
    return pl.pallas_call(
        _fox_attn_kernel,
        grid=(B, pairs),
        in_specs=[
            blk(0), blk(pairs), blk(2 * pairs),
            pl.BlockSpec((1, 1, S // TK, 2, TK), lambda b, p: (b, p, 0, 0, 0)),
        ],
        out_specs=pl.BlockSpec((1, S, LANES), lambda b, p: (b, 0, p)),
        out_shape=jax.ShapeDtypeStruct((B, S, D_MODEL), BF16),
        scratch_shapes=[
            pltpu.VMEM((2, 2 * TQ, LANES), BF16),
            pltpu.VMEM((S, LANES), BF16),
            pltpu.VMEM((S, LANES), BF16),
            pltpu.VMEM((2 * TQ, TK), F32),
            pltpu.VMEM((2 * TQ, TK), F32),
            pltpu.VMEM((2 * TQ, LANES), F32),
            pltpu.VMEM((2 * TQ, LANES), F32),
            pltpu.VMEM((2 * TQ, LANES), F32),
        ],
        compiler_params=pltpu.CompilerParams(
            dimension_semantics=("parallel", "parallel"),
            vmem_limit_bytes=VMEM_LIMIT_BYTES),
        name="fox_attn",
    )(qkv, qkv, qkv, c)


def _ffn_kernel(*refs, has_mix, final_norm, seg_in, seg_out):
    refs = list(refs)
    x_ref = refs.pop(0)
    if has_mix:
        a_ref = refs.pop(0)
        wmix_ref = refs.pop(0)
    g_ref, wgu_ref, wd_ref = refs[:3]
    refs = refs[3:]
    if final_norm:
        gf_ref = refs.pop(0)
    if seg_in or seg_out:
        o_ref, act_ref, slab_ref = refs
    else:
        o_ref, act_ref = refs
    n_slabs = D_MODEL // LANES
    seg_rows = [(ch * S5_CHUNK + sg, ch * S5_CHUNK + sg * S5_TJ)
                for ch in range(TM // S5_CHUNK) for sg in range(S5_SEGS)]

    if seg_in:
        cols = []
        for c in range(n_slabs):
            slab_ref[c] = x_ref[:, c * LANES:(c + 1) * LANES]
            cols.append(jnp.concatenate(
                [slab_ref[c, pl.ds(r_seg, S5_TJ, stride=S5_SEGS), :] for r_seg, _ in seg_rows],
                axis=0))
        x = jnp.concatenate(cols, axis=1)
    else:
        x = x_ref[...]
    if has_mix:
        x = x + _dot(a_ref[...], wmix_ref[...])
    h = _rms(x, g_ref[...]).astype(BF16)
    for c in range(D_FF // FF_CHUNK):
        lo = c * FF_CHUNK
        gate = _dot(h, wgu_ref[:, lo:lo + FF_CHUNK])
        up = _dot(h, wgu_ref[:, D_FF + lo:D_FF + lo + FF_CHUNK])
        act_ref[:, lo:lo + FF_CHUNK] = (gate * jax.nn.sigmoid(gate) * up).astype(BF16)
    y = x + _dot(act_ref[...], wd_ref[...])
    if final_norm:
        y = _rms(y, gf_ref[...])
    if seg_out:
        for c in range(n_slabs):
            for r_seg, r_time in seg_rows:
                slab_ref[c, pl.ds(r_seg, S5_TJ, stride=S5_SEGS), :] = (
                    y[r_time:r_time + S5_TJ, c * LANES:(c + 1) * LANES])
            o_ref[:, c * LANES:(c + 1) * LANES] = slab_ref[c]
    else:
        o_ref[...] = y


def _ffn(x, g, wgu, wd, i, mix=None, final_g=None, seg_in=False, seg_out=False):
    N, D = x.shape
    row = pl.BlockSpec((TM, D), lambda t: (t, 0))
    scratch = [pltpu.VMEM((TM, D_FF), BF16)]
    if seg_in or seg_out:
        scratch.append(pltpu.VMEM((D // LANES, TM, LANES), F32))
    args, specs = [x], [row]
    if mix is not None:
        a2, wmix, j = mix
        args += [a2, wmix]
        specs += [row, _layer(wmix, j)]
    args += [g, wgu, wd]
    specs += [_layer(g, i), _layer(wgu, i), _layer(wd, i)]
    if final_g is not None:
        args.append(final_g)
        specs.append(_resident(final_g.shape))
    return pl.pallas_call(
        functools.partial(_ffn_kernel, has_mix=mix is not None,
                          final_norm=final_g is not None, seg_in=seg_in, seg_out=seg_out),
        grid=(N // TM,),
        in_specs=specs,
        out_specs=row,
        out_shape=jax.ShapeDtypeStruct((N, D), F32),
        scratch_shapes=scratch,
        compiler_params=pltpu.CompilerParams(
            dimension_semantics=("parallel",),
            vmem_limit_bytes=VMEM_LIMIT_BYTES),
        name="ffn",
    )(*args)


def _s5_kernel(x_ref, g_ref, win_ref, bblk_ref, cblk_ref, lam_ref,
               d_ref, wglu_ref, o_ref, st_ref, carry_ref):
    @pl.when(pl.program_id(1) == 0)
    def _():
        carry_ref[...] = jnp.zeros_like(carry_ref)

    x = x_ref[0]
    h = _rms(x, g_ref[...]).astype(BF16)
    u = _dot(h, win_ref[...])
    ub = u.astype(BF16)

    sub = lax.broadcasted_iota(jnp.int32, (SUBLANES, LANES), 0)
    vshape = (SUBLANES, LANES)
    blocks_per_slab = S5_SLAB_HALF // LANES
    n_blocks = S5_SLABS * blocks_per_slab

    def cmul(ar, ai, br, bi):
        return ar * br - ai * bi, ar * bi + ai * br

    groups_per_slab = blocks_per_slab // S5_COLS_PER_STEP
    ys = []
    for grp in range(n_blocks // S5_COLS_PER_STEP):
        sl = grp // groups_per_slab
        if grp % groups_per_slab == 0:
            st_ref[:, sl * S5_SLAB_W:(sl + 1) * S5_SLAB_W] = _dot(
                ub[:, sl * S5_SLAB_IN:(sl + 1) * S5_SLAB_IN], bblk_ref[sl])
        offs = []
        for b in range(grp * S5_COLS_PER_STEP, (grp + 1) * S5_COLS_PER_STEP):
            re = (b // blocks_per_slab) * S5_SLAB_W + (b % blocks_per_slab) * LANES
            offs.append((re, re + S5_SLAB_HALF))
        lam = [(jnp.broadcast_to(lam_ref[:, re:re + LANES], vshape),
                jnp.broadcast_to(lam_ref[:, im:im + LANES], vshape)) for re, im in offs]

        def scan_body(j, state):
            r = j * SUBLANES
            new = []
            for (re, im), (lr, li), (sr, si) in zip(offs, lam, state):
                pr, pi = cmul(lr, li, sr, si)
                nr = pr + st_ref[pl.ds(r, SUBLANES), re:re + LANES]
                ni = pi + st_ref[pl.ds(r, SUBLANES), im:im + LANES]
                st_ref[pl.ds(r, SUBLANES), re:re + LANES] = nr
                st_ref[pl.ds(r, SUBLANES), im:im + LANES] = ni
                new.append((nr, ni))
            return tuple(new)

        init = tuple((carry_ref[:, re:re + LANES], carry_ref[:, im:im + LANES])
                     for re, im in offs)
        fin = init
        for j in range(S5_TJ):
            fin = scan_body(j, fin)

        carries = []
        for (re, im), (lr, li), (fr, fi) in zip(offs, lam, fin):
            for _ in range(S5_TJ.bit_length() - 1):
                lr, li = cmul(lr, li, lr, li)
            er, ei = fr, fi
            for k in range(1, S5_SEGS):
                pr, pi = cmul(lr, li, pltpu.roll(er, 1, 0), pltpu.roll(ei, 1, 0))
                er = jnp.where(sub == k, fr + pr, er)
                ei = jnp.where(sub == k, fi + pi, ei)
            sr, si = pltpu.roll(er, 1, 0), pltpu.roll(ei, 1, 0)
            carry_ref[:, re:re + LANES] = jnp.where(sub == 0, sr, 0.0)
            carry_ref[:, im:im + LANES] = jnp.where(sub == 0, si, 0.0)
            carries.append((jnp.where(sub == 0, 0.0, sr), jnp.where(sub == 0, 0.0, si)))

        def fix_body(j, f):
            r = j * SUBLANES
            new = []
            for (re, im), (lr, li), (fr, fi) in zip(offs, lam, f):
                fr, fi = cmul(lr, li, fr, fi)
                st_ref[pl.ds(r, SUBLANES), re:re + LANES] += fr
                st_ref[pl.ds(r, SUBLANES), im:im + LANES] += fi
                new.append((fr, fi))
            return tuple(new)

        f = tuple(carries)
        for j in range(S5_TJ):
            f = fix_body(j, f)

        if grp % groups_per_slab == groups_per_slab - 1:
            ys.append(_dot(st_ref[:, sl * S5_SLAB_W:(sl + 1) * S5_SLAB_W].astype(BF16),
                           cblk_ref[sl]))

    y = jnp.concatenate(ys, axis=1) + d_ref[...] * u
    gl = jax.nn.gelu(y).astype(BF16)
    vg = _dot(gl, wglu_ref[...])
    o_ref[0] = x + vg[:, :D_MODEL] * jax.nn.sigmoid(vg[:, D_MODEL:])


def _s5(xp, g, i, win, bblk, cblk, lam, d, wglu, j):
    B, S, D = xp.shape
    return pl.pallas_call(
        _s5_kernel,
        grid=(B, S // S5_CHUNK),
        in_specs=[
            pl.BlockSpec((1, S5_CHUNK, D), lambda b, c: (b, c, 0)),
            _layer(g, i), _layer(win, j), _resident(bblk.shape),
            _resident(cblk.shape), _resident(lam.shape),
            _layer(d, j), _layer(wglu, j),
        ],
        out_specs=pl.BlockSpec((1, S5_CHUNK, D), lambda b, c: (b, c, 0)),
        out_shape=jax.ShapeDtypeStruct((B, S, D), F32),
        scratch_shapes=[
            pltpu.VMEM((S5_CHUNK, S5_SW), F32),
            pltpu.VMEM((SUBLANES, S5_SW), F32),
        ],
        compiler_params=pltpu.CompilerParams(
            dimension_semantics=("parallel", "arbitrary"),
            vmem_limit_bytes=VMEM_LIMIT_BYTES),
        name="s5",
    )(xp, g, win, bblk, cblk, lam, d, wglu)


def _s5_params(a_re, a_im, log_dt, b_re, b_im, c_re, c_im):
    ar, ai = a_re.astype(F32), a_im.astype(F32)
    dt = jnp.exp(log_dt.astype(F32))[:, None]
    mag = jnp.exp(ar * dt)
    lr, li = mag * jnp.cos(ai * dt), mag * jnp.sin(ai * dt)
    den = ar * ar + ai * ai
    kr = ((lr - 1.0) * ar + li * ai) / den
    ki = (li * ar - (lr - 1.0) * ai) / den
    br, bi = b_re.astype(F32), b_im.astype(F32)
    bbr = kr[..., None] * br - ki[..., None] * bi
    bbi = kr[..., None] * bi + ki[..., None] * br

    gs = S5_GROUPS // S5_SLABS
    eye = jnp.eye(gs, dtype=F32)

    def state_row(zr, zi):
        zr = zr.reshape(S5_SLABS, S5_SLAB_HALF)
        zi = zi.reshape(S5_SLABS, S5_SLAB_HALF)
        return jnp.concatenate([zr, zi], axis=-1).reshape(S5_SW)

    def in_block(b):
        b4 = b.reshape(S5_SLABS, gs, S5_STATE, S5_GROUP)
        return jnp.einsum('sgpc,gh->sgchp', b4, eye).reshape(S5_SLABS, S5_SLAB_IN, S5_SLAB_HALF)

    def out_block(c):
        c4 = c.reshape(S5_SLABS, gs, S5_GROUP, S5_STATE)
        return jnp.einsum('sgcp,gh->shpgc', c4, eye).reshape(S5_SLABS, S5_SLAB_HALF, S5_SLAB_IN)

    bblk = jnp.concatenate([in_block(bbr), in_block(bbi)], axis=2)
    cblk = jnp.concatenate([out_block(c_re.astype(F32)), -out_block(c_im.astype(F32))], axis=1)
    return bblk.astype(BF16), cblk.astype(BF16), state_row(lr, li)[None]


POOL_HALO = 16


def _pool_kernel(x_ref, halo_ref, g_ref, w_ref, b_ref, sc_ref, o_ref, buf_ref):
    t = pl.program_id(1)
    x = x_ref[0]
    g = g_ref[...]
    h = _rms(x, g)
    hh = _rms(halo_ref[0], g)
    buf_ref[0:POOL_HALO] = jnp.where(t > 0, hh, 0.0)
    buf_ref[POOL_HALO:] = h
    tpos = t * TM + lax.broadcasted_iota(jnp.int32, (TM, POOL_WIDTH), 0) + 1
    outs = []
    for gi, w in enumerate(POOL_WINDOWS):
        lo = gi * POOL_WIDTH
        acc = h[:, lo:lo + POOL_WIDTH]
        for k in range(1, w):
            acc = acc + buf_ref[POOL_HALO - k:POOL_HALO - k + TM, lo:lo + POOL_WIDTH]
        mean = acc / jnp.minimum(tpos, w).astype(F32)
        outs.append(_dot((mean - h[:, lo:lo + POOL_WIDTH]).astype(BF16), w_ref[gi]))
    y = (jnp.concatenate(outs, axis=1) + b_ref[...]) * sc_ref[...]
    o_ref[0] = x + y


def _pool(x, g, i, w, b, sc, j):
    B, S, D = x.shape
    per = TM // POOL_HALO
    return pl.pallas_call(
        _pool_kernel,
        grid=(B, S // TM),
        in_specs=[
            pl.BlockSpec((1, TM, D), lambda b, t: (b, t, 0)),
            pl.BlockSpec((1, POOL_HALO, D), lambda b, t: (b, jnp.maximum(t * per - 1, 0), 0)),
            _layer(g, i), _layer(w, j), _layer(b, j), _layer(sc, j),
        ],
        out_specs=pl.BlockSpec((1, TM, D), lambda b, t: (b, t, 0)),
        out_shape=jax.ShapeDtypeStruct((B, S, D), F32),
        scratch_shapes=[pltpu.VMEM((POOL_HALO + TM, D), F32)],
        compiler_params=pltpu.CompilerParams(
            dimension_semantics=("parallel", "arbitrary"),
            vmem_limit_bytes=VMEM_LIMIT_BYTES),
        name="pool",
    )(x, x, g, w, b, sc)


def kernel(x, mix_norm_g, ffn_norm_g, final_norm_g, fox_w_in, fox_b_f, fox_w_out, s5_w_in, s5_a_re, s5_a_im, s5_log_dt, s5_b_re, s5_b_im, s5_c_re, s5_c_im, s5_d, s5_w_glu, pool_w, pool_b, pool_scale, ffn_w_gate_up, ffn_w_down):
    B, S, D = x.shape
    N = B * S
    tri = jnp.triu(jnp.ones((TM, TM), F32)).astype(BF16)
    row = lambda a: a.astype(F32).reshape(a.shape[0], 1, a.shape[1])
    gm_all, gf_all = row(mix_norm_g), row(ffn_norm_g)
    wgu_all, wd_all = ffn_w_gate_up.astype(BF16), ffn_w_down.astype(BF16)
    fox_win = fox_w_in.astype(BF16)
    fox_wft = lax.optimization_barrier(fox_w_in[:, :, 3 * D:]).transpose(0, 2, 1).astype(BF16)
    fox_bf = fox_b_f.astype(F32)[:, :, None]
    fox_wout = fox_w_out.astype(BF16)
    s5_win, s5_wglu, s5_dd = s5_w_in.astype(BF16), s5_w_glu.astype(BF16), row(s5_d)
    pool_ww, pool_bb, pool_sc = pool_w.astype(BF16), row(pool_b), row(pool_scale)

    for i in range(DEPTH):
        kind, j = i % 3, i // 3
        next_is_s5 = i + 1 < DEPTH and (i + 1) % 3 == 1
        final_g = final_norm_g[None].astype(F32) if i == DEPTH - 1 else None
        mix = None
        if kind == 0:
            qkv, c = _fox_proj(x, gm_all, i, fox_win, fox_wft, fox_bf, j, tri)
            c = c.reshape(B, FOX_HEADS // 2, 2, S // TK, TK).transpose(0, 1, 3, 2, 4)
            o = _fox_attn(qkv, c)
            mix = (o.reshape(N, D), fox_wout, j)
        elif kind == 1:
            bblk, cblk, lam = _s5_params(
                s5_a_re[j], s5_a_im[j], s5_log_dt[j], s5_b_re[j], s5_b_im[j], s5_c_re[j], s5_c_im[j])
            x = _s5(x, gm_all, i, s5_win, bblk, cblk, lam, s5_dd, s5_wglu, j)
        else:
            x = _pool(x, gm_all, i, pool_ww, pool_bb, pool_sc, j)
        x = _ffn(x.reshape(N, D), gf_all, wgu_all, wd_all, i, mix=mix, final_g=final_g,
                 seg_in=kind == 1, seg_out=next_is_s5).reshape(B, S, D)
    return x
```

```python
import functools
import math

import jax
import jax.numpy as jnp
from jax import lax
from jax.experimental import pallas as pl
from jax.experimental.pallas import tpu as pltpu

F32 = jnp.float32
BF16 = jnp.bfloat16

D_MODEL = 1024
DEPTH = 4
EPS = 1e-6
FOX_HEADS = 16
FOX_HEAD_DIM = 64
S5_GROUP = 16
S5_GROUPS = 64
S5_STATE = 64
POOL_WINDOWS = (2, 4, 8, 16)
POOL_WIDTH = 256
D_FF = 2816

LANES = 128
SUBLANES = 8
MXU_DIM = 256
VMEM_LIMIT_BYTES = 56 * 1024 * 1024

LOG2E = 1.4426950408889634
NEG_BIG = -1e30

TM = 512
FF_CHUNK = MXU_DIM
TQ = 1024
TK = 1024
S5_SEGS = SUBLANES
S5_TJ = 32
S5_CHUNK = S5_SEGS * S5_TJ
S5_SLABS = 4
S5_SLAB_IN = D_MODEL // S5_SLABS
S5_SLAB_HALF = (S5_GROUPS // S5_SLABS) * S5_STATE
S5_SLAB_W = 2 * S5_SLAB_HALF
S5_SW = S5_SLABS * S5_SLAB_W
S5_COLS_PER_STEP = 4


def _resident(shape):
    nd = len(shape)
    return pl.BlockSpec(shape, lambda *_: (0,) * nd, pipeline_mode=pl.Buffered(1))


def _layer(arr, i):
    nd = arr.ndim
    return pl.BlockSpec((None,) + arr.shape[1:], lambda *_: (i,) + (0,) * (nd - 1),
                        pipeline_mode=pl.Buffered(1))


def _rms(x, g):
    return x * lax.rsqrt(jnp.mean(x * x, axis=-1, keepdims=True) + EPS) * g


def _dot(a, b):
    return jnp.dot(a, b, preferred_element_type=F32)


def _fox_proj_kernel(x_ref, g_ref, wqk_ref, wvt_ref, wft_ref, bf_ref, tri_ref,
                     qk_ref, vt_ref, c_ref, carry_ref):
    @pl.when(pl.program_id(1) == 0)
    def _():
        carry_ref[...] = jnp.zeros_like(carry_ref)

    h = _rms(x_ref[0], g_ref[...]).astype(BF16)
    qk = _dot(h, wqk_ref[:, :2 * D_MODEL])
    q = qk[:, :D_MODEL] * (FOX_HEAD_DIM ** -0.5 * LOG2E)
    qk_ref[0, :, :D_MODEL] = q.astype(BF16)
    qk_ref[0, :, D_MODEL:] = qk[:, D_MODEL:].astype(BF16)
    vt_ref[0] = lax.dot_general(wvt_ref[...], h, (((1,), (1,)), ((), ())),
                                preferred_element_type=F32).astype(BF16)

    z = lax.dot_general(wft_ref[...], h, (((1,), (1,)), ((), ())),
                        preferred_element_type=F32) + bf_ref[...]
    logf = jnp.minimum(z, 0.0) - jnp.log(1.0 + jnp.exp(-jnp.abs(z)))
    hi = logf.astype(BF16)
    r1 = logf - hi.astype(F32)
    mid = r1.astype(BF16)
    lo = (r1 - mid.astype(F32)).astype(BF16)
    parts = _dot(jnp.concatenate([hi, mid, lo], axis=0), tri_ref[...])
    c = (parts[:FOX_HEADS] + parts[FOX_HEADS:2 * FOX_HEADS] + parts[2 * FOX_HEADS:]
         + carry_ref[:, 0:1])
    c_pad = jnp.concatenate([c * LOG2E, jnp.zeros((LANES - FOX_HEADS, TM), F32)], axis=0)
    c_ref[0] = c_pad.T
    carry_ref[...] = jnp.broadcast_to(c[:, TM - 1:TM], carry_ref.shape)


def _fox_proj(x, g, i, win, wvt, wft, bf, j, tri):
    B, S, D = x.shape
    return pl.pallas_call(
        _fox_proj_kernel,
        grid=(B, S // TM),
        in_specs=[
            pl.BlockSpec((1, TM, D), lambda b, t: (b, t, 0)),
            _layer(g, i), _layer(win, j), _layer(wvt, j), _layer(wft, j), _layer(bf, j),
            _resident((TM, TM)),
        ],
        out_specs=[
            pl.BlockSpec((1, TM, 2 * D), lambda b, t: (b, t, 0)),
            pl.BlockSpec((1, D, TM), lambda b, t: (b, 0, t)),
            pl.BlockSpec((1, TM, LANES), lambda b, t: (b, t, 0)),
        ],
        out_shape=[
            jax.ShapeDtypeStruct((B, S, 2 * D), BF16),
            jax.ShapeDtypeStruct((B, D, S), BF16),
            jax.ShapeDtypeStruct((B, S, LANES), F32),
        ],
        scratch_shapes=[pltpu.VMEM((FOX_HEADS, LANES), F32)],
        compiler_params=pltpu.CompilerParams(
            dimension_semantics=("parallel", "arbitrary"),
            vmem_limit_bytes=VMEM_LIMIT_BYTES),
        name="fox_proj",
    )(x, g, win, wvt, wft, bf, tri)


def _fox_attn_kernel(q_ref, k_ref, v_ref, c_ref, o_ref,
                     qs_ref, va_ref, vb_ref, s0_ref, s1_ref, m_ref, alpha_ref, acc_ref):
    assert TQ == TK
    S = q_ref.shape[1]
    nq = S // TQ
    lane = lax.broadcasted_iota(jnp.int32, (TQ, LANES), 1)
    first = lane < FOX_HEAD_DIM

    def prep_v(i, c):
        r = pl.multiple_of(i * TK, TK)
        v = v_ref[0, pl.ds(r, TK), :].astype(F32)
        ln = lax.broadcasted_iota(jnp.int32, v.shape, 1)
        va = jnp.where(ln < FOX_HEAD_DIM, v, jnp.where(ln == FOX_HEAD_DIM, 1.0, 0.0))
        vb = jnp.where(ln >= FOX_HEAD_DIM, v, jnp.where(ln == 0, 1.0, 0.0))
        va_ref[pl.ds(r, TK), :] = va.astype(BF16)
        vb_ref[pl.ds(r, TK), :] = vb.astype(BF16)
        return c

    lax.fori_loop(0, S // TK, prep_v, 0)

    def prep_q(qb, slot):
        q = q_ref[0, pl.ds(pl.multiple_of(qb * TQ, TQ), TQ), :]
        zero = jnp.zeros_like(q)
        qs_ref[slot, 0:TQ] = jnp.where(first, q, zero)
        qs_ref[slot, TQ:2 * TQ] = jnp.where(first, zero, q)

    H = TQ // 2

    def qk(q, k):
        return lax.dot_general(q, k, (((1,), (1,)), ((), ())), preferred_element_type=F32)

    def key_bias(ck, lo, width, rows):
        return jnp.concatenate(
            [jnp.broadcast_to(ck[0:1, lo:lo + width], (rows, width)),
             jnp.broadcast_to(ck[1:2, lo:lo + width], (rows, width))], axis=0)

    def causal(s):
        row = lax.broadcasted_iota(jnp.int32, s.shape, 0)
        col = lax.broadcasted_iota(jnp.int32, s.shape, 1)
        return jnp.where(col <= row, s, NEG_BIG)

    def lanes(m, width):
        return jnp.concatenate([m] * (width // LANES), axis=1)

    def next_full(nxt_ref, qslot, ci):
        k = k_ref[0, pl.ds(pl.multiple_of(ci * TK, TK), TK), :]
        s = qk(qs_ref[qslot], k) - key_bias(c_ref[0, 0, ci], 0, TK, TQ)
        nxt_ref[...] = s
        return [(0, 2 * TQ, jnp.max(s, axis=1, keepdims=True))]

    def next_diag(nxt_ref, qslot, ci):
        r = pl.multiple_of(ci * TK, TK)
        q = qs_ref[qslot]
        ck = c_ref[0, 0, ci]
        left = qk(q, k_ref[0, pl.ds(r, H), :]) - key_bias(ck, 0, H, TQ)
        left = jnp.concatenate([causal(left[0:H]), left[H:TQ],
                                causal(left[TQ:TQ + H]), left[TQ + H:]], axis=0)
        q_hi = jnp.concatenate([q[H:TQ], q[TQ + H:]], axis=0)
        right = qk(q_hi, k_ref[0, pl.ds(pl.multiple_of(r + H, H), H), :])
        right = right - key_bias(ck, H, H, H)
        right = jnp.concatenate([causal(right[:H]), causal(right[H:])], axis=0)
        nxt_ref[:, 0:H] = left
        nxt_ref[H:TQ, H:TK] = right[:H]
        nxt_ref[TQ + H:, H:TK] = right[H:]
        rowmax = lambda a: jnp.max(a, axis=1, keepdims=True)
        return [(0, H, rowmax(left[0:H])),
                (H, TQ, jnp.maximum(rowmax(left[H:TQ]), rowmax(right[:H]))),
                (TQ, TQ + H, rowmax(left[TQ:TQ + H])),
                (TQ + H, 2 * TQ, jnp.maximum(rowmax(left[TQ + H:]), rowmax(right[H:])))]

    def pv_full(cur_ref, ci, m_cur):
        p = jnp.exp2(cur_ref[...] - lanes(m_cur, TK)).astype(BF16)
        r = pl.multiple_of(ci * TK, TK)
        return jnp.concatenate([_dot(p[:TQ], va_ref[pl.ds(r, TK), :]),
                                _dot(p[TQ:], vb_ref[pl.ds(r, TK), :])], axis=0)

    def pv_diag(cur_ref, ci, m_cur):
        r = pl.multiple_of(ci * TK, TK)
        r_hi = pl.multiple_of(r + H, H)
        p_left = jnp.exp2(cur_ref[:, 0:H] - lanes(m_cur, H)).astype(BF16)
        out = []
        for head, vh_ref in enumerate((va_ref, vb_ref)):
            lo, hi = head * TQ, head * TQ + H
            pv = _dot(p_left[lo:lo + TQ], vh_ref[pl.ds(r, H), :])
            p_right = jnp.exp2(cur_ref[hi:hi + H, H:TK] - lanes(m_cur[hi:hi + H], H))
            pv_hi = _dot(p_right.astype(BF16), vh_ref[pl.ds(r_hi, H), :])
            out += [pv[:H], pv[H:] + pv_hi]
        return jnp.concatenate(out, axis=0)

    def step_on(cur_ref, nxt_ref, ci, qslot_n, ci_n, cur_diag, nxt_diag):
        row_maxes = (next_diag if nxt_diag else next_full)(nxt_ref, qslot_n, ci_n)
        m_cur = m_ref[...]
        alpha = alpha_ref[...]
        for lo, hi, mx in row_maxes:
            if cur_diag:
                m_ref[lo:hi] = jnp.broadcast_to(mx, (hi - lo, LANES))
                alpha_ref[lo:hi] = jnp.zeros((hi - lo, LANES), F32)
            else:
                m_next = jnp.maximum(m_cur[lo:hi], mx)
                m_ref[lo:hi] = m_next
                alpha_ref[lo:hi] = jnp.exp2(m_cur[lo:hi] - m_next)
        pv = (pv_diag if cur_diag else pv_full)(cur_ref, ci, m_cur)
        acc_ref[...] = alpha * acc_ref[...] + pv

    def step(cnt, *args):
        lax.cond((cnt & 1) == 0,
                 lambda: step_on(s0_ref, s1_ref, *args),
                 lambda: step_on(s1_ref, s0_ref, *args))
        return cnt + 1

    def q_block(qb, cnt):
        qslot = qb & 1
        prep_q(jnp.minimum(qb + 1, nq - 1), 1 - qslot)
        acc_ref[...] = jnp.zeros_like(acc_ref)
        last = qb

        def inner(ci, cnt):
            return step(cnt, ci, qslot, ci + 1, False, False)

        cnt = lax.fori_loop(0, last - 1, inner, cnt)

        def before_diag(ci, cnt):
            return step(cnt, ci, qslot, ci + 1, False, True)

        cnt = lax.fori_loop(jnp.maximum(last - 1, 0), last, before_diag, cnt)
        cnt = step(cnt, last, 1 - qslot, 0, True, False)

        acc = acc_ref[...]
        la = jnp.broadcast_to(acc[:TQ, FOX_HEAD_DIM:FOX_HEAD_DIM + 1], (TQ, LANES))
        lb = jnp.broadcast_to(acc[TQ:, 0:1], (TQ, LANES))
        o = jnp.where(first, acc[:TQ] / la, acc[TQ:] / lb)
        o_ref[0, pl.ds(pl.multiple_of(qb * TQ, TQ), TQ), :] = o.astype(BF16)
        return cnt

    prep_q(0, 0)
    for lo, hi, mx in next_diag(s0_ref, 0, 0):
        m_ref[lo:hi] = jnp.broadcast_to(mx, (hi - lo, LANES))
    alpha_ref[...] = jnp.zeros_like(alpha_ref)
    lax.fori_loop(0, nq, q_block, 0)


def _fox_attn(qkv, c):
    B, S, _ = qkv.shape
    pairs = FOX_HEADS // 2
    blk = lambda off: pl.BlockSpec((1, S, LANES), lambda b, p: (b, 0, off + p))
    return pl.pallas_call(
        _fox_attn_kernel,
        grid=(B, pairs),
        in_specs=[
            blk(0), blk(pairs), blk(2 * pairs),
            pl.BlockSpec((1, 1, S // TK, 2, TK), lambda b, p: (b, p, 0, 0, 0)),
        ],
        out_specs=pl.BlockSpec((1, S, LANES), lambda b, p: (b, 0, p)),
        out_shape=jax.ShapeDtypeStruct((B, S, D_MODEL), BF16),
        scratch_shapes=[
            pltpu.VMEM((2, 2 * TQ, LANES), BF16),
            pltpu.VMEM((S, LANES), BF16),
            pltpu.VMEM((S, LANES), BF16),
            pltpu.VMEM((2 * TQ, TK), F32),
            pltpu.VMEM((2 * TQ, TK), F32),
            pltpu.VMEM((2 * TQ, LANES), F32),
            pltpu.VMEM((2 * TQ, LANES), F32),
            pltpu.VMEM((2 * TQ, LANES), F32),
        ],
        compiler_params=pltpu.CompilerParams(
            dimension_semantics=("parallel", "parallel"),
            vmem_limit_bytes=VMEM_LIMIT_BYTES),
        name="fox_attn",
    )(qkv, qkv, qkv, c)


ATT_PREP = 512


def _fox_attn_t_kernel(q_ref, k_ref, vt_ref, c_ref, o_ref,
                       qs_ref, kx_ref, vat_ref, vbt_ref, s0_ref, s1_ref,
                       m_ref, alpha_ref, acc_ref):
    assert TQ == TK
    S = q_ref.shape[1]
    nq = S // TQ
    H = TQ // 2
    pair = pl.program_id(1)

    rr = lax.broadcasted_iota(jnp.int32, (LANES, LANES), 0)
    ll = lax.broadcasted_iota(jnp.int32, (LANES, LANES), 1)

    def selector(j):
        one = lambda cond: jnp.where(cond, 1.0, 0.0)
        return (one(rr == 2 * pair) * one(ll == j)
                + one(rr == 2 * pair + 1) * one(ll == 3 + j)).astype(BF16)

    sel = [selector(j) for j in range(3)]
    vrow = lax.broadcasted_iota(jnp.int32, (LANES, ATT_PREP), 0)
    for i in range(S // ATT_PREP):
        r = i * ATT_PREP
        c = c_ref[0, r:r + ATT_PREP, :]
        hi = c.astype(BF16)
        r1 = c - hi.astype(F32)
        mid = r1.astype(BF16)
        lo = (r1 - mid.astype(F32)).astype(BF16)
        ext = _dot(hi, sel[0]) + _dot(mid, sel[1]) + _dot(lo, sel[2])
        kx_ref[r:r + ATT_PREP, 0:LANES] = k_ref[0, r:r + ATT_PREP, :]
        kx_ref[r:r + ATT_PREP, LANES:2 * LANES] = ext.astype(BF16)
        vt = vt_ref[0, :, r:r + ATT_PREP].astype(F32)
        ci, off = r // TK, r % TK
        vat_ref[ci, :, off:off + ATT_PREP] = jnp.where(
            vrow < FOX_HEAD_DIM, vt, jnp.where(vrow == FOX_HEAD_DIM, 1.0, 0.0)).astype(BF16)
        vbt_ref[ci, :, off:off + ATT_PREP] = jnp.where(
            vrow >= FOX_HEAD_DIM, vt, jnp.where(vrow == 0, 1.0, 0.0)).astype(BF16)

    lane = lax.broadcasted_iota(jnp.int32, (TQ, LANES), 1)
    first = lane < FOX_HEAD_DIM
    neg_a = jnp.where(lane < 3, -1.0, 0.0).astype(BF16)
    neg_b = jnp.where(lane < 3, 0.0, jnp.where(lane < 6, -1.0, 0.0)).astype(BF16)
    for slot in range(2):
        qs_ref[slot, 0:TQ, LANES:2 * LANES] = neg_a
        qs_ref[slot, TQ:2 * TQ, LANES:2 * LANES] = neg_b

    def prep_q(qb, slot):
        q = q_ref[0, pl.ds(pl.multiple_of(qb * TQ, TQ), TQ), :]
        zero = jnp.zeros_like(q)
        qs_ref[slot, 0:TQ, 0:LANES] = jnp.where(first, q, zero)
        qs_ref[slot, TQ:2 * TQ, 0:LANES] = jnp.where(first, zero, q)

    def qk_t(k, q):
        return lax.dot_general(k, q, (((1,), (1,)), ((), ())), preferred_element_type=F32)

    def colmax(a):
        return jnp.max(a, axis=0, keepdims=True)

    def causal(s):
        row = lax.broadcasted_iota(jnp.int32, s.shape, 0)
        col = lax.broadcasted_iota(jnp.int32, s.shape, 1)
        return jnp.where(row <= col, s, NEG_BIG)

    def next_full(nxt_ref, qslot, ci):
        k = kx_ref[pl.ds(pl.multiple_of(ci * TK, TK), TK), :]
        s = qk_t(k, qs_ref[qslot])
        nxt_ref[...] = s
        return [(0, 2 * TQ, colmax(s))]

    def next_diag(nxt_ref, qslot, ci):
        r = pl.multiple_of(ci * TK, TK)
        q = qs_ref[qslot]
        top = qk_t(kx_ref[pl.ds(r, H), :], q)
        top = jnp.concatenate([causal(top[:, 0:H]), top[:, H:TQ],
                               causal(top[:, TQ:TQ + H]), top[:, TQ + H:]], axis=1)
        q_hi = jnp.concatenate([q[H:TQ], q[TQ + H:]], axis=0)
        bot = qk_t(kx_ref[pl.ds(pl.multiple_of(r + H, H), H), :], q_hi)
        bot = jnp.concatenate([causal(bot[:, :H]), causal(bot[:, H:])], axis=1)
        nxt_ref[0:H, :] = top
        nxt_ref[H:TK, H:TQ] = bot[:, :H]
        nxt_ref[H:TK, TQ + H:] = bot[:, H:]
        return [(0, H, colmax(top[:, 0:H])),
                (H, TQ, jnp.maximum(colmax(top[:, H:TQ]), colmax(bot[:, :H]))),
                (TQ, TQ + H, colmax(top[:, TQ:TQ + H])),
                (TQ + H, 2 * TQ, jnp.maximum(colmax(top[:, TQ + H:]), colmax(bot[:, H:])))]

    def pv_full(cur_ref, ci, m_row):
        p = jnp.exp2(cur_ref[...] - m_row).astype(BF16)
        return jnp.concatenate([_dot(vat_ref[ci], p[:, :TQ]),
                                _dot(vbt_ref[ci], p[:, TQ:])], axis=1)

    def pv_diag(cur_ref, ci, m_row):
        p_top = jnp.exp2(cur_ref[0:H, :] - m_row).astype(BF16)
        out = []
        for head, vh_ref in enumerate((vat_ref, vbt_ref)):
            lo, hi = head * TQ, head * TQ + H
            pv = _dot(vh_ref[ci, :, 0:H], p_top[:, lo:lo + TQ])
            p_bot = jnp.exp2(cur_ref[H:TK, hi:hi + H] - m_row[:, hi:hi + H])
            pv_hi = _dot(vh_ref[ci, :, H:TK], p_bot.astype(BF16))
            out += [pv[:, :H], pv[:, H:] + pv_hi]
        return jnp.concatenate(out, axis=1)

    def step_on(cur_ref, nxt_ref, ci, qslot_n, ci_n, cur_diag, nxt_diag):
        col_maxes = (next_diag if nxt_diag else next_full)(nxt_ref, qslot_n, ci_n)
        m_cur = m_ref[...]
        alpha = alpha_ref[0:1, :]
        for lo, hi, mx in col_maxes:
            if cur_diag:
                m_ref[:, lo:hi] = jnp.broadcast_to(mx, (SUBLANES, hi - lo))
                alpha_ref[:, lo:hi] = jnp.zeros((SUBLANES, hi - lo), F32)
            else:
                m_next = jnp.maximum(m_cur[:, lo:hi], mx)
                m_ref[:, lo:hi] = m_next
                alpha_ref[:, lo:hi] = jnp.exp2(m_cur[:, lo:hi] - m_next)
        pv = (pv_diag if cur_diag else pv_full)(cur_ref, ci, m_cur[0:1, :])
        acc_ref[...] = alpha * acc_ref[...] + pv

    def step(cnt, *args):
        lax.cond((cnt & 1) == 0,
                 lambda: step_on(s0_ref, s1_ref, *args),
                 lambda: step_on(s1_ref, s0_ref, *args))
        return cnt + 1

    orow = lax.broadcasted_iota(jnp.int32, (LANES, TQ), 0)

    def q_block(qb, cnt):
        qslot = qb & 1
        prep_q(jnp.minimum(qb + 1, nq - 1), 1 - qslot)
        acc_ref[...] = jnp.zeros_like(acc_ref)
        last = qb

        def inner(ci, cnt):
            return step(cnt, ci, qslot, ci + 1, False, False)

        cnt = lax.fori_loop(0, last - 1, inner, cnt)

        def before_diag(ci, cnt):
            return step(cnt, ci, qslot, ci + 1, False, True)

        cnt = lax.fori_loop(jnp.maximum(last - 1, 0), last, before_diag, cnt)
        cnt = step(cnt, last, 1 - qslot, 0, True, False)

        acc = acc_ref[...]
        o_t = jnp.where(orow < FOX_HEAD_DIM,
                        acc[:, :TQ] / acc[FOX_HEAD_DIM:FOX_HEAD_DIM + 1, :TQ],
                        acc[:, TQ:] / acc[0:1, TQ:])
        o_ref[0, pl.ds(pl.multiple_of(qb * TQ, TQ), TQ), :] = o_t.T.astype(BF16)
        return cnt

    prep_q(0, 0)
    for lo, hi, mx in next_diag(s0_ref, 0, 0):
        m_ref[:, lo:hi] = jnp.broadcast_to(mx, (SUBLANES, hi - lo))
    alpha_ref[...] = jnp.zeros_like(alpha_ref)
    lax.fori_loop(0, nq, q_block, 0)


def _fox_attn_t(qk, vt, c):
    B, S, _ = qk.shape
    pairs = FOX_HEADS // 2
    return pl.pallas_call(
        _fox_attn_t_kernel,
        grid=(B, pairs),
        in_specs=[
            pl.BlockSpec((1, S, LANES), lambda b, p: (b, 0, p)),
            pl.BlockSpec((1, S, LANES), lambda b, p: (b, 0, pairs + p)),
            pl.BlockSpec((1, LANES, S), lambda b, p: (b, p, 0)),
            pl.BlockSpec((1, S, LANES), lambda b, p: (b, 0, 0)),
        ],
        out_specs=pl.BlockSpec((1, S, LANES), lambda b, p: (b, 0, p)),
        out_shape=jax.ShapeDtypeStruct((B, S, D_MODEL), BF16),
        scratch_shapes=[
            pltpu.VMEM((2, 2 * TQ, 2 * LANES), BF16),
            pltpu.VMEM((S, 2 * LANES), BF16),
            pltpu.VMEM((S // TK, LANES, TK), BF16),
            pltpu.VMEM((S // TK, LANES, TK), BF16),
            pltpu.VMEM((TK, 2 * TQ), F32),
            pltpu.VMEM((TK, 2 * TQ), F32),
            pltpu.VMEM((SUBLANES, 2 * TQ), F32),
            pltpu.VMEM((SUBLANES, 2 * TQ), F32),
            pltpu.VMEM((LANES, 2 * TQ), F32),
        ],
        compiler_params=pltpu.CompilerParams(
            dimension_semantics=("parallel", "parallel"),
            vmem_limit_bytes=VMEM_LIMIT_BYTES),
        name="fox_attn",
    )(qk, qk, vt, c)


def _ffn_kernel(*refs, has_mix, final_norm, seg_in, seg_out):
    refs = list(refs)
    x_ref = refs.pop(0)
    if has_mix:
        a_ref = refs.pop(0)
        wmix_ref = refs.pop(0)
    g_ref, wgu_ref, wd_ref = refs[:3]
    refs = refs[3:]
    if final_norm:
        gf_ref = refs.pop(0)
    if seg_in or seg_out:
        o_ref, act_ref, slab_ref = refs
    else:
        o_ref, act_ref = refs
    n_slabs = D_MODEL // LANES
    seg_rows = [(ch * S5_CHUNK + sg, ch * S5_CHUNK + sg * S5_TJ)
                for ch in range(TM // S5_CHUNK) for sg in range(S5_SEGS)]

    if seg_in:
        cols = []
        for c in range(n_slabs):
            slab_ref[c] = x_ref[:, c * LANES:(c + 1) * LANES]
            cols.append(jnp.concatenate(
                [slab_ref[c, pl.ds(r_seg, S5_TJ, stride=S5_SEGS), :] for r_seg, _ in seg_rows],
                axis=0))
        x = jnp.concatenate(cols, axis=1)
    else:
        x = x_ref[...]
    if has_mix:
        x = x + _dot(a_ref[...], wmix_ref[...])
    h = _rms(x, g_ref[...]).astype(BF16)
    for c in range(D_FF // FF_CHUNK):
        lo = c * FF_CHUNK
        gate = _dot(h, wgu_ref[:, lo:lo + FF_CHUNK])
        up = _dot(h, wgu_ref[:, D_FF + lo:D_FF + lo + FF_CHUNK])
        act_ref[:, lo:lo + FF_CHUNK] = (gate * jax.nn.sigmoid(gate) * up).astype(BF16)
    y = x + _dot(act_ref[...], wd_ref[...])
    if final_norm:
        y = _rms(y, gf_ref[...])
    if seg_out:
        for c in range(n_slabs):
            for r_seg, r_time in seg_rows:
                slab_ref[c, pl.ds(r_seg, S5_TJ, stride=S5_SEGS), :] = (
                    y[r_time:r_time + S5_TJ, c * LANES:(c + 1) * LANES])
            o_ref[:, c * LANES:(c + 1) * LANES] = slab_ref[c]
    else:
        o_ref[...] = y


def _ffn(x, g, wgu, wd, i, mix=None, final_g=None, seg_in=False, seg_out=False):
    N, D = x.shape
    row = pl.BlockSpec((TM, D), lambda t: (t, 0))
    scratch = [pltpu.VMEM((TM, D_FF), BF16)]
    if seg_in or seg_out:
        scratch.append(pltpu.VMEM((D // LANES, TM, LANES), F32))
    args, specs = [x], [row]
    if mix is not None:
        a2, wmix, j = mix
        args += [a2, wmix]
        specs += [row, _layer(wmix, j)]
    args += [g, wgu, wd]
    specs += [_layer(g, i), _layer(wgu, i), _layer(wd, i)]
    if final_g is not None:
        args.append(final_g)
        specs.append(_resident(final_g.shape))
    return pl.pallas_call(
        functools.partial(_ffn_kernel, has_mix=mix is not None,
                          final_norm=final_g is not None, seg_in=seg_in, seg_out=seg_out),
        grid=(N // TM,),
        in_specs=specs,
        out_specs=row,
        out_shape=jax.ShapeDtypeStruct((N, D), F32),
        scratch_shapes=scratch,
        compiler_params=pltpu.CompilerParams(
            dimension_semantics=("parallel",),
            vmem_limit_bytes=VMEM_LIMIT_BYTES),
        name="ffn",
    )(*args)


def _s5_kernel(x_ref, g_ref, win_ref, bblk_ref, cblk_ref, lam_ref,
               d_ref, wglu_ref, o_ref, st_ref, carry_ref):
    @pl.when(pl.program_id(1) == 0)
    def _():
        carry_ref[...] = jnp.zeros_like(carry_ref)

    x = x_ref[0]
    h = _rms(x, g_ref[...]).astype(BF16)
    u = _dot(h, win_ref[...])
    ub = u.astype(BF16)

    sub = lax.broadcasted_iota(jnp.int32, (SUBLANES, LANES), 0)
    vshape = (SUBLANES, LANES)
    blocks_per_slab = S5_SLAB_HALF // LANES
    n_blocks = S5_SLABS * blocks_per_slab

    def cmul(ar, ai, br, bi):
        return ar * br - ai * bi, ar * bi + ai * br

    groups_per_slab = blocks_per_slab // S5_COLS_PER_STEP
    ys = []
    for grp in range(n_blocks // S5_COLS_PER_STEP):
        sl = grp // groups_per_slab
        if grp % groups_per_slab == 0:
            st_ref[:, sl * S5_SLAB_W:(sl + 1) * S5_SLAB_W] = _dot(
                ub[:, sl * S5_SLAB_IN:(sl + 1) * S5_SLAB_IN], bblk_ref[sl])
        offs = []
        for b in range(grp * S5_COLS_PER_STEP, (grp + 1) * S5_COLS_PER_STEP):
            re = (b // blocks_per_slab) * S5_SLAB_W + (b % blocks_per_slab) * LANES
            offs.append((re, re + S5_SLAB_HALF))
        lam = [(jnp.broadcast_to(lam_ref[:, re:re + LANES], vshape),
                jnp.broadcast_to(lam_ref[:, im:im + LANES], vshape)) for re, im in offs]

        def scan_body(j, state):
            r = j * SUBLANES
            new = []
            for (re, im), (lr, li), (sr, si) in zip(offs, lam, state):
                pr, pi = cmul(lr, li, sr, si)
                nr = pr + st_ref[pl.ds(r, SUBLANES), re:re + LANES]
                ni = pi + st_ref[pl.ds(r, SUBLANES), im:im + LANES]
                st_ref[pl.ds(r, SUBLANES), re:re + LANES] = nr
                st_ref[pl.ds(r, SUBLANES), im:im + LANES] = ni
                new.append((nr, ni))
            return tuple(new)

        init = tuple((carry_ref[:, re:re + LANES], carry_ref[:, im:im + LANES])
                     for re, im in offs)
        fin = init
        for j in range(S5_TJ):
            fin = scan_body(j, fin)

        carries = []
        for (re, im), (lr, li), (fr, fi) in zip(offs, lam, fin):
            for _ in range(S5_TJ.bit_length() - 1):
                lr, li = cmul(lr, li, lr, li)
            er, ei = fr, fi
            for k in range(1, S5_SEGS):
                pr, pi = cmul(lr, li, pltpu.roll(er, 1, 0), pltpu.roll(ei, 1, 0))
                er = jnp.where(sub == k, fr + pr, er)
                ei = jnp.where(sub == k, fi + pi, ei)
            sr, si = pltpu.roll(er, 1, 0), pltpu.roll(ei, 1, 0)
            carry_ref[:, re:re + LANES] = jnp.where(sub == 0, sr, 0.0)
            carry_ref[:, im:im + LANES] = jnp.where(sub == 0, si, 0.0)
            carries.append((jnp.where(sub == 0, 0.0, sr), jnp.where(sub == 0, 0.0, si)))

        def fix_body(j, f):
            r = j * SUBLANES
            new = []
            for (re, im), (lr, li), (fr, fi) in zip(offs, lam, f):
                fr, fi = cmul(lr, li, fr, fi)
                st_ref[pl.ds(r, SUBLANES), re:re + LANES] += fr
                st_ref[pl.ds(r, SUBLANES), im:im + LANES] += fi
                new.append((fr, fi))
            return tuple(new)

        f = tuple(carries)
        for j in range(S5_TJ):
            f = fix_body(j, f)

        if grp % groups_per_slab == groups_per_slab - 1:
            ys.append(_dot(st_ref[:, sl * S5_SLAB_W:(sl + 1) * S5_SLAB_W].astype(BF16),
                           cblk_ref[sl]))

    y = jnp.concatenate(ys, axis=1) + d_ref[...] * u
    gl = jax.nn.gelu(y).astype(BF16)
    vg = _dot(gl, wglu_ref[...])
    o_ref[0] = x + vg[:, :D_MODEL] * jax.nn.sigmoid(vg[:, D_MODEL:])


def _s5(xp, g, i, win, bblk, cblk, lam, d, wglu, j):
    B, S, D = xp.shape
    return pl.pallas_call(
        _s5_kernel,
        grid=(B, S // S5_CHUNK),
        in_specs=[
            pl.BlockSpec((1, S5_CHUNK, D), lambda b, c: (b, c, 0)),
            _layer(g, i), _layer(win, j), _resident(bblk.shape),
            _resident(cblk.shape), _resident(lam.shape),
            _layer(d, j), _layer(wglu, j),
        ],
        out_specs=pl.BlockSpec((1, S5_CHUNK, D), lambda b, c: (b, c, 0)),
        out_shape=jax.ShapeDtypeStruct((B, S, D), F32),
        scratch_shapes=[
            pltpu.VMEM((S5_CHUNK, S5_SW), F32),
            pltpu.VMEM((SUBLANES, S5_SW), F32),
        ],
        compiler_params=pltpu.CompilerParams(
            dimension_semantics=("parallel", "arbitrary"),
            vmem_limit_bytes=VMEM_LIMIT_BYTES),
        name="s5",
    )(xp, g, win, bblk, cblk, lam, d, wglu)


def _s5_params(a_re, a_im, log_dt, b_re, b_im, c_re, c_im):
    ar, ai = a_re.astype(F32), a_im.astype(F32)
    dt = jnp.exp(log_dt.astype(F32))[:, None]
    mag = jnp.exp(ar * dt)
    lr, li = mag * jnp.cos(ai * dt), mag * jnp.sin(ai * dt)
    den = ar * ar + ai * ai
    kr = ((lr - 1.0) * ar + li * ai) / den
    ki = (li * ar - (lr - 1.0) * ai) / den
    br, bi = b_re.astype(F32), b_im.astype(F32)
    bbr = kr[..., None] * br - ki[..., None] * bi
    bbi = kr[..., None] * bi + ki[..., None] * br

    gs = S5_GROUPS // S5_SLABS
    eye = jnp.eye(gs, dtype=F32)

    def state_row(zr, zi):
        zr = zr.reshape(S5_SLABS, S5_SLAB_HALF)
        zi = zi.reshape(S5_SLABS, S5_SLAB_HALF)
        return jnp.concatenate([zr, zi], axis=-1).reshape(S5_SW)

    def in_block(b):
        b4 = b.reshape(S5_SLABS, gs, S5_STATE, S5_GROUP)
        return jnp.einsum('sgpc,gh->sgchp', b4, eye).reshape(S5_SLABS, S5_SLAB_IN, S5_SLAB_HALF)

    def out_block(c):
        c4 = c.reshape(S5_SLABS, gs, S5_GROUP, S5_STATE)
        return jnp.einsum('sgcp,gh->shpgc', c4, eye).reshape(S5_SLABS, S5_SLAB_HALF, S5_SLAB_IN)

    bblk = jnp.concatenate([in_block(bbr), in_block(bbi)], axis=2)
    cblk = jnp.concatenate([out_block(c_re.astype(F32)), -out_block(c_im.astype(F32))], axis=1)
    return bblk.astype(BF16), cblk.astype(BF16), state_row(lr, li)[None]


POOL_HALO = 32


def _pool_kernel(x_ref, halo_ref, g_ref, w_ref, b_ref, sc_ref, o_ref, a_ref, b2_ref):
    t = pl.program_id(1)
    x = x_ref[0]
    g = g_ref[...]
    h = _rms(x, g)
    hh = _rms(halo_ref[0], g)
    a_ref[0:POOL_HALO] = jnp.where(t > 0, hh, 0.0)
    a_ref[POOL_HALO:] = h
    b2_ref[0:SUBLANES] = jnp.zeros((SUBLANES, D_MODEL), F32)
    rows = POOL_HALO + TM
    tpos = t * TM + lax.broadcasted_iota(jnp.int32, (TM, POOL_WIDTH), 0) + 1
    outs = []
    for gi, w in enumerate(POOL_WINDOWS):
        lo = gi * POOL_WIDTH
        cols = slice(lo, lo + POOL_WIDTH)
        src, dst, sh = a_ref, b2_ref, 1
        while sh < w:
            dst[SUBLANES:rows, cols] = src[SUBLANES:rows, cols] + src[SUBLANES - sh:rows - sh, cols]
            src, dst, sh = dst, src, 2 * sh
        mean = src[POOL_HALO:rows, cols] / jnp.minimum(tpos, w).astype(F32)
        outs.append(_dot((mean - h[:, cols]).astype(BF16), w_ref[gi]))
    y = (jnp.concatenate(outs, axis=1) + b_ref[...]) * sc_ref[...]
    o_ref[0] = x + y


def _pool(x, g, i, w, b, sc, j):
    B, S, D = x.shape
    per = TM // POOL_HALO
    return pl.pallas_call(
        _pool_kernel,
        grid=(B, S // TM),
        in_specs=[
            pl.BlockSpec((1, TM, D), lambda b, t: (b, t, 0)),
            pl.BlockSpec((1, POOL_HALO, D), lambda b, t: (b, jnp.maximum(t * per - 1, 0), 0)),
            _layer(g, i), _layer(w, j), _layer(b, j), _layer(sc, j),
        ],
        out_specs=pl.BlockSpec((1, TM, D), lambda b, t: (b, t, 0)),
        out_shape=jax.ShapeDtypeStruct((B, S, D), F32),
        scratch_shapes=[pltpu.VMEM((POOL_HALO + TM, D), F32),
                        pltpu.VMEM((POOL_HALO + TM, D), F32)],
        compiler_params=pltpu.CompilerParams(
            dimension_semantics=("parallel", "arbitrary"),
            vmem_limit_bytes=VMEM_LIMIT_BYTES),
        name="pool",
    )(x, x, g, w, b, sc)


def kernel(x, mix_norm_g, ffn_norm_g, final_norm_g, fox_w_in, fox_b_f, fox_w_out, s5_w_in, s5_a_re, s5_a_im, s5_log_dt, s5_b_re, s5_b_im, s5_c_re, s5_c_im, s5_d, s5_w_glu, pool_w, pool_b, pool_scale, ffn_w_gate_up, ffn_w_down):
    B, S, D = x.shape
    N = B * S
    tri = jnp.triu(jnp.ones((TM, TM), F32)).astype(BF16)
    row = lambda a: a.astype(F32).reshape(a.shape[0], 1, a.shape[1])
    gm_all, gf_all = row(mix_norm_g), row(ffn_norm_g)
    wgu_all, wd_all = ffn_w_gate_up.astype(BF16), ffn_w_down.astype(BF16)
    fox_win = fox_w_in.astype(BF16)
    fox_wft = lax.optimization_barrier(fox_w_in[:, :, 3 * D:]).transpose(0, 2, 1).astype(BF16)
    fox_wvt = lax.optimization_barrier(fox_w_in[:, :, 2 * D:3 * D]).transpose(0, 2, 1).astype(BF16)
    fox_bf = fox_b_f.astype(F32)[:, :, None]
    fox_wout = fox_w_out.astype(BF16)
    s5_win, s5_wglu, s5_dd = s5_w_in.astype(BF16), s5_w_glu.astype(BF16), row(s5_d)
    pool_ww, pool_bb, pool_sc = pool_w.astype(BF16), row(pool_b), row(pool_scale)

    for i in range(DEPTH):
        kind, j = i % 3, i // 3
        next_is_s5 = i + 1 < DEPTH and (i + 1) % 3 == 1
        final_g = final_norm_g[None].astype(F32) if i == DEPTH - 1 else None
        mix = None
        if kind == 0:
            qk, vt, c = _fox_proj(x, gm_all, i, fox_win, fox_wvt, fox_wft, fox_bf, j, tri)
            o = _fox_attn_t(qk, vt, c)
            mix = (o.reshape(N, D), fox_wout, j)
        elif kind == 1:
            bblk, cblk, lam = _s5_params(
                s5_a_re[j], s5_a_im[j], s5_log_dt[j], s5_b_re[j], s5_b_im[j], s5_c_re[j], s5_c_im[j])
            x = _s5(x, gm_all, i, s5_win, bblk, cblk, lam, s5_dd, s5_wglu, j)
        else:
            x = _pool(x, gm_all, i, pool_ww, pool_bb, pool_sc, j)
        x = _ffn(x.reshape(N, D), gf_all, wgu_all, wd_all, i, mix=mix, final_g=final_g,
                 seg_in=kind == 1, seg_out=next_is_s5).reshape(B, S, D)
    return x
```

```python
import functools
import math

import jax
import jax.numpy as jnp
from jax import lax
from jax.experimental import pallas as pl
from jax.experimental.pallas import tpu as pltpu

F32 = jnp.float32
BF16 = jnp.bfloat16

D_MODEL = 1024
DEPTH = 4
EPS = 1e-6
FOX_HEADS = 16
FOX_HEAD_DIM = 64
S5_GROUP = 16
S5_GROUPS = 64
S5_STATE = 64
POOL_WINDOWS = (2, 4, 8, 16)
POOL_WIDTH = 256
D_FF = 2816

LANES = 128
SUBLANES = 8
MXU_DIM = 256
VMEM_LIMIT_BYTES = 56 * 1024 * 1024

LOG2E = 1.4426950408889634
NEG_BIG = -1e30

TM = 512
FF_CHUNK = MXU_DIM
TQ = 1024
TK = 1024
S5_SEGS = SUBLANES
S5_TJ = 32
S5_CHUNK = S5_SEGS * S5_TJ
S5_SLABS = 4
S5_SLAB_IN = D_MODEL // S5_SLABS
S5_SLAB_HALF = (S5_GROUPS // S5_SLABS) * S5_STATE
S5_SLAB_W = 2 * S5_SLAB_HALF
S5_SW = S5_SLABS * S5_SLAB_W
S5_COLS_PER_STEP = 4


def _resident(shape):
    nd = len(shape)
    return pl.BlockSpec(shape, lambda *_: (0,) * nd, pipeline_mode=pl.Buffered(1))


def _layer(arr, i):
    nd = arr.ndim
    return pl.BlockSpec((None,) + arr.shape[1:], lambda *_: (i,) + (0,) * (nd - 1),
                        pipeline_mode=pl.Buffered(1))


def _rms(x, g):
    return x * lax.rsqrt(jnp.mean(x * x, axis=-1, keepdims=True) + EPS) * g


def _dot(a, b):
    return jnp.dot(a, b, preferred_element_type=F32)


def _fox_proj_kernel(x_ref, g_ref, wqkv_ref, wft_ref, bf_ref, tri_ref,
                     qkv_ref, c_ref, carry_ref):
    @pl.when(pl.program_id(1) == 0)
    def _():
        carry_ref[...] = jnp.zeros_like(carry_ref)

    h = _rms(x_ref[0], g_ref[...]).astype(BF16)
    qkv = _dot(h, wqkv_ref[:, :3 * D_MODEL])
    q = qkv[:, :D_MODEL] * (FOX_HEAD_DIM ** -0.5 * LOG2E)
    qkv_ref[0, :, :D_MODEL] = q.astype(BF16)
    qkv_ref[0, :, D_MODEL:] = qkv[:, D_MODEL:].astype(BF16)

    z = lax.dot_general(wft_ref[...], h, (((1,), (1,)), ((), ())),
                        preferred_element_type=F32) + bf_ref[...]
    logf = jnp.minimum(z, 0.0) - jnp.log(1.0 + jnp.exp(-jnp.abs(z)))
    hi = logf.astype(BF16)
    r1 = logf - hi.astype(F32)
    mid = r1.astype(BF16)
    lo = (r1 - mid.astype(F32)).astype(BF16)
    parts = _dot(jnp.concatenate([hi, mid, lo], axis=0), tri_ref[...])
    c = (parts[:FOX_HEADS] + parts[FOX_HEADS:2 * FOX_HEADS] + parts[2 * FOX_HEADS:]
         + carry_ref[:, 0:1])
    c_ref[0] = c * LOG2E
    carry_ref[...] = jnp.broadcast_to(c[:, TM - 1:TM], carry_ref.shape)


def _fox_proj(x, g, i, win, wft, bf, j, tri):
    B, S, D = x.shape
    return pl.pallas_call(
        _fox_proj_kernel,
        grid=(B, S // TM),
        in_specs=[
            pl.BlockSpec((1, TM, D), lambda b, t: (b, t, 0)),
            _layer(g, i), _layer(win, j), _layer(wft, j), _layer(bf, j),
            _resident((TM, TM)),
        ],
        out_specs=[
            pl.BlockSpec((1, TM, 3 * D), lambda b, t: (b, t, 0)),
            pl.BlockSpec((1, FOX_HEADS, TM), lambda b, t: (b, 0, t)),
        ],
        out_shape=[
            jax.ShapeDtypeStruct((B, S, 3 * D), BF16),
            jax.ShapeDtypeStruct((B, FOX_HEADS, S), F32),
        ],
        scratch_shapes=[pltpu.VMEM((FOX_HEADS, LANES), F32)],
        compiler_params=pltpu.CompilerParams(
            dimension_semantics=("parallel", "arbitrary"),
            vmem_limit_bytes=VMEM_LIMIT_BYTES),
        name="fox_proj",
    )(x, g, win, wft, bf, tri)


def _fox_attn_kernel(q_ref, k_ref, v_ref, c_ref, o_ref,
                     qs_ref, va_ref, vb_ref, s0_ref, s1_ref, m_ref, alpha_ref, acc_ref):
    assert TQ == TK
    S = q_ref.shape[1]
    nq = S // TQ
    lane = lax.broadcasted_iota(jnp.int32, (TQ, LANES), 1)
    first = lane < FOX_HEAD_DIM

    def prep_v(i, c):
        r = pl.multiple_of(i * TK, TK)
        v = v_ref[0, pl.ds(r, TK), :].astype(F32)
        ln = lax.broadcasted_iota(jnp.int32, v.shape, 1)
        va = jnp.where(ln < FOX_HEAD_DIM, v, jnp.where(ln == FOX_HEAD_DIM, 1.0, 0.0))
        vb = jnp.where(ln >= FOX_HEAD_DIM, v, jnp.where(ln == 0, 1.0, 0.0))
        va_ref[pl.ds(r, TK), :] = va.astype(BF16)
        vb_ref[pl.ds(r, TK), :] = vb.astype(BF16)
        return c

    lax.fori_loop(0, S // TK, prep_v, 0)

    def prep_q(qb, slot):
        q = q_ref[0, pl.ds(pl.multiple_of(qb * TQ, TQ), TQ), :]
        zero = jnp.zeros_like(q)
        qs_ref[slot, 0:TQ] = jnp.where(first, q, zero)
        qs_ref[slot, TQ:2 * TQ] = jnp.where(first, zero, q)

    H = TQ // 2

    def qk(q, k):
        return lax.dot_general(q, k, (((1,), (1,)), ((), ())), preferred_element_type=F32)

    def key_bias(ck, lo, width, rows):
        return jnp.concatenate(
            [jnp.broadcast_to(ck[0:1, lo:lo + width], (rows, width)),
             jnp.broadcast_to(ck[1:2, lo:lo + width], (rows, width))], axis=0)

    def causal(s):
        row = lax.broadcasted_iota(jnp.int32, s.shape, 0)
        col = lax.broadcasted_iota(jnp.int32, s.shape, 1)
        return jnp.where(col <= row, s, NEG_BIG)

    def lanes(m, width):
        return jnp.concatenate([m] * (width // LANES), axis=1)

    def next_full(nxt_ref, qslot, ci):
        k = k_ref[0, pl.ds(pl.multiple_of(ci * TK, TK), TK), :]
        s = qk(qs_ref[qslot], k) - key_bias(c_ref[0, 0, ci], 0, TK, TQ)
        nxt_ref[...] = s
        return [(0, 2 * TQ, jnp.max(s, axis=1, keepdims=True))]

    def next_diag(nxt_ref, qslot, ci):
        r = pl.multiple_of(ci * TK, TK)
        q = qs_ref[qslot]
        ck = c_ref[0, 0, ci]
        left = qk(q, k_ref[0, pl.ds(r, H), :]) - key_bias(ck, 0, H, TQ)
        left = jnp.concatenate([causal(left[0:H]), left[H:TQ],
                                causal(left[TQ:TQ + H]), left[TQ + H:]], axis=0)
        q_hi = jnp.concatenate([q[H:TQ], q[TQ + H:]], axis=0)
        right = qk(q_hi, k_ref[0, pl.ds(pl.multiple_of(r + H, H), H), :])
        right = right - key_bias(ck, H, H, H)
        right = jnp.concatenate([causal(right[:H]), causal(right[H:])], axis=0)
        nxt_ref[:, 0:H] = left
        nxt_ref[H:TQ, H:TK] = right[:H]
        nxt_ref[TQ + H:, H:TK] = right[H:]
        rowmax = lambda a: jnp.max(a, axis=1, keepdims=True)
        return [(0, H, rowmax(left[0:H])),
                (H, TQ, jnp.maximum(rowmax(left[H:TQ]), rowmax(right[:H]))),
                (TQ, TQ + H, rowmax(left[TQ:TQ + H])),
                (TQ + H, 2 * TQ, jnp.maximum(rowmax(left[TQ + H:]), rowmax(right[H:])))]

    def pv_full(cur_ref, ci, m_cur):
        p = jnp.exp2(cur_ref[...] - lanes(m_cur, TK)).astype(BF16)
        r = pl.multiple_of(ci * TK, TK)
        return jnp.concatenate([_dot(p[:TQ], va_ref[pl.ds(r, TK), :]),
                                _dot(p[TQ:], vb_ref[pl.ds(r, TK), :])], axis=0)

    def pv_diag(cur_ref, ci, m_cur):
        r = pl.multiple_of(ci * TK, TK)
        r_hi = pl.multiple_of(r + H, H)
        p_left = jnp.exp2(cur_ref[:, 0:H] - lanes(m_cur, H)).astype(BF16)
        out = []
        for head, vh_ref in enumerate((va_ref, vb_ref)):
            lo, hi = head * TQ, head * TQ + H
            pv = _dot(p_left[lo:lo + TQ], vh_ref[pl.ds(r, H), :])
            p_right = jnp.exp2(cur_ref[hi:hi + H, H:TK] - lanes(m_cur[hi:hi + H], H))
            pv_hi = _dot(p_right.astype(BF16), vh_ref[pl.ds(r_hi, H), :])
            out += [pv[:H], pv[H:] + pv_hi]
        return jnp.concatenate(out, axis=0)

    def step_on(cur_ref, nxt_ref, ci, qslot_n, ci_n, cur_diag, nxt_diag):
        if nxt_diag is None:
            row_maxes = []
        else:
            row_maxes = (next_diag if nxt_diag else next_full)(nxt_ref, qslot_n, ci_n)
        m_cur = m_ref[...]
        alpha = alpha_ref[...]
        for lo, hi, mx in row_maxes:
            if cur_diag:
                m_ref[lo:hi] = jnp.broadcast_to(mx, (hi - lo, LANES))
                alpha_ref[lo:hi] = jnp.zeros((hi - lo, LANES), F32)
            else:
                m_next = jnp.maximum(m_cur[lo:hi], mx)
                m_ref[lo:hi] = m_next
                alpha_ref[lo:hi] = jnp.exp2(m_cur[lo:hi] - m_next)
        pv = (pv_diag if cur_diag else pv_full)(cur_ref, ci, m_cur)
        acc_ref[...] = alpha * acc_ref[...] + pv

    def step(cnt, *args):
        lax.cond((cnt & 1) == 0,
                 lambda: step_on(s0_ref, s1_ref, *args),
                 lambda: step_on(s1_ref, s0_ref, *args))
        return cnt + 1

    def q_block(qb, cnt):
        qslot = qb & 1
        prep_q(jnp.minimum(qb + 1, nq - 1), 1 - qslot)
        acc_ref[...] = jnp.zeros_like(acc_ref)
        last = qb

        def inner(ci, cnt):
            return step(cnt, ci, qslot, ci + 1, False, False)

        cnt = lax.fori_loop(0, last - 1, inner, cnt)

        def before_diag(ci, cnt):
            return step(cnt, ci, qslot, ci + 1, False, True)

        cnt = lax.fori_loop(jnp.maximum(last - 1, 0), last, before_diag, cnt)
        cnt = lax.cond(qb == nq - 1,
                       lambda c: step(c, last, None, None, True, None),
                       lambda c: step(c, last, 1 - qslot, 0, True, False), cnt)

        acc = acc_ref[...]
        la = jnp.broadcast_to(acc[:TQ, FOX_HEAD_DIM:FOX_HEAD_DIM + 1], (TQ, LANES))
        lb = jnp.broadcast_to(acc[TQ:, 0:1], (TQ, LANES))
        o = jnp.where(first, acc[:TQ] / la, acc[TQ:] / lb)
        o_ref[0, pl.ds(pl.multiple_of(qb * TQ, TQ), TQ), :] = o.astype(BF16)
        return cnt

    prep_q(0, 0)
    for lo, hi, mx in next_diag(s0_ref, 0, 0):
        m_ref[lo:hi] = jnp.broadcast_to(mx, (hi - lo, LANES))
    alpha_ref[...] = jnp.zeros_like(alpha_ref)
    lax.fori_loop(0, nq, q_block, 0)


def _fox_attn(qkv, c):
    B, S, _ = qkv.shape
    pairs = FOX_HEADS // 2
    blk = lambda off: pl.BlockSpec((1, S, LANES), lambda b, p: (b, 0, off + p))
    return pl.pallas_call(
        _fox_attn_kernel,
        grid=(B, pairs),
        in_specs=[
            blk(0), blk(pairs), blk(2 * pairs),
            pl.BlockSpec((1, 1, S // TK, 2, TK), lambda b, p: (b, p, 0, 0, 0)),
        ],
        out_specs=pl.BlockSpec((1, S, LANES), lambda b, p: (b, 0, p)),
        out_shape=jax.ShapeDtypeStruct((B, S, D_MODEL), BF16),
        scratch_shapes=[
            pltpu.VMEM((2, 2 * TQ, LANES), BF16),
            pltpu.VMEM((S, LANES), BF16),
            pltpu.VMEM((S, LANES), BF16),
            pltpu.VMEM((2 * TQ, TK), F32),
            pltpu.VMEM((2 * TQ, TK), F32),
            pltpu.VMEM((2 * TQ, LANES), F32),
            pltpu.VMEM((2 * TQ, LANES), F32),
            pltpu.VMEM((2 * TQ, LANES), F32),
        ],
        compiler_params=pltpu.CompilerParams(
            dimension_semantics=("parallel", "parallel"),
            vmem_limit_bytes=VMEM_LIMIT_BYTES),
        name="fox_attn",
    )(qkv, qkv, qkv, c)


ATT_PREP = 512


def _fox_attn_t_kernel(q_ref, k_ref, vt_ref, c_ref, o_ref,
                       qs_ref, kx_ref, vat_ref, vbt_ref, s0_ref, s1_ref,
                       m_ref, alpha_ref, acc_ref):
    assert TQ == TK
    S = q_ref.shape[1]
    nq = S // TQ
    H = TQ // 2
    pair = pl.program_id(1)

    rr = lax.broadcasted_iota(jnp.int32, (LANES, LANES), 0)
    ll = lax.broadcasted_iota(jnp.int32, (LANES, LANES), 1)

    def selector(j):
        one = lambda cond: jnp.where(cond, 1.0, 0.0)
        return (one(rr == 2 * pair) * one(ll == j)
                + one(rr == 2 * pair + 1) * one(ll == 3 + j)).astype(BF16)

    sel = [selector(j) for j in range(3)]
    vrow = lax.broadcasted_iota(jnp.int32, (LANES, ATT_PREP), 0)
    for i in range(S // ATT_PREP):
        r = i * ATT_PREP
        c = c_ref[0, r:r + ATT_PREP, :]
        hi = c.astype(BF16)
        r1 = c - hi.astype(F32)
        mid = r1.astype(BF16)
        lo = (r1 - mid.astype(F32)).astype(BF16)
        ext = _dot(hi, sel[0]) + _dot(mid, sel[1]) + _dot(lo, sel[2])
        kx_ref[r:r + ATT_PREP, 0:LANES] = k_ref[0, r:r + ATT_PREP, :]
        kx_ref[r:r + ATT_PREP, LANES:2 * LANES] = ext.astype(BF16)
        vt = vt_ref[0, :, r:r + ATT_PREP].astype(F32)
        ci, off = r // TK, r % TK
        vat_ref[ci, :, off:off + ATT_PREP] = jnp.where(
            vrow < FOX_HEAD_DIM, vt, jnp.where(vrow == FOX_HEAD_DIM, 1.0, 0.0)).astype(BF16)
        vbt_ref[ci, :, off:off + ATT_PREP] = jnp.where(
            vrow >= FOX_HEAD_DIM, vt, jnp.where(vrow == 0, 1.0, 0.0)).astype(BF16)

    lane = lax.broadcasted_iota(jnp.int32, (TQ, LANES), 1)
    first = lane < FOX_HEAD_DIM
    neg_a = jnp.where(lane < 3, -1.0, 0.0).astype(BF16)
    neg_b = jnp.where(lane < 3, 0.0, jnp.where(lane < 6, -1.0, 0.0)).astype(BF16)
    for slot in range(2):
        qs_ref[slot, 0:TQ, LANES:2 * LANES] = neg_a
        qs_ref[slot, TQ:2 * TQ, LANES:2 * LANES] = neg_b

    def prep_q(qb, slot):
        q = q_ref[0, pl.ds(pl.multiple_of(qb * TQ, TQ), TQ), :]
        zero = jnp.zeros_like(q)
        qs_ref[slot, 0:TQ, 0:LANES] = jnp.where(first, q, zero)
        qs_ref[slot, TQ:2 * TQ, 0:LANES] = jnp.where(first, zero, q)

    def qk_t(k, q):
        return lax.dot_general(k, q, (((1,), (1,)), ((), ())), preferred_element_type=F32)

    def colmax(a):
        return jnp.max(a, axis=0, keepdims=True)

    def causal(s):
        row = lax.broadcasted_iota(jnp.int32, s.shape, 0)
        col = lax.broadcasted_iota(jnp.int32, s.shape, 1)
        return jnp.where(row <= col, s, NEG_BIG)

    def next_full(nxt_ref, qslot, ci):
        k = kx_ref[pl.ds(pl.multiple_of(ci * TK, TK), TK), :]
        s = qk_t(k, qs_ref[qslot])
        nxt_ref[...] = s
        return [(0, 2 * TQ, colmax(s))]

    def next_diag(nxt_ref, qslot, ci):
        r = pl.multiple_of(ci * TK, TK)
        q = qs_ref[qslot]
        top = qk_t(kx_ref[pl.ds(r, H), :], q)
        top = jnp.concatenate([causal(top[:, 0:H]), top[:, H:TQ],
                               causal(top[:, TQ:TQ + H]), top[:, TQ + H:]], axis=1)
        q_hi = jnp.concatenate([q[H:TQ], q[TQ + H:]], axis=0)
        bot = qk_t(kx_ref[pl.ds(pl.multiple_of(r + H, H), H), :], q_hi)
        bot = jnp.concatenate([causal(bot[:, :H]), causal(bot[:, H:])], axis=1)
        nxt_ref[0:H, :] = top
        nxt_ref[H:TK, H:TQ] = bot[:, :H]
        nxt_ref[H:TK, TQ + H:] = bot[:, H:]
        return [(0, H, colmax(top[:, 0:H])),
                (H, TQ, jnp.maximum(colmax(top[:, H:TQ]), colmax(bot[:, :H]))),
                (TQ, TQ + H, colmax(top[:, TQ:TQ + H])),
                (TQ + H, 2 * TQ, jnp.maximum(colmax(top[:, TQ + H:]), colmax(bot[:, H:])))]

    def pv_full(cur_ref, ci, m_row):
        p = jnp.exp2(cur_ref[...] - m_row).astype(BF16)
        return jnp.concatenate([_dot(vat_ref[ci], p[:, :TQ]),
                                _dot(vbt_ref[ci], p[:, TQ:])], axis=1)

    def pv_diag(cur_ref, ci, m_row):
        p_top = jnp.exp2(cur_ref[0:H, :] - m_row).astype(BF16)
        out = []
        for head, vh_ref in enumerate((vat_ref, vbt_ref)):
            lo, hi = head * TQ, head * TQ + H
            pv = _dot(vh_ref[ci, :, 0:H], p_top[:, lo:lo + TQ])
            p_bot = jnp.exp2(cur_ref[H:TK, hi:hi + H] - m_row[:, hi:hi + H])
            pv_hi = _dot(vh_ref[ci, :, H:TK], p_bot.astype(BF16))
            out += [pv[:, :H], pv[:, H:] + pv_hi]
        return jnp.concatenate(out, axis=1)

    def step_on(cur_ref, nxt_ref, ci, qslot_n, ci_n, cur_diag, nxt_diag):
        col_maxes = (next_diag if nxt_diag else next_full)(nxt_ref, qslot_n, ci_n)
        m_cur = m_ref[...]
        alpha = alpha_ref[0:1, :]
        for lo, hi, mx in col_maxes:
            if cur_diag:
                m_ref[:, lo:hi] = jnp.broadcast_to(mx, (SUBLANES, hi - lo))
                alpha_ref[:, lo:hi] = jnp.zeros((SUBLANES, hi - lo), F32)
            else:
                m_next = jnp.maximum(m_cur[:, lo:hi], mx)
                m_ref[:, lo:hi] = m_next
                alpha_ref[:, lo:hi] = jnp.exp2(m_cur[:, lo:hi] - m_next)
        pv = (pv_diag if cur_diag else pv_full)(cur_ref, ci, m_cur[0:1, :])
        acc_ref[...] = alpha * acc_ref[...] + pv

    def step(cnt, *args):
        lax.cond((cnt & 1) == 0,
                 lambda: step_on(s0_ref, s1_ref, *args),
                 lambda: step_on(s1_ref, s0_ref, *args))
        return cnt + 1

    orow = lax.broadcasted_iota(jnp.int32, (LANES, TQ), 0)

    def q_block(qb, cnt):
        qslot = qb & 1
        prep_q(jnp.minimum(qb + 1, nq - 1), 1 - qslot)
        acc_ref[...] = jnp.zeros_like(acc_ref)
        last = qb

        def inner(ci, cnt):
            return step(cnt, ci, qslot, ci + 1, False, False)

        cnt = lax.fori_loop(0, last - 1, inner, cnt)

        def before_diag(ci, cnt):
            return step(cnt, ci, qslot, ci + 1, False, True)

        cnt = lax.fori_loop(jnp.maximum(last - 1, 0), last, before_diag, cnt)
        cnt = step(cnt, last, 1 - qslot, 0, True, False)

        acc = acc_ref[...]
        o_t = jnp.where(orow < FOX_HEAD_DIM,
                        acc[:, :TQ] / acc[FOX_HEAD_DIM:FOX_HEAD_DIM + 1, :TQ],
                        acc[:, TQ:] / acc[0:1, TQ:])
        o_ref[0, pl.ds(pl.multiple_of(qb * TQ, TQ), TQ), :] = o_t.T.astype(BF16)
        return cnt

    prep_q(0, 0)
    for lo, hi, mx in next_diag(s0_ref, 0, 0):
        m_ref[:, lo:hi] = jnp.broadcast_to(mx, (SUBLANES, hi - lo))
    alpha_ref[...] = jnp.zeros_like(alpha_ref)
    lax.fori_loop(0, nq, q_block, 0)


def _fox_attn_t(qk, vt, c):
    B, S, _ = qk.shape
    pairs = FOX_HEADS // 2
    return pl.pallas_call(
        _fox_attn_t_kernel,
        grid=(B, pairs),
        in_specs=[
            pl.BlockSpec((1, S, LANES), lambda b, p: (b, 0, p)),
            pl.BlockSpec((1, S, LANES), lambda b, p: (b, 0, pairs + p)),
            pl.BlockSpec((1, LANES, S), lambda b, p: (b, p, 0)),
            pl.BlockSpec((1, S, LANES), lambda b, p: (b, 0, 0)),
        ],
        out_specs=pl.BlockSpec((1, S, LANES), lambda b, p: (b, 0, p)),
        out_shape=jax.ShapeDtypeStruct((B, S, D_MODEL), BF16),
        scratch_shapes=[
            pltpu.VMEM((2, 2 * TQ, 2 * LANES), BF16),
            pltpu.VMEM((S, 2 * LANES), BF16),
            pltpu.VMEM((S // TK, LANES, TK), BF16),
            pltpu.VMEM((S // TK, LANES, TK), BF16),
            pltpu.VMEM((TK, 2 * TQ), F32),
            pltpu.VMEM((TK, 2 * TQ), F32),
            pltpu.VMEM((SUBLANES, 2 * TQ), F32),
            pltpu.VMEM((SUBLANES, 2 * TQ), F32),
            pltpu.VMEM((LANES, 2 * TQ), F32),
        ],
        compiler_params=pltpu.CompilerParams(
            dimension_semantics=("parallel", "parallel"),
            vmem_limit_bytes=VMEM_LIMIT_BYTES),
        name="fox_attn",
    )(qk, qk, vt, c)


def _ffn_kernel(*refs, has_mix, final_norm, seg_in, seg_out):
    refs = list(refs)
    x_ref = refs.pop(0)
    if has_mix:
        a_ref = refs.pop(0)
        wmix_ref = refs.pop(0)
    g_ref, wgu_ref, wd_ref = refs[:3]
    refs = refs[3:]
    if final_norm:
        gf_ref = refs.pop(0)
    if seg_in or seg_out:
        o_ref, act_ref, slab_ref = refs
    else:
        o_ref, act_ref = refs
    n_slabs = D_MODEL // LANES
    seg_rows = [(ch * S5_CHUNK + sg, ch * S5_CHUNK + sg * S5_TJ)
                for ch in range(TM // S5_CHUNK) for sg in range(S5_SEGS)]

    if seg_in:
        cols = []
        for c in range(n_slabs):
            slab_ref[c] = x_ref[:, c * LANES:(c + 1) * LANES]
            cols.append(jnp.concatenate(
                [slab_ref[c, pl.ds(r_seg, S5_TJ, stride=S5_SEGS), :] for r_seg, _ in seg_rows],
                axis=0))
        x = jnp.concatenate(cols, axis=1)
    else:
        x = x_ref[...]
    if has_mix:
        x = x + _dot(a_ref[...], wmix_ref[...])
    h = _rms(x, g_ref[...]).astype(BF16)
    for c in range(D_FF // FF_CHUNK):
        lo = c * FF_CHUNK
        gate = _dot(h, wgu_ref[:, lo:lo + FF_CHUNK])
        up = _dot(h, wgu_ref[:, D_FF + lo:D_FF + lo + FF_CHUNK])
        act_ref[:, lo:lo + FF_CHUNK] = (gate * jax.nn.sigmoid(gate) * up).astype(BF16)
    y = x + _dot(act_ref[...], wd_ref[...])
    if final_norm:
        y = _rms(y, gf_ref[...])
    if seg_out:
        for c in range(n_slabs):
            for r_seg, r_time in seg_rows:
                slab_ref[c, pl.ds(r_seg, S5_TJ, stride=S5_SEGS), :] = (
                    y[r_time:r_time + S5_TJ, c * LANES:(c + 1) * LANES])
            o_ref[:, c * LANES:(c + 1) * LANES] = slab_ref[c]
    else:
        o_ref[...] = y


def _ffn(x, g, wgu, wd, i, mix=None, final_g=None, seg_in=False, seg_out=False):
    N, D = x.shape
    row = pl.BlockSpec((TM, D), lambda t: (t, 0))
    scratch = [pltpu.VMEM((TM, D_FF), BF16)]
    if seg_in or seg_out:
        scratch.append(pltpu.VMEM((D // LANES, TM, LANES), F32))
    args, specs = [x], [row]
    if mix is not None:
        a2, wmix, j = mix
        args += [a2, wmix]
        specs += [row, _layer(wmix, j)]
    args += [g, wgu, wd]
    specs += [_layer(g, i), _layer(wgu, i), _layer(wd, i)]
    if final_g is not None:
        args.append(final_g)
        specs.append(_resident(final_g.shape))
    return pl.pallas_call(
        functools.partial(_ffn_kernel, has_mix=mix is not None,
                          final_norm=final_g is not None, seg_in=seg_in, seg_out=seg_out),
        grid=(N // TM,),
        in_specs=specs,
        out_specs=row,
        out_shape=jax.ShapeDtypeStruct((N, D), F32),
        scratch_shapes=scratch,
        compiler_params=pltpu.CompilerParams(
            dimension_semantics=("parallel",),
            vmem_limit_bytes=VMEM_LIMIT_BYTES),
        name="ffn",
    )(*args)


def _s5_kernel(x_ref, g_ref, win_ref, bblk_ref, cblk_ref, lam_ref,
               d_ref, wglu_ref, o_ref, st_ref, carry_ref):
    @pl.when(pl.program_id(1) == 0)
    def _():
        carry_ref[...] = jnp.zeros_like(carry_ref)

    x = x_ref[0]
    h = _rms(x, g_ref[...]).astype(BF16)
    u = _dot(h, win_ref[...])
    ub = u.astype(BF16)

    sub = lax.broadcasted_iota(jnp.int32, (SUBLANES, LANES), 0)
    vshape = (SUBLANES, LANES)
    blocks_per_slab = S5_SLAB_HALF // LANES
    n_blocks = S5_SLABS * blocks_per_slab

    def cmul(ar, ai, br, bi):
        return ar * br - ai * bi, ar * bi + ai * br

    groups_per_slab = blocks_per_slab // S5_COLS_PER_STEP
    ys = []
    for grp in range(n_blocks // S5_COLS_PER_STEP):
        sl = grp // groups_per_slab
        if grp % groups_per_slab == 0:
            st_ref[:, sl * S5_SLAB_W:(sl + 1) * S5_SLAB_W] = _dot(
                ub[:, sl * S5_SLAB_IN:(sl + 1) * S5_SLAB_IN], bblk_ref[sl])
        offs = []
        for b in range(grp * S5_COLS_PER_STEP, (grp + 1) * S5_COLS_PER_STEP):
            re = (b // blocks_per_slab) * S5_SLAB_W + (b % blocks_per_slab) * LANES
            offs.append((re, re + S5_SLAB_HALF))
        lam = [(jnp.broadcast_to(lam_ref[:, re:re + LANES], vshape),
                jnp.broadcast_to(lam_ref[:, im:im + LANES], vshape)) for re, im in offs]

        def scan_body(j, state):
            r = j * SUBLANES
            new = []
            for (re, im), (lr, li), (sr, si) in zip(offs, lam, state):
                pr, pi = cmul(lr, li, sr, si)
                nr = pr + st_ref[pl.ds(r, SUBLANES), re:re + LANES]
                ni = pi + st_ref[pl.ds(r, SUBLANES), im:im + LANES]
                st_ref[pl.ds(r, SUBLANES), re:re + LANES] = nr
                st_ref[pl.ds(r, SUBLANES), im:im + LANES] = ni
                new.append((nr, ni))
            return tuple(new)

        init = tuple((carry_ref[:, re:re + LANES], carry_ref[:, im:im + LANES])
                     for re, im in offs)
        fin = init
        for j in range(S5_TJ):
            fin = scan_body(j, fin)

        carries = []
        for (re, im), (lr, li), (fr, fi) in zip(offs, lam, fin):
            for _ in range(S5_TJ.bit_length() - 1):
                lr, li = cmul(lr, li, lr, li)
            er, ei = fr, fi
            for k in range(1, S5_SEGS):
                pr, pi = cmul(lr, li, pltpu.roll(er, 1, 0), pltpu.roll(ei, 1, 0))
                er = jnp.where(sub == k, fr + pr, er)
                ei = jnp.where(sub == k, fi + pi, ei)
            sr, si = pltpu.roll(er, 1, 0), pltpu.roll(ei, 1, 0)
            carry_ref[:, re:re + LANES] = jnp.where(sub == 0, sr, 0.0)
            carry_ref[:, im:im + LANES] = jnp.where(sub == 0, si, 0.0)
            carries.append((jnp.where(sub == 0, 0.0, sr), jnp.where(sub == 0, 0.0, si)))

        def fix_body(j, f):
            r = j * SUBLANES
            new = []
            for (re, im), (lr, li), (fr, fi) in zip(offs, lam, f):
                fr, fi = cmul(lr, li, fr, fi)
                st_ref[pl.ds(r, SUBLANES), re:re + LANES] += fr
                st_ref[pl.ds(r, SUBLANES), im:im + LANES] += fi
                new.append((fr, fi))
            return tuple(new)

        f = tuple(carries)
        for j in range(S5_TJ):
            f = fix_body(j, f)

        if grp % groups_per_slab == groups_per_slab - 1:
            ys.append(_dot(st_ref[:, sl * S5_SLAB_W:(sl + 1) * S5_SLAB_W].astype(BF16),
                           cblk_ref[sl]))

    y = jnp.concatenate(ys, axis=1) + d_ref[...] * u
    gl = jax.nn.gelu(y).astype(BF16)
    vg = _dot(gl, wglu_ref[...])
    o_ref[0] = x + vg[:, :D_MODEL] * jax.nn.sigmoid(vg[:, D_MODEL:])


def _s5(xp, g, i, win, bblk, cblk, lam, d, wglu, j):
    B, S, D = xp.shape
    return pl.pallas_call(
        _s5_kernel,
        grid=(B, S // S5_CHUNK),
        in_specs=[
            pl.BlockSpec((1, S5_CHUNK, D), lambda b, c: (b, c, 0)),
            _layer(g, i), _layer(win, j), _resident(bblk.shape),
            _resident(cblk.shape), _resident(lam.shape),
            _layer(d, j), _layer(wglu, j),
        ],
        out_specs=pl.BlockSpec((1, S5_CHUNK, D), lambda b, c: (b, c, 0)),
        out_shape=jax.ShapeDtypeStruct((B, S, D), F32),
        scratch_shapes=[
            pltpu.VMEM((S5_CHUNK, S5_SW), F32),
            pltpu.VMEM((SUBLANES, S5_SW), F32),
        ],
        compiler_params=pltpu.CompilerParams(
            dimension_semantics=("parallel", "arbitrary"),
            vmem_limit_bytes=VMEM_LIMIT_BYTES),
        name="s5",
    )(xp, g, win, bblk, cblk, lam, d, wglu)


def _s5_params(a_re, a_im, log_dt, b_re, b_im, c_re, c_im):
    ar, ai = a_re.astype(F32), a_im.astype(F32)
    dt = jnp.exp(log_dt.astype(F32))[:, None]
    mag = jnp.exp(ar * dt)
    lr, li = mag * jnp.cos(ai * dt), mag * jnp.sin(ai * dt)
    den = ar * ar + ai * ai
    kr = ((lr - 1.0) * ar + li * ai) / den
    ki = (li * ar - (lr - 1.0) * ai) / den
    br, bi = b_re.astype(F32), b_im.astype(F32)
    bbr = kr[..., None] * br - ki[..., None] * bi
    bbi = kr[..., None] * bi + ki[..., None] * br

    gs = S5_GROUPS // S5_SLABS
    eye = jnp.eye(gs, dtype=F32)

    def state_row(zr, zi):
        zr = zr.reshape(S5_SLABS, S5_SLAB_HALF)
        zi = zi.reshape(S5_SLABS, S5_SLAB_HALF)
        return jnp.concatenate([zr, zi], axis=-1).reshape(S5_SW)

    same_group = (eye > 0)[None, :, None, :, None]

    def in_block(b):
        bt = b.reshape(S5_SLABS, gs, S5_STATE, S5_GROUP).transpose(0, 1, 3, 2)
        full = jnp.where(same_group, bt[:, :, :, None, :], 0.0)
        return full.reshape(S5_SLABS, S5_SLAB_IN, S5_SLAB_HALF)

    def out_block(c):
        ct = c.reshape(S5_SLABS, gs, S5_GROUP, S5_STATE).transpose(0, 1, 3, 2)
        full = jnp.where(same_group, ct[:, :, :, None, :], 0.0)
        return full.reshape(S5_SLABS, S5_SLAB_HALF, S5_SLAB_IN)

    bblk = jnp.concatenate([in_block(bbr), in_block(bbi)], axis=2)
    cblk = jnp.concatenate([out_block(c_re.astype(F32)), -out_block(c_im.astype(F32))], axis=1)
    return bblk.astype(BF16), cblk.astype(BF16), state_row(lr, li)[None]


POOL_HALO = 32


def _pool_kernel(x_ref, halo_ref, g_ref, w_ref, b_ref, sc_ref, o_ref, a_ref, b2_ref):
    t = pl.program_id(1)
    x = x_ref[0]
    g = g_ref[...]
    h = _rms(x, g)
    hh = _rms(halo_ref[0], g)
    a_ref[0:POOL_HALO] = jnp.where(t > 0, hh, 0.0)
    a_ref[POOL_HALO:] = h
    b2_ref[0:SUBLANES] = jnp.zeros((SUBLANES, D_MODEL), F32)
    rows = POOL_HALO + TM
    tpos = t * TM + lax.broadcasted_iota(jnp.int32, (TM, POOL_WIDTH), 0) + 1
    outs = []
    for gi, w in enumerate(POOL_WINDOWS):
        lo = gi * POOL_WIDTH
        cols = slice(lo, lo + POOL_WIDTH)
        src, dst, sh = a_ref, b2_ref, 1
        while sh < w:
            dst[SUBLANES:rows, cols] = src[SUBLANES:rows, cols] + src[SUBLANES - sh:rows - sh, cols]
            src, dst, sh = dst, src, 2 * sh
        mean = src[POOL_HALO:rows, cols] / jnp.minimum(tpos, w).astype(F32)
        outs.append(_dot((mean - h[:, cols]).astype(BF16), w_ref[gi]))
    y = (jnp.concatenate(outs, axis=1) + b_ref[...]) * sc_ref[...]
    o_ref[0] = x + y


def _pool(x, g, i, w, b, sc, j):
    B, S, D = x.shape
    per = TM // POOL_HALO
    return pl.pallas_call(
        _pool_kernel,
        grid=(B, S // TM),
        in_specs=[
            pl.BlockSpec((1, TM, D), lambda b, t: (b, t, 0)),
            pl.BlockSpec((1, POOL_HALO, D), lambda b, t: (b, jnp.maximum(t * per - 1, 0), 0)),
            _layer(g, i), _layer(w, j), _layer(b, j), _layer(sc, j),
        ],
        out_specs=pl.BlockSpec((1, TM, D), lambda b, t: (b, t, 0)),
        out_shape=jax.ShapeDtypeStruct((B, S, D), F32),
        scratch_shapes=[pltpu.VMEM((POOL_HALO + TM, D), F32),
                        pltpu.VMEM((POOL_HALO + TM, D), F32)],
        compiler_params=pltpu.CompilerParams(
            dimension_semantics=("parallel", "arbitrary"),
            vmem_limit_bytes=VMEM_LIMIT_BYTES),
        name="pool",
    )(x, x, g, w, b, sc)


def kernel(x, mix_norm_g, ffn_norm_g, final_norm_g, fox_w_in, fox_b_f, fox_w_out, s5_w_in, s5_a_re, s5_a_im, s5_log_dt, s5_b_re, s5_b_im, s5_c_re, s5_c_im, s5_d, s5_w_glu, pool_w, pool_b, pool_scale, ffn_w_gate_up, ffn_w_down):
    B, S, D = x.shape
    N = B * S
    tri = jnp.triu(jnp.ones((TM, TM), F32)).astype(BF16)
    row = lambda a: a.astype(F32).reshape(a.shape[0], 1, a.shape[1])
    gm_all, gf_all = row(mix_norm_g), row(ffn_norm_g)
    wgu_all, wd_all = ffn_w_gate_up.astype(BF16), ffn_w_down.astype(BF16)
    fox_win = fox_w_in.astype(BF16)
    fox_wft = lax.optimization_barrier(fox_w_in[:, :, 3 * D:]).transpose(0, 2, 1).astype(BF16)
    fox_bf = fox_b_f.astype(F32)[:, :, None]
    fox_wout = fox_w_out.astype(BF16)
    s5_win, s5_wglu, s5_dd = s5_w_in.astype(BF16), s5_w_glu.astype(BF16), row(s5_d)
    pool_ww, pool_bb, pool_sc = pool_w.astype(BF16), row(pool_b), row(pool_scale)

    for i in range(DEPTH):
        kind, j = i % 3, i // 3
        next_is_s5 = i + 1 < DEPTH and (i + 1) % 3 == 1
        final_g = final_norm_g[None].astype(F32) if i == DEPTH - 1 else None
        mix = None
        if kind == 0:
            qkv, c = _fox_proj(x, gm_all, i, fox_win, fox_wft, fox_bf, j, tri)
            c = c.reshape(B, FOX_HEADS // 2, 2, S // TK, TK).transpose(0, 1, 3, 2, 4)
            o = _fox_attn(qkv, c)
            mix = (o.reshape(N, D), fox_wout, j)
        elif kind == 1:
            bblk, cblk, lam = _s5_params(
                s5_a_re[j], s5_a_im[j], s5_log_dt[j], s5_b_re[j], s5_b_im[j], s5_c_re[j], s5_c_im[j])
            x = _s5(x, gm_all, i, s5_win, bblk, cblk, lam, s5_dd, s5_wglu, j)
        else:
            x = _pool(x, gm_all, i, pool_ww, pool_bb, pool_sc, j)
        x = _ffn(x.reshape(N, D), gf_all, wgu_all, wd_all, i, mix=mix, final_g=final_g,
                 seg_in=kind == 1, seg_out=next_is_s5).reshape(B, S, D)
    return x
```

```python
import functools
import math

import jax
import jax.numpy as jnp
from jax import lax
from jax.experimental import pallas as pl
from jax.experimental.pallas import tpu as pltpu

F32 = jnp.float32
BF16 = jnp.bfloat16

D_MODEL = 1024
DEPTH = 4
EPS = 1e-6
FOX_HEADS = 16
FOX_HEAD_DIM = 64
S5_GROUP = 16
S5_GROUPS = 64
S5_STATE = 64
POOL_WINDOWS = (2, 4, 8, 16)
POOL_WIDTH = 256
D_FF = 2816

LANES = 128
SUBLANES = 8
MXU_DIM = 256
VMEM_LIMIT_BYTES = 56 * 1024 * 1024

LOG2E = 1.4426950408889634
NEG_BIG = -1e30

TM = 512
FF_CHUNK = MXU_DIM
TQ = 1024
TK = 1024
S5_SEGS = SUBLANES
S5_TJ = 32
S5_CHUNK = S5_SEGS * S5_TJ
S5_SLABS = 4
S5_SLAB_IN = D_MODEL // S5_SLABS
S5_SLAB_HALF = (S5_GROUPS // S5_SLABS) * S5_STATE
S5_SLAB_W = 2 * S5_SLAB_HALF
S5_SW = S5_SLABS * S5_SLAB_W
S5_COLS_PER_STEP = 4


def _resident(shape):
    nd = len(shape)
    return pl.BlockSpec(shape, lambda *_: (0,) * nd, pipeline_mode=pl.Buffered(1))


def _layer(arr, i):
    nd = arr.ndim
    return pl.BlockSpec((None,) + arr.shape[1:], lambda *_: (i,) + (0,) * (nd - 1),
                        pipeline_mode=pl.Buffered(1))


def _rms(x, g):
    return x * lax.rsqrt(jnp.mean(x * x, axis=-1, keepdims=True) + EPS) * g


def _dot(a, b):
    return jnp.dot(a, b, preferred_element_type=F32)


def _fox_proj_kernel(x_ref, g_ref, wqkv_ref, wft_ref, bf_ref, tri_ref,
                     qkv_ref, c_ref, carry_ref):
    @pl.when(pl.program_id(1) == 0)
    def _():
        carry_ref[...] = jnp.zeros_like(carry_ref)

    h = _rms(x_ref[0], g_ref[...]).astype(BF16)
    qkv = _dot(h, wqkv_ref[:, :3 * D_MODEL])
    q = qkv[:, :D_MODEL] * (FOX_HEAD_DIM ** -0.5 * LOG2E)
    qkv_ref[0, :, :D_MODEL] = q.astype(BF16)
    qkv_ref[0, :, D_MODEL:] = qkv[:, D_MODEL:].astype(BF16)

    z = lax.dot_general(wft_ref[...], h, (((1,), (1,)), ((), ())),
                        preferred_element_type=F32) + bf_ref[...]
    logf = jnp.minimum(z, 0.0) - jnp.log(1.0 + jnp.exp(-jnp.abs(z)))
    hi = logf.astype(BF16)
    r1 = logf - hi.astype(F32)
    mid = r1.astype(BF16)
    lo = (r1 - mid.astype(F32)).astype(BF16)
    parts = _dot(jnp.concatenate([hi, mid, lo], axis=0), tri_ref[...])
    c = (parts[:FOX_HEADS] + parts[FOX_HEADS:2 * FOX_HEADS] + parts[2 * FOX_HEADS:]
         + carry_ref[:, 0:1])
    c_ref[0] = c * LOG2E
    carry_ref[...] = jnp.broadcast_to(c[:, TM - 1:TM], carry_ref.shape)


def _fox_proj(x, g, i, win, wft, bf, j, tri):
    B, S, D = x.shape
    return pl.pallas_call(
        _fox_proj_kernel,
        grid=(B, S // TM),
        in_specs=[
            pl.BlockSpec((1, TM, D), lambda b, t: (b, t, 0)),
            _layer(g, i), _layer(win, j), _layer(wft, j), _layer(bf, j),
            _resident((TM, TM)),
        ],
        out_specs=[
            pl.BlockSpec((1, TM, 3 * D), lambda b, t: (b, t, 0)),
            pl.BlockSpec((1, FOX_HEADS, TM), lambda b, t: (b, 0, t)),
        ],
        out_shape=[
            jax.ShapeDtypeStruct((B, S, 3 * D), BF16),
            jax.ShapeDtypeStruct((B, FOX_HEADS, S), F32),
        ],
        scratch_shapes=[pltpu.VMEM((FOX_HEADS, LANES), F32)],
        compiler_params=pltpu.CompilerParams(
            dimension_semantics=("parallel", "arbitrary"),
            vmem_limit_bytes=VMEM_LIMIT_BYTES),
        name="fox_proj",
    )(x, g, win, wft, bf, tri)


def _fox_attn_kernel(q_ref, k_ref, v_ref, c_ref, o_ref,
                     qs_ref, va_ref, vb_ref, s0_ref, s1_ref, m_ref, alpha_ref, acc_ref):
    assert TQ == TK
    S = q_ref.shape[1]
    nq = S // TQ
    lane = lax.broadcasted_iota(jnp.int32, (TQ, LANES), 1)
    first = lane < FOX_HEAD_DIM

    def prep_v(i, c):
        r = pl.multiple_of(i * TK, TK)
        v = v_ref[0, pl.ds(r, TK), :].astype(F32)
        ln = lax.broadcasted_iota(jnp.int32, v.shape, 1)
        va = jnp.where(ln < FOX_HEAD_DIM, v, jnp.where(ln == FOX_HEAD_DIM, 1.0, 0.0))
        vb = jnp.where(ln >= FOX_HEAD_DIM, v, jnp.where(ln == 0, 1.0, 0.0))
        va_ref[pl.ds(r, TK), :] = va.astype(BF16)
        vb_ref[pl.ds(r, TK), :] = vb.astype(BF16)
        return c

    lax.fori_loop(0, S // TK, prep_v, 0)

    def prep_q(qb, slot):
        q = q_ref[0, pl.ds(pl.multiple_of(qb * TQ, TQ), TQ), :]
        zero = jnp.zeros_like(q)
        qs_ref[slot, 0:TQ] = jnp.where(first, q, zero)
        qs_ref[slot, TQ:2 * TQ] = jnp.where(first, zero, q)

    H = TQ // 2

    def qk(q, k):
        return lax.dot_general(q, k, (((1,), (1,)), ((), ())), preferred_element_type=F32)

    def key_bias(ck, lo, width, rows):
        return jnp.concatenate(
            [jnp.broadcast_to(ck[0:1, lo:lo + width], (rows, width)),
             jnp.broadcast_to(ck[1:2, lo:lo + width], (rows, width))], axis=0)

    def causal(s):
        row = lax.broadcasted_iota(jnp.int32, s.shape, 0)
        col = lax.broadcasted_iota(jnp.int32, s.shape, 1)
        return jnp.where(col <= row, s, NEG_BIG)

    def lanes(m, width):
        return jnp.concatenate([m] * (width // LANES), axis=1)

    def next_full(nxt_ref, qslot, ci):
        k = k_ref[0, pl.ds(pl.multiple_of(ci * TK, TK), TK), :]
        s = qk(qs_ref[qslot], k) - key_bias(c_ref[0, 0, ci], 0, TK, TQ)
        nxt_ref[...] = s
        return [(0, 2 * TQ, jnp.max(s, axis=1, keepdims=True))]

    def next_diag(nxt_ref, qslot, ci):
        r = pl.multiple_of(ci * TK, TK)
        q = qs_ref[qslot]
        ck = c_ref[0, 0, ci]
        left = qk(q, k_ref[0, pl.ds(r, H), :]) - key_bias(ck, 0, H, TQ)
        left = jnp.concatenate([causal(left[0:H]), left[H:TQ],
                                causal(left[TQ:TQ + H]), left[TQ + H:]], axis=0)
        q_hi = jnp.concatenate([q[H:TQ], q[TQ + H:]], axis=0)
        right = qk(q_hi, k_ref[0, pl.ds(pl.multiple_of(r + H, H), H), :])
        right = right - key_bias(ck, H, H, H)
        right = jnp.concatenate([causal(right[:H]), causal(right[H:])], axis=0)
        nxt_ref[:, 0:H] = left
        nxt_ref[H:TQ, H:TK] = right[:H]
        nxt_ref[TQ + H:, H:TK] = right[H:]
        rowmax = lambda a: jnp.max(a, axis=1, keepdims=True)
        return [(0, H, rowmax(left[0:H])),
                (H, TQ, jnp.maximum(rowmax(left[H:TQ]), rowmax(right[:H]))),
                (TQ, TQ + H, rowmax(left[TQ:TQ + H])),
                (TQ + H, 2 * TQ, jnp.maximum(rowmax(left[TQ + H:]), rowmax(right[H:])))]

    def pv_full(cur_ref, ci, m_cur):
        p = jnp.exp2(cur_ref[...] - lanes(m_cur, TK)).astype(BF16)
        r = pl.multiple_of(ci * TK, TK)
        return jnp.concatenate([_dot(p[:TQ], va_ref[pl.ds(r, TK), :]),
                                _dot(p[TQ:], vb_ref[pl.ds(r, TK), :])], axis=0)

    def pv_diag(cur_ref, ci, m_cur):
        r = pl.multiple_of(ci * TK, TK)
        r_hi = pl.multiple_of(r + H, H)
        p_left = jnp.exp2(cur_ref[:, 0:H] - lanes(m_cur, H)).astype(BF16)
        out = []
        for head, vh_ref in enumerate((va_ref, vb_ref)):
            lo, hi = head * TQ, head * TQ + H
            pv = _dot(p_left[lo:lo + TQ], vh_ref[pl.ds(r, H), :])
            p_right = jnp.exp2(cur_ref[hi:hi + H, H:TK] - lanes(m_cur[hi:hi + H], H))
            pv_hi = _dot(p_right.astype(BF16), vh_ref[pl.ds(r_hi, H), :])
            out += [pv[:H], pv[H:] + pv_hi]
        return jnp.concatenate(out, axis=0)

    def step_on(cur_ref, nxt_ref, ci, qslot_n, ci_n, cur_diag, nxt_diag):
        if nxt_diag is None:
            row_maxes = []
        else:
            row_maxes = (next_diag if nxt_diag else next_full)(nxt_ref, qslot_n, ci_n)
        m_cur = m_ref[...]
        alpha = alpha_ref[...]
        for lo, hi, mx in row_maxes:
            if cur_diag:
                m_ref[lo:hi] = jnp.broadcast_to(mx, (hi - lo, LANES))
                alpha_ref[lo:hi] = jnp.zeros((hi - lo, LANES), F32)
            else:
                m_next = jnp.maximum(m_cur[lo:hi], mx)
                m_ref[lo:hi] = m_next
                alpha_ref[lo:hi] = jnp.exp2(m_cur[lo:hi] - m_next)
        pv = (pv_diag if cur_diag else pv_full)(cur_ref, ci, m_cur)
        acc = alpha * acc_ref[...] + pv
        if not cur_diag:
            acc_ref[...] = acc
            return
        la = jnp.broadcast_to(acc[:TQ, FOX_HEAD_DIM:FOX_HEAD_DIM + 1], (TQ, LANES))
        lb = jnp.broadcast_to(acc[TQ:, 0:1], (TQ, LANES))
        o = jnp.where(first, acc[:TQ] / la, acc[TQ:] / lb)
        o_ref[0, pl.ds(pl.multiple_of(ci * TQ, TQ), TQ), :] = o.astype(BF16)
        acc_ref[...] = jnp.zeros_like(acc_ref)
        prep_q(jnp.minimum(ci + 2, nq - 1), ci & 1)

    def step(cnt, *args):
        lax.cond((cnt & 1) == 0,
                 lambda: step_on(s0_ref, s1_ref, *args),
                 lambda: step_on(s1_ref, s0_ref, *args))
        return cnt + 1

    def q_block(qb, cnt):
        qslot = qb & 1
        last = qb

        def inner(ci, cnt):
            return step(cnt, ci, qslot, ci + 1, False, False)

        cnt = lax.fori_loop(0, last - 1, inner, cnt)

        def before_diag(ci, cnt):
            return step(cnt, ci, qslot, ci + 1, False, True)

        cnt = lax.fori_loop(jnp.maximum(last - 1, 0), last, before_diag, cnt)
        cnt = lax.cond(qb == nq - 1,
                       lambda c: step(c, last, None, None, True, None),
                       lambda c: step(c, last, 1 - qslot, 0, True, False), cnt)
        return cnt

    prep_q(0, 0)
    prep_q(1, 1)
    acc_ref[...] = jnp.zeros_like(acc_ref)
    for lo, hi, mx in next_diag(s0_ref, 0, 0):
        m_ref[lo:hi] = jnp.broadcast_to(mx, (hi - lo, LANES))
    alpha_ref[...] = jnp.zeros_like(alpha_ref)
    lax.fori_loop(0, nq, q_block, 0)


def _fox_attn(qkv, c):
    B, S, _ = qkv.shape
    pairs = FOX_HEADS // 2
    blk = lambda off: pl.BlockSpec((1, S, LANES), lambda b, p: (b, 0, off + p))
    return pl.pallas_call(
        _fox_attn_kernel,
        grid=(B, pairs),
        in_specs=[
            blk(0), blk(pairs), blk(2 * pairs),
            pl.BlockSpec((1, 1, S // TK, 2, TK), lambda b, p: (b, p, 0, 0, 0)),
        ],
        out_specs=pl.BlockSpec((1, S, LANES), lambda b, p: (b, 0, p)),
        out_shape=jax.ShapeDtypeStruct((B, S, D_MODEL), BF16),
        scratch_shapes=[
            pltpu.VMEM((2, 2 * TQ, LANES), BF16),
            pltpu.VMEM((S, LANES), BF16),
            pltpu.VMEM((S, LANES), BF16),
            pltpu.VMEM((2 * TQ, TK), F32),
            pltpu.VMEM((2 * TQ, TK), F32),
            pltpu.VMEM((2 * TQ, LANES), F32),
            pltpu.VMEM((2 * TQ, LANES), F32),
            pltpu.VMEM((2 * TQ, LANES), F32),
        ],
        compiler_params=pltpu.CompilerParams(
            dimension_semantics=("parallel", "parallel"),
            vmem_limit_bytes=VMEM_LIMIT_BYTES),
        name="fox_attn",
    )(qkv, qkv, qkv, c)


def _ffn_kernel(*refs, has_mix, final_norm, seg_in, seg_out):
    refs = list(refs)
    x_ref = refs.pop(0)
    if has_mix:
        a_ref = refs.pop(0)
        wmix_ref = refs.pop(0)
    g_ref, wgu_ref, wd_ref = refs[:3]
    refs = refs[3:]
    if final_norm:
        gf_ref = refs.pop(0)
    if seg_in or seg_out:
        o_ref, act_ref, slab_ref = refs
    else:
        o_ref, act_ref = refs
    n_slabs = D_MODEL // LANES
    seg_rows = [(ch * S5_CHUNK + sg, ch * S5_CHUNK + sg * S5_TJ)
                for ch in range(TM // S5_CHUNK) for sg in range(S5_SEGS)]

    if seg_in:
        cols = []
        for c in range(n_slabs):
            slab_ref[c] = x_ref[:, c * LANES:(c + 1) * LANES]
            cols.append(jnp.concatenate(
                [slab_ref[c, pl.ds(r_seg, S5_TJ, stride=S5_SEGS), :] for r_seg, _ in seg_rows],
                axis=0))
        x = jnp.concatenate(cols, axis=1)
    else:
        x = x_ref[...]
    if has_mix:
        x = x + _dot(a_ref[...], wmix_ref[...])
    h = _rms(x, g_ref[...]).astype(BF16)
    for c in range(D_FF // FF_CHUNK):
        lo = c * FF_CHUNK
        gate = _dot(h, wgu_ref[:, lo:lo + FF_CHUNK])
        up = _dot(h, wgu_ref[:, D_FF + lo:D_FF + lo + FF_CHUNK])
        act_ref[:, lo:lo + FF_CHUNK] = (gate * jax.nn.sigmoid(gate) * up).astype(BF16)
    y = x + _dot(act_ref[...], wd_ref[...])
    if final_norm:
        y = _rms(y, gf_ref[...])
    if seg_out:
        for c in range(n_slabs):
            for r_seg, r_time in seg_rows:
                slab_ref[c, pl.ds(r_seg, S5_TJ, stride=S5_SEGS), :] = (
                    y[r_time:r_time + S5_TJ, c * LANES:(c + 1) * LANES])
            o_ref[:, c * LANES:(c + 1) * LANES] = slab_ref[c]
    else:
        o_ref[...] = y


def _ffn(x, g, wgu, wd, i, mix=None, final_g=None, seg_in=False, seg_out=False):
    N, D = x.shape
    row = pl.BlockSpec((TM, D), lambda t: (t, 0))
    scratch = [pltpu.VMEM((TM, D_FF), BF16)]
    if seg_in or seg_out:
        scratch.append(pltpu.VMEM((D // LANES, TM, LANES), F32))
    args, specs = [x], [row]
    if mix is not None:
        a2, wmix, j = mix
        args += [a2, wmix]
        specs += [row, _layer(wmix, j)]
    args += [g, wgu, wd]
    specs += [_layer(g, i), _layer(wgu, i), _layer(wd, i)]
    if final_g is not None:
        args.append(final_g)
        specs.append(_resident(final_g.shape))
    return pl.pallas_call(
        functools.partial(_ffn_kernel, has_mix=mix is not None,
                          final_norm=final_g is not None, seg_in=seg_in, seg_out=seg_out),
        grid=(N // TM,),
        in_specs=specs,
        out_specs=row,
        out_shape=jax.ShapeDtypeStruct((N, D), F32),
        scratch_shapes=scratch,
        compiler_params=pltpu.CompilerParams(
            dimension_semantics=("parallel",),
            vmem_limit_bytes=VMEM_LIMIT_BYTES),
        name="ffn",
    )(*args)


def _s5_kernel(x_ref, g_ref, win_ref, bblk_ref, cblk_ref, lam_ref,
               d_ref, wglu_ref, o_ref, st_ref, carry_ref):
    @pl.when(pl.program_id(1) == 0)
    def _():
        carry_ref[...] = jnp.zeros_like(carry_ref)

    x = x_ref[0]
    h = _rms(x, g_ref[...]).astype(BF16)
    u = _dot(h, win_ref[...])
    ub = u.astype(BF16)

    sub = lax.broadcasted_iota(jnp.int32, (SUBLANES, LANES), 0)
    vshape = (SUBLANES, LANES)
    blocks_per_slab = S5_SLAB_HALF // LANES
    n_blocks = S5_SLABS * blocks_per_slab

    def cmul(ar, ai, br, bi):
        return ar * br - ai * bi, ar * bi + ai * br

    groups_per_slab = blocks_per_slab // S5_COLS_PER_STEP
    ys = []
    for grp in range(n_blocks // S5_COLS_PER_STEP):
        sl = grp // groups_per_slab
        if grp % groups_per_slab == 0:
            st_ref[:, sl * S5_SLAB_W:(sl + 1) * S5_SLAB_W] = _dot(
                ub[:, sl * S5_SLAB_IN:(sl + 1) * S5_SLAB_IN], bblk_ref[sl])
        offs = []
        for b in range(grp * S5_COLS_PER_STEP, (grp + 1) * S5_COLS_PER_STEP):
            re = (b // blocks_per_slab) * S5_SLAB_W + (b % blocks_per_slab) * LANES
            offs.append((re, re + S5_SLAB_HALF))
        lam = [(jnp.broadcast_to(lam_ref[:, re:re + LANES], vshape),
                jnp.broadcast_to(lam_ref[:, im:im + LANES], vshape)) for re, im in offs]

        def scan_body(j, state):
            r = j * SUBLANES
            new = []
            for (re, im), (lr, li), (sr, si) in zip(offs, lam, state):
                pr, pi = cmul(lr, li, sr, si)
                nr = pr + st_ref[pl.ds(r, SUBLANES), re:re + LANES]
                ni = pi + st_ref[pl.ds(r, SUBLANES), im:im + LANES]
                st_ref[pl.ds(r, SUBLANES), re:re + LANES] = nr
                st_ref[pl.ds(r, SUBLANES), im:im + LANES] = ni
                new.append((nr, ni))
            return tuple(new)

        init = tuple((carry_ref[:, re:re + LANES], carry_ref[:, im:im + LANES])
                     for re, im in offs)
        fin = init
        for j in range(S5_TJ):
            fin = scan_body(j, fin)

        carries = []
        for (re, im), (lr, li), (fr, fi) in zip(offs, lam, fin):
            for _ in range(S5_TJ.bit_length() - 1):
                lr, li = cmul(lr, li, lr, li)
            er, ei = fr, fi
            for k in range(1, S5_SEGS):
                pr, pi = cmul(lr, li, pltpu.roll(er, 1, 0), pltpu.roll(ei, 1, 0))
                er = jnp.where(sub == k, fr + pr, er)
                ei = jnp.where(sub == k, fi + pi, ei)
            sr, si = pltpu.roll(er, 1, 0), pltpu.roll(ei, 1, 0)
            carry_ref[:, re:re + LANES] = jnp.where(sub == 0, sr, 0.0)
            carry_ref[:, im:im + LANES] = jnp.where(sub == 0, si, 0.0)
            carries.append((jnp.where(sub == 0, 0.0, sr), jnp.where(sub == 0, 0.0, si)))

        def fix_body(j, f):
            r = j * SUBLANES
            new = []
            for (re, im), (lr, li), (fr, fi) in zip(offs, lam, f):
                fr, fi = cmul(lr, li, fr, fi)
                st_ref[pl.ds(r, SUBLANES), re:re + LANES] += fr
                st_ref[pl.ds(r, SUBLANES), im:im + LANES] += fi
                new.append((fr, fi))
            return tuple(new)

        f = tuple(carries)
        for j in range(S5_TJ):
            f = fix_body(j, f)

        if grp % groups_per_slab == groups_per_slab - 1:
            ys.append(_dot(st_ref[:, sl * S5_SLAB_W:(sl + 1) * S5_SLAB_W].astype(BF16),
                           cblk_ref[sl]))

    y = jnp.concatenate(ys, axis=1) + d_ref[...] * u
    gl = jax.nn.gelu(y).astype(BF16)
    vg = _dot(gl, wglu_ref[...])
    o_ref[0] = x + vg[:, :D_MODEL] * jax.nn.sigmoid(vg[:, D_MODEL:])


def _s5(xp, g, i, win, bblk, cblk, lam, d, wglu, j):
    B, S, D = xp.shape
    return pl.pallas_call(
        _s5_kernel,
        grid=(B, S // S5_CHUNK),
        in_specs=[
            pl.BlockSpec((1, S5_CHUNK, D), lambda b, c: (b, c, 0)),
            _layer(g, i), _layer(win, j), _resident(bblk.shape),
            _resident(cblk.shape), _resident(lam.shape),
            _layer(d, j), _layer(wglu, j),
        ],
        out_specs=pl.BlockSpec((1, S5_CHUNK, D), lambda b, c: (b, c, 0)),
        out_shape=jax.ShapeDtypeStruct((B, S, D), F32),
        scratch_shapes=[
            pltpu.VMEM((S5_CHUNK, S5_SW), F32),
            pltpu.VMEM((SUBLANES, S5_SW), F32),
        ],
        compiler_params=pltpu.CompilerParams(
            dimension_semantics=("parallel", "arbitrary"),
            vmem_limit_bytes=VMEM_LIMIT_BYTES),
        name="s5",
    )(xp, g, win, bblk, cblk, lam, d, wglu)


def _s5_params(a_re, a_im, log_dt, b_re, b_im, c_re, c_im):
    ar, ai = a_re.astype(F32), a_im.astype(F32)
    dt = jnp.exp(log_dt.astype(F32))[:, None]
    mag = jnp.exp(ar * dt)
    lr, li = mag * jnp.cos(ai * dt), mag * jnp.sin(ai * dt)
    den = ar * ar + ai * ai
    kr = ((lr - 1.0) * ar + li * ai) / den
    ki = (li * ar - (lr - 1.0) * ai) / den
    br, bi = b_re.astype(F32), b_im.astype(F32)
    bbr = kr[..., None] * br - ki[..., None] * bi
    bbi = kr[..., None] * bi + ki[..., None] * br

    gs = S5_GROUPS // S5_SLABS
    eye = jnp.eye(gs, dtype=F32)

    def state_row(zr, zi):
        zr = zr.reshape(S5_SLABS, S5_SLAB_HALF)
        zi = zi.reshape(S5_SLABS, S5_SLAB_HALF)
        return jnp.concatenate([zr, zi], axis=-1).reshape(S5_SW)

    same_group = (eye > 0)[None, :, None, :, None]

    def in_block(b):
        bt = b.reshape(S5_SLABS, gs, S5_STATE, S5_GROUP).transpose(0, 1, 3, 2)
        full = jnp.where(same_group, bt[:, :, :, None, :], 0.0)
        return full.reshape(S5_SLABS, S5_SLAB_IN, S5_SLAB_HALF)

    def out_block(c):
        ct = c.reshape(S5_SLABS, gs, S5_GROUP, S5_STATE).transpose(0, 1, 3, 2)
        full = jnp.where(same_group, ct[:, :, :, None, :], 0.0)
        return full.reshape(S5_SLABS, S5_SLAB_HALF, S5_SLAB_IN)

    bblk = jnp.concatenate([in_block(bbr), in_block(bbi)], axis=2)
    cblk = jnp.concatenate([out_block(c_re.astype(F32)), -out_block(c_im.astype(F32))], axis=1)
    return bblk.astype(BF16), cblk.astype(BF16), state_row(lr, li)[None]


POOL_HALO = 32


def _pool_kernel(x_ref, halo_ref, g_ref, w_ref, b_ref, sc_ref, o_ref, a_ref, b2_ref):
    t = pl.program_id(1)
    x = x_ref[0]
    g = g_ref[...]
    h = _rms(x, g)
    hh = _rms(halo_ref[0], g)
    a_ref[0:POOL_HALO] = jnp.where(t > 0, hh, 0.0)
    a_ref[POOL_HALO:] = h
    b2_ref[0:SUBLANES] = jnp.zeros((SUBLANES, D_MODEL), F32)
    rows = POOL_HALO + TM
    tpos = t * TM + lax.broadcasted_iota(jnp.int32, (TM, POOL_WIDTH), 0) + 1
    outs = []
    for gi, w in enumerate(POOL_WINDOWS):
        lo = gi * POOL_WIDTH
        cols = slice(lo, lo + POOL_WIDTH)
        src, dst, sh = a_ref, b2_ref, 1
        while sh < w:
            dst[SUBLANES:rows, cols] = src[SUBLANES:rows, cols] + src[SUBLANES - sh:rows - sh, cols]
            src, dst, sh = dst, src, 2 * sh
        mean = src[POOL_HALO:rows, cols] / jnp.minimum(tpos, w).astype(F32)
        outs.append(_dot((mean - h[:, cols]).astype(BF16), w_ref[gi]))
    y = (jnp.concatenate(outs, axis=1) + b_ref[...]) * sc_ref[...]
    o_ref[0] = x + y


def _pool(x, g, i, w, b, sc, j):
    B, S, D = x.shape
    per = TM // POOL_HALO
    return pl.pallas_call(
        _pool_kernel,
        grid=(B, S // TM),
        in_specs=[
            pl.BlockSpec((1, TM, D), lambda b, t: (b, t, 0)),
            pl.BlockSpec((1, POOL_HALO, D), lambda b, t: (b, jnp.maximum(t * per - 1, 0), 0)),
            _layer(g, i), _layer(w, j), _layer(b, j), _layer(sc, j),
        ],
        out_specs=pl.BlockSpec((1, TM, D), lambda b, t: (b, t, 0)),
        out_shape=jax.ShapeDtypeStruct((B, S, D), F32),
        scratch_shapes=[pltpu.VMEM((POOL_HALO + TM, D), F32),
                        pltpu.VMEM((POOL_HALO + TM, D), F32)],
        compiler_params=pltpu.CompilerParams(
            dimension_semantics=("parallel", "arbitrary"),
            vmem_limit_bytes=VMEM_LIMIT_BYTES),
        name="pool",
    )(x, x, g, w, b, sc)


def kernel(x, mix_norm_g, ffn_norm_g, final_norm_g, fox_w_in, fox_b_f, fox_w_out, s5_w_in, s5_a_re, s5_a_im, s5_log_dt, s5_b_re, s5_b_im, s5_c_re, s5_c_im, s5_d, s5_w_glu, pool_w, pool_b, pool_scale, ffn_w_gate_up, ffn_w_down):
    B, S, D = x.shape
    N = B * S
    tri = jnp.triu(jnp.ones((TM, TM), F32)).astype(BF16)
    row = lambda a: a.astype(F32).reshape(a.shape[0], 1, a.shape[1])
    gm_all, gf_all = row(mix_norm_g), row(ffn_norm_g)
    wgu_all, wd_all = ffn_w_gate_up.astype(BF16), ffn_w_down.astype(BF16)
    fox_win = fox_w_in.astype(BF16)
    fox_wft = lax.optimization_barrier(fox_w_in[:, :, 3 * D:]).transpose(0, 2, 1).astype(BF16)
    fox_bf = fox_b_f.astype(F32)[:, :, None]
    fox_wout = fox_w_out.astype(BF16)
    s5_win, s5_wglu, s5_dd = s5_w_in.astype(BF16), s5_w_glu.astype(BF16), row(s5_d)
    pool_ww, pool_bb, pool_sc = pool_w.astype(BF16), row(pool_b), row(pool_scale)

    for i in range(DEPTH):
        kind, j = i % 3, i // 3
        next_is_s5 = i + 1 < DEPTH and (i + 1) % 3 == 1
        final_g = final_norm_g[None].astype(F32) if i == DEPTH - 1 else None
        mix = None
        if kind == 0:
            qkv, c = _fox_proj(x, gm_all, i, fox_win, fox_wft, fox_bf, j, tri)
            c = c.reshape(B, FOX_HEADS // 2, 2, S // TK, TK).transpose(0, 1, 3, 2, 4)
            o = _fox_attn(qkv, c)
            mix = (o.reshape(N, D), fox_wout, j)
        elif kind == 1:
            bblk, cblk, lam = _s5_params(
                s5_a_re[j], s5_a_im[j], s5_log_dt[j], s5_b_re[j], s5_b_im[j], s5_c_re[j], s5_c_im[j])
            x = _s5(x, gm_all, i, s5_win, bblk, cblk, lam, s5_dd, s5_wglu, j)
        else:
            x = _pool(x, gm_all, i, pool_ww, pool_bb, pool_sc, j)
        x = _ffn(x.reshape(N, D), gf_all, wgu_all, wd_all, i, mix=mix, final_g=final_g,
                 seg_in=kind == 1, seg_out=next_is_s5).reshape(B, S, D)
    return x
```

```python
import functools
import math

import jax
import jax.numpy as jnp
from jax import lax
from jax.experimental import pallas as pl
from jax.experimental.pallas import tpu as pltpu

F32 = jnp.float32
BF16 = jnp.bfloat16

D_MODEL = 1024
DEPTH = 4
EPS = 1e-6
FOX_HEADS = 16
FOX_HEAD_DIM = 64
S5_GROUP = 16
S5_GROUPS = 64
S5_STATE = 64
POOL_WINDOWS = (2, 4, 8, 16)
POOL_WIDTH = 256
D_FF = 2816

LANES = 128
SUBLANES = 8
MXU_DIM = 256
VMEM_LIMIT_BYTES = 56 * 1024 * 1024

LOG2E = 1.4426950408889634
NEG_BIG = -1e30

TM = 512
FF_CHUNK = MXU_DIM
TQ = 1024
TK = 1024
S5_SEGS = SUBLANES
S5_TJ = 32
S5_CHUNK = S5_SEGS * S5_TJ
S5_SLABS = 4
S5_SLAB_IN = D_MODEL // S5_SLABS
S5_SLAB_HALF = (S5_GROUPS // S5_SLABS) * S5_STATE
S5_SLAB_W = 2 * S5_SLAB_HALF
S5_SW = S5_SLABS * S5_SLAB_W
S5_COLS_PER_STEP = 4


def _resident(shape):
    nd = len(shape)
    return pl.BlockSpec(shape, lambda *_: (0,) * nd, pipeline_mode=pl.Buffered(1))


def _layer(arr, i):
    nd = arr.ndim
    return pl.BlockSpec((None,) + arr.shape[1:], lambda *_: (i,) + (0,) * (nd - 1),
                        pipeline_mode=pl.Buffered(1))


def _rms(x, g):
    return x * lax.rsqrt(jnp.mean(x * x, axis=-1, keepdims=True) + EPS) * g


def _dot(a, b):
    return jnp.dot(a, b, preferred_element_type=F32)


def _fox_proj_kernel(x_ref, g_ref, wqkv_ref, wft_ref, bf_ref, tri_ref,
                     qkv_ref, c_ref, carry_ref):
    @pl.when(pl.program_id(1) == 0)
    def _():
        carry_ref[...] = jnp.zeros_like(carry_ref)

    h = _rms(x_ref[0], g_ref[...]).astype(BF16)
    qkv = _dot(h, wqkv_ref[:, :3 * D_MODEL])
    q = qkv[:, :D_MODEL] * (FOX_HEAD_DIM ** -0.5 * LOG2E)
    qkv_ref[0, :, :D_MODEL] = q.astype(BF16)
    qkv_ref[0, :, D_MODEL:] = qkv[:, D_MODEL:].astype(BF16)

    z = lax.dot_general(wft_ref[...], h, (((1,), (1,)), ((), ())),
                        preferred_element_type=F32) + bf_ref[...]
    logf = jnp.minimum(z, 0.0) - jnp.log(1.0 + jnp.exp(-jnp.abs(z)))
    hi = logf.astype(BF16)
    r1 = logf - hi.astype(F32)
    mid = r1.astype(BF16)
    lo = (r1 - mid.astype(F32)).astype(BF16)
    parts = _dot(jnp.concatenate([hi, mid, lo], axis=0), tri_ref[...])
    c = (parts[:FOX_HEADS] + parts[FOX_HEADS:2 * FOX_HEADS] + parts[2 * FOX_HEADS:]
         + carry_ref[:, 0:1])
    c_ref[0] = c * LOG2E
    carry_ref[...] = jnp.broadcast_to(c[:, TM - 1:TM], carry_ref.shape)


def _fox_proj(x, g, i, win, wft, bf, j, tri):
    B, S, D = x.shape
    return pl.pallas_call(
        _fox_proj_kernel,
        grid=(B, S // TM),
        in_specs=[
            pl.BlockSpec((1, TM, D), lambda b, t: (b, t, 0)),
            _layer(g, i), _layer(win, j), _layer(wft, j), _layer(bf, j),
            _resident((TM, TM)),
        ],
        out_specs=[
            pl.BlockSpec((1, TM, 3 * D), lambda b, t: (b, t, 0)),
            pl.BlockSpec((1, FOX_HEADS, TM), lambda b, t: (b, 0, t)),
        ],
        out_shape=[
            jax.ShapeDtypeStruct((B, S, 3 * D), BF16),
            jax.ShapeDtypeStruct((B, FOX_HEADS, S), F32),
        ],
        scratch_shapes=[pltpu.VMEM((FOX_HEADS, LANES), F32)],
        compiler_params=pltpu.CompilerParams(
            dimension_semantics=("parallel", "arbitrary"),
            vmem_limit_bytes=VMEM_LIMIT_BYTES),
        name="fox_proj",
    )(x, g, win, wft, bf, tri)


def _fox_attn_kernel(*refs, n_cast):
    q_ref, k_ref, v_ref, c_ref = refs[:4]
    cast_in = refs[4:4 + n_cast]
    o_ref = refs[4 + n_cast]
    cast_out = refs[5 + n_cast:5 + 2 * n_cast]
    qs_ref, va_ref, vb_ref, s0_ref, s1_ref, m_ref, alpha_ref, acc_ref = refs[5 + 2 * n_cast:]
    for src_ref, dst_ref in zip(cast_in, cast_out):
        dst_ref[...] = src_ref[...].astype(BF16)
    assert TQ == TK
    S = q_ref.shape[1]
    nq = S // TQ
    lane = lax.broadcasted_iota(jnp.int32, (TQ, LANES), 1)
    first = lane < FOX_HEAD_DIM

    def prep_v(i, c):
        r = pl.multiple_of(i * TK, TK)
        v = v_ref[0, pl.ds(r, TK), :].astype(F32)
        ln = lax.broadcasted_iota(jnp.int32, v.shape, 1)
        va = jnp.where(ln < FOX_HEAD_DIM, v, jnp.where(ln == FOX_HEAD_DIM, 1.0, 0.0))
        vb = jnp.where(ln >= FOX_HEAD_DIM, v, jnp.where(ln == 0, 1.0, 0.0))
        va_ref[pl.ds(r, TK), :] = va.astype(BF16)
        vb_ref[pl.ds(r, TK), :] = vb.astype(BF16)
        return c

    lax.fori_loop(0, S // TK, prep_v, 0)

    def prep_q(qb, slot):
        q = q_ref[0, pl.ds(pl.multiple_of(qb * TQ, TQ), TQ), :]
        zero = jnp.zeros_like(q)
        qs_ref[slot, 0:TQ] = jnp.where(first, q, zero)
        qs_ref[slot, TQ:2 * TQ] = jnp.where(first, zero, q)

    H = TQ // 2

    def qk(q, k):
        return lax.dot_general(q, k, (((1,), (1,)), ((), ())), preferred_element_type=F32)

    def key_bias(ck, lo, width, rows):
        return jnp.concatenate(
            [jnp.broadcast_to(ck[0:1, lo:lo + width], (rows, width)),
             jnp.broadcast_to(ck[1:2, lo:lo + width], (rows, width))], axis=0)

    def causal(s):
        row = lax.broadcasted_iota(jnp.int32, s.shape, 0)
        col = lax.broadcasted_iota(jnp.int32, s.shape, 1)
        return jnp.where(col <= row, s, NEG_BIG)

    def lanes(m, width):
        return jnp.concatenate([m] * (width // LANES), axis=1)

    def next_full(nxt_ref, qslot, ci):
        k = k_ref[0, pl.ds(pl.multiple_of(ci * TK, TK), TK), :]
        s = qk(qs_ref[qslot], k) - key_bias(c_ref[0, 0, ci], 0, TK, TQ)
        nxt_ref[...] = s
        return [(0, 2 * TQ, jnp.max(s, axis=1, keepdims=True))]

    def next_diag(nxt_ref, qslot, ci):
        r = pl.multiple_of(ci * TK, TK)
        q = qs_ref[qslot]
        ck = c_ref[0, 0, ci]
        left = qk(q, k_ref[0, pl.ds(r, H), :]) - key_bias(ck, 0, H, TQ)
        left = jnp.concatenate([causal(left[0:H]), left[H:TQ],
                                causal(left[TQ:TQ + H]), left[TQ + H:]], axis=0)
        q_hi = jnp.concatenate([q[H:TQ], q[TQ + H:]], axis=0)
        right = qk(q_hi, k_ref[0, pl.ds(pl.multiple_of(r + H, H), H), :])
        right = right - key_bias(ck, H, H, H)
        right = jnp.concatenate([causal(right[:H]), causal(right[H:])], axis=0)
        nxt_ref[:, 0:H] = left
        nxt_ref[H:TQ, H:TK] = right[:H]
        nxt_ref[TQ + H:, H:TK] = right[H:]
        rowmax = lambda a: jnp.max(a, axis=1, keepdims=True)
        return [(0, H, rowmax(left[0:H])),
                (H, TQ, jnp.maximum(rowmax(left[H:TQ]), rowmax(right[:H]))),
                (TQ, TQ + H, rowmax(left[TQ:TQ + H])),
                (TQ + H, 2 * TQ, jnp.maximum(rowmax(left[TQ + H:]), rowmax(right[H:])))]

    def pv_full(cur_ref, ci, m_cur):
        p = jnp.exp2(cur_ref[...] - lanes(m_cur, TK)).astype(BF16)
        r = pl.multiple_of(ci * TK, TK)
        return jnp.concatenate([_dot(p[:TQ], va_ref[pl.ds(r, TK), :]),
                                _dot(p[TQ:], vb_ref[pl.ds(r, TK), :])], axis=0)

    def pv_diag(cur_ref, ci, m_cur):
        r = pl.multiple_of(ci * TK, TK)
        r_hi = pl.multiple_of(r + H, H)
        p_left = jnp.exp2(cur_ref[:, 0:H] - lanes(m_cur, H)).astype(BF16)
        out = []
        for head, vh_ref in enumerate((va_ref, vb_ref)):
            lo, hi = head * TQ, head * TQ + H
            pv = _dot(p_left[lo:lo + TQ], vh_ref[pl.ds(r, H), :])
            p_right = jnp.exp2(cur_ref[hi:hi + H, H:TK] - lanes(m_cur[hi:hi + H], H))
            pv_hi = _dot(p_right.astype(BF16), vh_ref[pl.ds(r_hi, H), :])
            out += [pv[:H], pv[H:] + pv_hi]
        return jnp.concatenate(out, axis=0)

    def step_on(cur_ref, nxt_ref, ci, qslot_n, ci_n, cur_diag, nxt_diag):
        if nxt_diag is None:
            row_maxes = []
        else:
            row_maxes = (next_diag if nxt_diag else next_full)(nxt_ref, qslot_n, ci_n)
        m_cur = m_ref[...]
        alpha = alpha_ref[...]
        for lo, hi, mx in row_maxes:
            if cur_diag:
                m_ref[lo:hi] = jnp.broadcast_to(mx, (hi - lo, LANES))
                alpha_ref[lo:hi] = jnp.zeros((hi - lo, LANES), F32)
            else:
                m_next = jnp.maximum(m_cur[lo:hi], mx)
                m_ref[lo:hi] = m_next
                alpha_ref[lo:hi] = jnp.exp2(m_cur[lo:hi] - m_next)
        pv = (pv_diag if cur_diag else pv_full)(cur_ref, ci, m_cur)
        acc = alpha * acc_ref[...] + pv
        if not cur_diag:
            acc_ref[...] = acc
            return
        la = jnp.broadcast_to(acc[:TQ, FOX_HEAD_DIM:FOX_HEAD_DIM + 1], (TQ, LANES))
        lb = jnp.broadcast_to(acc[TQ:, 0:1], (TQ, LANES))
        o = jnp.where(first, acc[:TQ] / la, acc[TQ:] / lb)
        o_ref[0, pl.ds(pl.multiple_of(ci * TQ, TQ), TQ), :] = o.astype(BF16)
        acc_ref[...] = jnp.zeros_like(acc_ref)
        prep_q(jnp.minimum(ci + 2, nq - 1), ci & 1)

    def step(cnt, *args):
        lax.cond((cnt & 1) == 0,
                 lambda: step_on(s0_ref, s1_ref, *args),
                 lambda: step_on(s1_ref, s0_ref, *args))
        return cnt + 1

    def q_block(qb, cnt):
        qslot = qb & 1
        last = qb

        def inner(ci, cnt):
            return step(cnt, ci, qslot, ci + 1, False, False)

        cnt = lax.fori_loop(0, last - 1, inner, cnt)

        def before_diag(ci, cnt):
            return step(cnt, ci, qslot, ci + 1, False, True)

        cnt = lax.fori_loop(jnp.maximum(last - 1, 0), last, before_diag, cnt)
        cnt = lax.cond(qb == nq - 1,
                       lambda c: step(c, last, None, None, True, None),
                       lambda c: step(c, last, 1 - qslot, 0, True, False), cnt)
        return cnt

    prep_q(0, 0)
    prep_q(1, 1)
    acc_ref[...] = jnp.zeros_like(acc_ref)
    for lo, hi, mx in next_diag(s0_ref, 0, 0):
        m_ref[lo:hi] = jnp.broadcast_to(mx, (hi - lo, LANES))
    alpha_ref[...] = jnp.zeros_like(alpha_ref)
    lax.fori_loop(0, nq, q_block, 0)


def _fox_attn(qkv, c, cast=()):
    B, S, _ = qkv.shape
    pairs = FOX_HEADS // 2
    steps = B * pairs
    blk = lambda off: pl.BlockSpec((1, S, LANES), lambda b, p: (b, 0, off + p))
    cast_specs = [pl.BlockSpec((w.shape[0] // steps, w.shape[1]), lambda b, p: (b * pairs + p, 0))
                  for w in cast]
    return pl.pallas_call(
        functools.partial(_fox_attn_kernel, n_cast=len(cast)),
        grid=(B, pairs),
        in_specs=[
            blk(0), blk(pairs), blk(2 * pairs),
            pl.BlockSpec((1, 1, S // TK, 2, TK), lambda b, p: (b, p, 0, 0, 0)),
        ] + cast_specs,
        out_specs=[pl.BlockSpec((1, S, LANES), lambda b, p: (b, 0, p))] + cast_specs,
        out_shape=[jax.ShapeDtypeStruct((B, S, D_MODEL), BF16)]
                  + [jax.ShapeDtypeStruct(w.shape, BF16) for w in cast],
        scratch_shapes=[
            pltpu.VMEM((2, 2 * TQ, LANES), BF16),
            pltpu.VMEM((S, LANES), BF16),
            pltpu.VMEM((S, LANES), BF16),
            pltpu.VMEM((2 * TQ, TK), F32),
            pltpu.VMEM((2 * TQ, TK), F32),
            pltpu.VMEM((2 * TQ, LANES), F32),
            pltpu.VMEM((2 * TQ, LANES), F32),
            pltpu.VMEM((2 * TQ, LANES), F32),
        ],
        compiler_params=pltpu.CompilerParams(
            dimension_semantics=("parallel", "parallel"),
            vmem_limit_bytes=VMEM_LIMIT_BYTES),
        name="fox_attn",
    )(qkv, qkv, qkv, c, *cast)


def _ffn_kernel(*refs, has_mix, final_norm, seg_in, seg_out):
    refs = list(refs)
    x_ref = refs.pop(0)
    if has_mix:
        a_ref = refs.pop(0)
        wmix_ref = refs.pop(0)
    g_ref, wgu_ref, wd_ref = refs[:3]
    refs = refs[3:]
    if final_norm:
        gf_ref = refs.pop(0)
    if seg_in or seg_out:
        o_ref, act_ref, slab_ref = refs
    else:
        o_ref, act_ref = refs
    n_slabs = D_MODEL // LANES
    seg_rows = [(ch * S5_CHUNK + sg, ch * S5_CHUNK + sg * S5_TJ)
                for ch in range(TM // S5_CHUNK) for sg in range(S5_SEGS)]

    if seg_in:
        cols = []
        for c in range(n_slabs):
            slab_ref[c] = x_ref[:, c * LANES:(c + 1) * LANES]
            cols.append(jnp.concatenate(
                [slab_ref[c, pl.ds(r_seg, S5_TJ, stride=S5_SEGS), :] for r_seg, _ in seg_rows],
                axis=0))
        x = jnp.concatenate(cols, axis=1)
    else:
        x = x_ref[...]
    if has_mix:
        x = x + _dot(a_ref[...], wmix_ref[...])
    h = _rms(x, g_ref[...]).astype(BF16)
    for c in range(D_FF // FF_CHUNK):
        lo = c * FF_CHUNK
        gate = _dot(h, wgu_ref[:, lo:lo + FF_CHUNK])
        up = _dot(h, wgu_ref[:, D_FF + lo:D_FF + lo + FF_CHUNK])
        act_ref[:, lo:lo + FF_CHUNK] = (gate * jax.nn.sigmoid(gate) * up).astype(BF16)
    y = x + _dot(act_ref[...], wd_ref[...])
    if final_norm:
        y = _rms(y, gf_ref[...])
    if seg_out:
        for c in range(n_slabs):
            for r_seg, r_time in seg_rows:
                slab_ref[c, pl.ds(r_seg, S5_TJ, stride=S5_SEGS), :] = (
                    y[r_time:r_time + S5_TJ, c * LANES:(c + 1) * LANES])
            o_ref[:, c * LANES:(c + 1) * LANES] = slab_ref[c]
    else:
        o_ref[...] = y


def _ffn(x, g, wgu, wd, i, mix=None, final_g=None, seg_in=False, seg_out=False):
    N, D = x.shape
    row = pl.BlockSpec((TM, D), lambda t: (t, 0))
    scratch = [pltpu.VMEM((TM, D_FF), BF16)]
    if seg_in or seg_out:
        scratch.append(pltpu.VMEM((D // LANES, TM, LANES), F32))
    args, specs = [x], [row]
    if mix is not None:
        a2, wmix, j = mix
        args += [a2, wmix]
        specs += [row, _layer(wmix, j)]
    args += [g, wgu, wd]
    specs += [_layer(g, i), _layer(wgu, i), _layer(wd, i)]
    if final_g is not None:
        args.append(final_g)
        specs.append(_resident(final_g.shape))
    return pl.pallas_call(
        functools.partial(_ffn_kernel, has_mix=mix is not None,
                          final_norm=final_g is not None, seg_in=seg_in, seg_out=seg_out),
        grid=(N // TM,),
        in_specs=specs,
        out_specs=row,
        out_shape=jax.ShapeDtypeStruct((N, D), F32),
        scratch_shapes=scratch,
        compiler_params=pltpu.CompilerParams(
            dimension_semantics=("parallel",),
            vmem_limit_bytes=VMEM_LIMIT_BYTES),
        name="ffn",
    )(*args)


def _s5_kernel(x_ref, g_ref, win_ref, bblk_ref, cblk_ref, lam_ref,
               d_ref, wglu_ref, o_ref, st_ref, carry_ref):
    @pl.when(pl.program_id(1) == 0)
    def _():
        carry_ref[...] = jnp.zeros_like(carry_ref)

    x = x_ref[0]
    h = _rms(x, g_ref[...]).astype(BF16)
    u = _dot(h, win_ref[...])
    ub = u.astype(BF16)

    sub = lax.broadcasted_iota(jnp.int32, (SUBLANES, LANES), 0)
    vshape = (SUBLANES, LANES)
    blocks_per_slab = S5_SLAB_HALF // LANES
    n_blocks = S5_SLABS * blocks_per_slab

    def cmul(ar, ai, br, bi):
        return ar * br - ai * bi, ar * bi + ai * br

    groups_per_slab = blocks_per_slab // S5_COLS_PER_STEP
    ys = []
    for grp in range(n_blocks // S5_COLS_PER_STEP):
        sl = grp // groups_per_slab
        if grp % groups_per_slab == 0:
            st_ref[:, sl * S5_SLAB_W:(sl + 1) * S5_SLAB_W] = _dot(
                ub[:, sl * S5_SLAB_IN:(sl + 1) * S5_SLAB_IN], bblk_ref[sl])
        offs = []
        for b in range(grp * S5_COLS_PER_STEP, (grp + 1) * S5_COLS_PER_STEP):
            re = (b // blocks_per_slab) * S5_SLAB_W + (b % blocks_per_slab) * LANES
            offs.append((re, re + S5_SLAB_HALF))
        lam = [(jnp.broadcast_to(lam_ref[:, re:re + LANES], vshape),
                jnp.broadcast_to(lam_ref[:, im:im + LANES], vshape)) for re, im in offs]

        def scan_body(j, state):
            r = j * SUBLANES
            new = []
            for (re, im), (lr, li), (sr, si) in zip(offs, lam, state):
                pr, pi = cmul(lr, li, sr, si)
                nr = pr + st_ref[pl.ds(r, SUBLANES), re:re + LANES]
                ni = pi + st_ref[pl.ds(r, SUBLANES), im:im + LANES]
                st_ref[pl.ds(r, SUBLANES), re:re + LANES] = nr
                st_ref[pl.ds(r, SUBLANES), im:im + LANES] = ni
                new.append((nr, ni))
            return tuple(new)

        init = tuple((carry_ref[:, re:re + LANES], carry_ref[:, im:im + LANES])
                     for re, im in offs)
        fin = init
        for j in range(S5_TJ):
            fin = scan_body(j, fin)

        carries = []
        for (re, im), (lr, li), (fr, fi) in zip(offs, lam, fin):
            for _ in range(S5_TJ.bit_length() - 1):
                lr, li = cmul(lr, li, lr, li)
            er, ei = fr, fi
            for k in range(1, S5_SEGS):
                pr, pi = cmul(lr, li, pltpu.roll(er, 1, 0), pltpu.roll(ei, 1, 0))
                er = jnp.where(sub == k, fr + pr, er)
                ei = jnp.where(sub == k, fi + pi, ei)
            sr, si = pltpu.roll(er, 1, 0), pltpu.roll(ei, 1, 0)
            carry_ref[:, re:re + LANES] = jnp.where(sub == 0, sr, 0.0)
            carry_ref[:, im:im + LANES] = jnp.where(sub == 0, si, 0.0)
            carries.append((jnp.where(sub == 0, 0.0, sr), jnp.where(sub == 0, 0.0, si)))

        def fix_body(j, f):
            r = j * SUBLANES
            new = []
            for (re, im), (lr, li), (fr, fi) in zip(offs, lam, f):
                fr, fi = cmul(lr, li, fr, fi)
                st_ref[pl.ds(r, SUBLANES), re:re + LANES] += fr
                st_ref[pl.ds(r, SUBLANES), im:im + LANES] += fi
                new.append((fr, fi))
            return tuple(new)

        f = tuple(carries)
        for j in range(S5_TJ):
            f = fix_body(j, f)

        if grp % groups_per_slab == groups_per_slab - 1:
            ys.append(_dot(st_ref[:, sl * S5_SLAB_W:(sl + 1) * S5_SLAB_W].astype(BF16),
                           cblk_ref[sl]))

    y = jnp.concatenate(ys, axis=1) + d_ref[...] * u
    gl = jax.nn.gelu(y).astype(BF16)
    vg = _dot(gl, wglu_ref[...])
    o_ref[0] = x + vg[:, :D_MODEL] * jax.nn.sigmoid(vg[:, D_MODEL:])


def _s5(xp, g, i, win, bblk, cblk, lam, d, wglu, j):
    B, S, D = xp.shape
    return pl.pallas_call(
        _s5_kernel,
        grid=(B, S // S5_CHUNK),
        in_specs=[
            pl.BlockSpec((1, S5_CHUNK, D), lambda b, c: (b, c, 0)),
            _layer(g, i), _layer(win, j), _resident(bblk.shape),
            _resident(cblk.shape), _resident(lam.shape),
            _layer(d, j), _layer(wglu, j),
        ],
        out_specs=pl.BlockSpec((1, S5_CHUNK, D), lambda b, c: (b, c, 0)),
        out_shape=jax.ShapeDtypeStruct((B, S, D), F32),
        scratch_shapes=[
            pltpu.VMEM((S5_CHUNK, S5_SW), F32),
            pltpu.VMEM((SUBLANES, S5_SW), F32),
        ],
        compiler_params=pltpu.CompilerParams(
            dimension_semantics=("parallel", "arbitrary"),
            vmem_limit_bytes=VMEM_LIMIT_BYTES),
        name="s5",
    )(xp, g, win, bblk, cblk, lam, d, wglu)


def _s5_params(a_re, a_im, log_dt, b_re, b_im, c_re, c_im):
    ar, ai = a_re.astype(F32), a_im.astype(F32)
    dt = jnp.exp(log_dt.astype(F32))[:, None]
    mag = jnp.exp(ar * dt)
    lr, li = mag * jnp.cos(ai * dt), mag * jnp.sin(ai * dt)
    den = ar * ar + ai * ai
    kr = ((lr - 1.0) * ar + li * ai) / den
    ki = (li * ar - (lr - 1.0) * ai) / den
    br, bi = b_re.astype(F32), b_im.astype(F32)
    bbr = kr[..., None] * br - ki[..., None] * bi
    bbi = kr[..., None] * bi + ki[..., None] * br

    gs = S5_GROUPS // S5_SLABS
    eye = jnp.eye(gs, dtype=F32)

    def state_row(zr, zi):
        zr = zr.reshape(S5_SLABS, S5_SLAB_HALF)
        zi = zi.reshape(S5_SLABS, S5_SLAB_HALF)
        return jnp.concatenate([zr, zi], axis=-1).reshape(S5_SW)

    same_group = (eye > 0)[None, :, None, :, None]

    def in_block(b):
        bt = b.reshape(S5_SLABS, gs, S5_STATE, S5_GROUP).transpose(0, 1, 3, 2)
        full = jnp.where(same_group, bt[:, :, :, None, :], 0.0)
        return full.reshape(S5_SLABS, S5_SLAB_IN, S5_SLAB_HALF)

    def out_block(c):
        ct = c.reshape(S5_SLABS, gs, S5_GROUP, S5_STATE).transpose(0, 1, 3, 2)
        full = jnp.where(same_group, ct[:, :, :, None, :], 0.0)
        return full.reshape(S5_SLABS, S5_SLAB_HALF, S5_SLAB_IN)

    bblk = jnp.concatenate([in_block(bbr), in_block(bbi)], axis=2)
    cblk = jnp.concatenate([out_block(c_re.astype(F32)), -out_block(c_im.astype(F32))], axis=1)
    return bblk.astype(BF16), cblk.astype(BF16), state_row(lr, li)[None]


POOL_HALO = 32


def _pool_kernel(x_ref, halo_ref, g_ref, w_ref, b_ref, sc_ref, o_ref, a_ref, b2_ref):
    t = pl.program_id(1)
    x = x_ref[0]
    g = g_ref[...]
    h = _rms(x, g)
    hh = _rms(halo_ref[0], g)
    a_ref[0:POOL_HALO] = jnp.where(t > 0, hh, 0.0)
    a_ref[POOL_HALO:] = h
    b2_ref[0:SUBLANES] = jnp.zeros((SUBLANES, D_MODEL), F32)
    rows = POOL_HALO + TM
    tpos = t * TM + lax.broadcasted_iota(jnp.int32, (TM, POOL_WIDTH), 0) + 1
    outs = []
    for gi, w in enumerate(POOL_WINDOWS):
        lo = gi * POOL_WIDTH
        cols = slice(lo, lo + POOL_WIDTH)
        src, dst, sh = a_ref, b2_ref, 1
        while sh < w:
            dst[SUBLANES:rows, cols] = src[SUBLANES:rows, cols] + src[SUBLANES - sh:rows - sh, cols]
            src, dst, sh = dst, src, 2 * sh
        mean = src[POOL_HALO:rows, cols] / jnp.minimum(tpos, w).astype(F32)
        outs.append(_dot((mean - h[:, cols]).astype(BF16), w_ref[gi]))
    y = (jnp.concatenate(outs, axis=1) + b_ref[...]) * sc_ref[...]
    o_ref[0] = x + y


def _pool(x, g, i, w, b, sc, j):
    B, S, D = x.shape
    per = TM // POOL_HALO
    return pl.pallas_call(
        _pool_kernel,
        grid=(B, S // TM),
        in_specs=[
            pl.BlockSpec((1, TM, D), lambda b, t: (b, t, 0)),
            pl.BlockSpec((1, POOL_HALO, D), lambda b, t: (b, jnp.maximum(t * per - 1, 0), 0)),
            _layer(g, i), _layer(w, j), _layer(b, j), _layer(sc, j),
        ],
        out_specs=pl.BlockSpec((1, TM, D), lambda b, t: (b, t, 0)),
        out_shape=jax.ShapeDtypeStruct((B, S, D), F32),
        scratch_shapes=[pltpu.VMEM((POOL_HALO + TM, D), F32),
                        pltpu.VMEM((POOL_HALO + TM, D), F32)],
        compiler_params=pltpu.CompilerParams(
            dimension_semantics=("parallel", "arbitrary"),
            vmem_limit_bytes=VMEM_LIMIT_BYTES),
        name="pool",
    )(x, x, g, w, b, sc)


def kernel(x, mix_norm_g, ffn_norm_g, final_norm_g, fox_w_in, fox_b_f, fox_w_out, s5_w_in, s5_a_re, s5_a_im, s5_log_dt, s5_b_re, s5_b_im, s5_c_re, s5_c_im, s5_d, s5_w_glu, pool_w, pool_b, pool_scale, ffn_w_gate_up, ffn_w_down):
    B, S, D = x.shape
    N = B * S
    tri = jnp.triu(jnp.ones((TM, TM), F32)).astype(BF16)
    row = lambda a: a.astype(F32).reshape(a.shape[0], 1, a.shape[1])
    gm_all, gf_all = row(mix_norm_g), row(ffn_norm_g)
    wgu_all = wd_all = fox_wout = s5_win = s5_wglu = None
    fox_win = fox_w_in.astype(BF16)
    fox_wft = lax.optimization_barrier(fox_w_in[:, :, 3 * D:]).transpose(0, 2, 1).astype(BF16)
    fox_bf = fox_b_f.astype(F32)[:, :, None]
    s5_dd = row(s5_d)
    pool_ww, pool_bb, pool_sc = pool_w.astype(BF16), row(pool_b), row(pool_scale)

    for i in range(DEPTH):
        kind, j = i % 3, i // 3
        next_is_s5 = i + 1 < DEPTH and (i + 1) % 3 == 1
        final_g = final_norm_g[None].astype(F32) if i == DEPTH - 1 else None
        mix = None
        if kind == 0:
            qkv, c = _fox_proj(x, gm_all, i, fox_win, fox_wft, fox_bf, j, tri)
            c = c.reshape(B, FOX_HEADS // 2, 2, S // TK, TK).transpose(0, 1, 3, 2, 4)
            if wgu_all is None:
                late = (ffn_w_gate_up, ffn_w_down, fox_w_out, s5_w_in, s5_w_glu)
                o, *cast = _fox_attn(qkv, c, cast=tuple(
                    w.astype(F32).reshape(-1, w.shape[-1]) for w in late))
                wgu_all, wd_all, fox_wout, s5_win, s5_wglu = (
                    w2.reshape(w.shape) for w2, w in zip(cast, late))
            else:
                o, = _fox_attn(qkv, c)
            mix = (o.reshape(N, D), fox_wout, j)
        elif kind == 1:
            bblk, cblk, lam = _s5_params(
                s5_a_re[j], s5_a_im[j], s5_log_dt[j], s5_b_re[j], s5_b_im[j], s5_c_re[j], s5_c_im[j])
            x = _s5(x, gm_all, i, s5_win, bblk, cblk, lam, s5_dd, s5_wglu, j)
        else:
            x = _pool(x, gm_all, i, pool_ww, pool_bb, pool_sc, j)
        x = _ffn(x.reshape(N, D), gf_all, wgu_all, wd_all, i, mix=mix, final_g=final_g,
                 seg_in=kind == 1, seg_out=next_is_s5).reshape(B, S, D)
    return x
```

```python
import functools
import math

import jax
import jax.numpy as jnp
from jax import lax
from jax.experimental import pallas as pl
from jax.experimental.pallas import tpu as pltpu

F32 = jnp.float32
BF16 = jnp.bfloat16

D_MODEL = 1024
DEPTH = 4
EPS = 1e-6
FOX_HEADS = 16
FOX_HEAD_DIM = 64
S5_GROUP = 16
S5_GROUPS = 64
S5_STATE = 64
POOL_WINDOWS = (2, 4, 8, 16)
POOL_WIDTH = 256
D_FF = 2816

LANES = 128
SUBLANES = 8
MXU_DIM = 256
VMEM_LIMIT_BYTES = 56 * 1024 * 1024

LOG2E = 1.4426950408889634
NEG_BIG = -1e30

TM = 512
FF_CHUNK = MXU_DIM
TQ = 1024
TK = 1024
S5_SEGS = SUBLANES
S5_TJ = 32
S5_CHUNK = S5_SEGS * S5_TJ
S5_SLABS = 4
S5_SLAB_IN = D_MODEL // S5_SLABS
S5_SLAB_HALF = (S5_GROUPS // S5_SLABS) * S5_STATE
S5_SLAB_W = 2 * S5_SLAB_HALF
S5_SW = S5_SLABS * S5_SLAB_W
S5_COLS_PER_STEP = 4


def _resident(shape):
    nd = len(shape)
    return pl.BlockSpec(shape, lambda *_: (0,) * nd, pipeline_mode=pl.Buffered(1))


def _layer(arr, i):
    nd = arr.ndim
    return pl.BlockSpec((None,) + arr.shape[1:], lambda *_: (i,) + (0,) * (nd - 1),
                        pipeline_mode=pl.Buffered(1))


def _rms(x, g):
    return x * lax.rsqrt(jnp.mean(x * x, axis=-1, keepdims=True) + EPS) * g


def _dot(a, b):
    return jnp.dot(a, b, preferred_element_type=F32)


def _fox_proj_kernel(x_ref, g_ref, win_ref, wft_ref, bf_ref, tri_ref,
                     qkv_ref, c_ref, carry_ref, wqkv_ref):
    @pl.when((pl.program_id(0) == 0) & (pl.program_id(1) == 0))
    def _():
        wqkv_ref[...] = win_ref[:, :3 * D_MODEL].astype(BF16)

    @pl.when(pl.program_id(1) == 0)
    def _():
        carry_ref[...] = jnp.zeros_like(carry_ref)

    h = _rms(x_ref[0], g_ref[...]).astype(BF16)
    qkv = _dot(h, wqkv_ref[...])
    q = qkv[:, :D_MODEL] * (FOX_HEAD_DIM ** -0.5 * LOG2E)
    qkv_ref[0, :, :D_MODEL] = q.astype(BF16)
    qkv_ref[0, :, D_MODEL:] = qkv[:, D_MODEL:].astype(BF16)

    z = lax.dot_general(wft_ref[...], h, (((1,), (1,)), ((), ())),
                        preferred_element_type=F32) + bf_ref[...]
    logf = jnp.minimum(z, 0.0) - jnp.log(1.0 + jnp.exp(-jnp.abs(z)))
    hi = logf.astype(BF16)
    r1 = logf - hi.astype(F32)
    mid = r1.astype(BF16)
    lo = (r1 - mid.astype(F32)).astype(BF16)
    parts = _dot(jnp.concatenate([hi, mid, lo], axis=0), tri_ref[...])
    c = (parts[:FOX_HEADS] + parts[FOX_HEADS:2 * FOX_HEADS] + parts[2 * FOX_HEADS:]
         + carry_ref[:, 0:1])
    c_ref[0] = c * LOG2E
    carry_ref[...] = jnp.broadcast_to(c[:, TM - 1:TM], carry_ref.shape)


def _fox_proj(x, g, i, win, wft, bf, j, tri):
    B, S, D = x.shape
    return pl.pallas_call(
        _fox_proj_kernel,
        grid=(B, S // TM),
        in_specs=[
            pl.BlockSpec((1, TM, D), lambda b, t: (b, t, 0)),
            _layer(g, i), _layer(win, j), _layer(wft, j), _layer(bf, j),
            _resident((TM, TM)),
        ],
        out_specs=[
            pl.BlockSpec((1, TM, 3 * D), lambda b, t: (b, t, 0)),
            pl.BlockSpec((1, FOX_HEADS, TM), lambda b, t: (b, 0, t)),
        ],
        out_shape=[
            jax.ShapeDtypeStruct((B, S, 3 * D), BF16),
            jax.ShapeDtypeStruct((B, FOX_HEADS, S), F32),
        ],
        scratch_shapes=[pltpu.VMEM((FOX_HEADS, LANES), F32),
                        pltpu.VMEM((D, 3 * D), BF16)],
        compiler_params=pltpu.CompilerParams(
            dimension_semantics=("arbitrary", "arbitrary"),
            vmem_limit_bytes=VMEM_LIMIT_BYTES),
        name="fox_proj",
    )(x, g, win, wft, bf, tri)


def _fox_attn_kernel(*refs, n_cast):
    q_ref, k_ref, v_ref, c_ref = refs[:4]
    cast_in = refs[4:4 + n_cast]
    o_ref = refs[4 + n_cast]
    cast_out = refs[5 + n_cast:5 + 2 * n_cast]
    qs_ref, va_ref, vb_ref, s0_ref, s1_ref, m_ref, alpha_ref, acc_ref = refs[5 + 2 * n_cast:]
    for src_ref, dst_ref in zip(cast_in, cast_out):
        dst_ref[...] = src_ref[...].astype(BF16)
    assert TQ == TK
    S = q_ref.shape[1]
    nq = S // TQ
    lane = lax.broadcasted_iota(jnp.int32, (TQ, LANES), 1)
    first = lane < FOX_HEAD_DIM

    def prep_v(i, c):
        r = pl.multiple_of(i * TK, TK)
        v = v_ref[0, pl.ds(r, TK), :].astype(F32)
        ln = lax.broadcasted_iota(jnp.int32, v.shape, 1)
        va = jnp.where(ln < FOX_HEAD_DIM, v, jnp.where(ln == FOX_HEAD_DIM, 1.0, 0.0))
        vb = jnp.where(ln >= FOX_HEAD_DIM, v, jnp.where(ln == 0, 1.0, 0.0))
        va_ref[pl.ds(r, TK), :] = va.astype(BF16)
        vb_ref[pl.ds(r, TK), :] = vb.astype(BF16)
        return c

    lax.fori_loop(0, S // TK, prep_v, 0)

    def prep_q(qb, slot):
        q = q_ref[0, pl.ds(pl.multiple_of(qb * TQ, TQ), TQ), :]
        zero = jnp.zeros_like(q)
        qs_ref[slot, 0:TQ] = jnp.where(first, q, zero)
        qs_ref[slot, TQ:2 * TQ] = jnp.where(first, zero, q)

    H = TQ // 2

    def qk(q, k):
        return lax.dot_general(q, k, (((1,), (1,)), ((), ())), preferred_element_type=F32)

    def key_bias(ck, lo, width, rows):
        return jnp.concatenate(
            [jnp.broadcast_to(ck[0:1, lo:lo + width], (rows, width)),
             jnp.broadcast_to(ck[1:2, lo:lo + width], (rows, width))], axis=0)

    def causal(s):
        row = lax.broadcasted_iota(jnp.int32, s.shape, 0)
        col = lax.broadcasted_iota(jnp.int32, s.shape, 1)
        return jnp.where(col <= row, s, NEG_BIG)

    def lanes(m, width):
        return jnp.concatenate([m] * (width // LANES), axis=1)

    def next_full(nxt_ref, qslot, ci):
        k = k_ref[0, pl.ds(pl.multiple_of(ci * TK, TK), TK), :]
        s = qk(qs_ref[qslot], k) - key_bias(c_ref[0, 0, ci], 0, TK, TQ)
        nxt_ref[...] = s
        return [(0, 2 * TQ, jnp.max(s, axis=1, keepdims=True))]

    def next_diag(nxt_ref, qslot, ci):
        r = pl.multiple_of(ci * TK, TK)
        q = qs_ref[qslot]
        ck = c_ref[0, 0, ci]
        left = qk(q, k_ref[0, pl.ds(r, H), :]) - key_bias(ck, 0, H, TQ)
        left = jnp.concatenate([causal(left[0:H]), left[H:TQ],
                                causal(left[TQ:TQ + H]), left[TQ + H:]], axis=0)
        q_hi = jnp.concatenate([q[H:TQ], q[TQ + H:]], axis=0)
        right = qk(q_hi, k_ref[0, pl.ds(pl.multiple_of(r + H, H), H), :])
        right = right - key_bias(ck, H, H, H)
        right = jnp.concatenate([causal(right[:H]), causal(right[H:])], axis=0)
        nxt_ref[:, 0:H] = left
        nxt_ref[H:TQ, H:TK] = right[:H]
        nxt_ref[TQ + H:, H:TK] = right[H:]
        rowmax = lambda a: jnp.max(a, axis=1, keepdims=True)
        return [(0, H, rowmax(left[0:H])),
                (H, TQ, jnp.maximum(rowmax(left[H:TQ]), rowmax(right[:H]))),
                (TQ, TQ + H, rowmax(left[TQ:TQ + H])),
                (TQ + H, 2 * TQ, jnp.maximum(rowmax(left[TQ + H:]), rowmax(right[H:])))]

    def pv_full(cur_ref, ci, m_cur):
        p = jnp.exp2(cur_ref[...] - lanes(m_cur, TK)).astype(BF16)
        r = pl.multiple_of(ci * TK, TK)
        return jnp.concatenate([_dot(p[:TQ], va_ref[pl.ds(r, TK), :]),
                                _dot(p[TQ:], vb_ref[pl.ds(r, TK), :])], axis=0)

    def pv_diag(cur_ref, ci, m_cur):
        r = pl.multiple_of(ci * TK, TK)
        r_hi = pl.multiple_of(r + H, H)
        p_left = jnp.exp2(cur_ref[:, 0:H] - lanes(m_cur, H)).astype(BF16)
        out = []
        for head, vh_ref in enumerate((va_ref, vb_ref)):
            lo, hi = head * TQ, head * TQ + H
            pv = _dot(p_left[lo:lo + TQ], vh_ref[pl.ds(r, H), :])
            p_right = jnp.exp2(cur_ref[hi:hi + H, H:TK] - lanes(m_cur[hi:hi + H], H))
            pv_hi = _dot(p_right.astype(BF16), vh_ref[pl.ds(r_hi, H), :])
            out += [pv[:H], pv[H:] + pv_hi]
        return jnp.concatenate(out, axis=0)

    def step_on(cur_ref, nxt_ref, ci, qslot_n, ci_n, cur_diag, nxt_diag):
        if nxt_diag is None:
            row_maxes = []
        else:
            row_maxes = (next_diag if nxt_diag else next_full)(nxt_ref, qslot_n, ci_n)
        m_cur = m_ref[...]
        alpha = alpha_ref[...]
        for lo, hi, mx in row_maxes:
            if cur_diag:
                m_ref[lo:hi] = jnp.broadcast_to(mx, (hi - lo, LANES))
                alpha_ref[lo:hi] = jnp.zeros((hi - lo, LANES), F32)
            else:
                m_next = jnp.maximum(m_cur[lo:hi], mx)
                m_ref[lo:hi] = m_next
                alpha_ref[lo:hi] = jnp.exp2(m_cur[lo:hi] - m_next)
        pv = (pv_diag if cur_diag else pv_full)(cur_ref, ci, m_cur)
        acc = alpha * acc_ref[...] + pv
        if not cur_diag:
            acc_ref[...] = acc
            return
        la = jnp.broadcast_to(acc[:TQ, FOX_HEAD_DIM:FOX_HEAD_DIM + 1], (TQ, LANES))
        lb = jnp.broadcast_to(acc[TQ:, 0:1], (TQ, LANES))
        o = jnp.where(first, acc[:TQ] / la, acc[TQ:] / lb)
        o_ref[0, pl.ds(pl.multiple_of(ci * TQ, TQ), TQ), :] = o.astype(BF16)
        acc_ref[...] = jnp.zeros_like(acc_ref)
        prep_q(jnp.minimum(ci + 2, nq - 1), ci & 1)

    def step(cnt, *args):
        lax.cond((cnt & 1) == 0,
                 lambda: step_on(s0_ref, s1_ref, *args),
                 lambda: step_on(s1_ref, s0_ref, *args))
        return cnt + 1

    def q_block(qb, cnt):
        qslot = qb & 1
        last = qb

        def inner(ci, cnt):
            return step(cnt, ci, qslot, ci + 1, False, False)

        cnt = lax.fori_loop(0, last - 1, inner, cnt)

        def before_diag(ci, cnt):
            return step(cnt, ci, qslot, ci + 1, False, True)

        cnt = lax.fori_loop(jnp.maximum(last - 1, 0), last, before_diag, cnt)
        cnt = lax.cond(qb == nq - 1,
                       lambda c: step(c, last, None, None, True, None),
                       lambda c: step(c, last, 1 - qslot, 0, True, False), cnt)
        return cnt

    prep_q(0, 0)
    prep_q(1, 1)
    acc_ref[...] = jnp.zeros_like(acc_ref)
    for lo, hi, mx in next_diag(s0_ref, 0, 0):
        m_ref[lo:hi] = jnp.broadcast_to(mx, (hi - lo, LANES))
    alpha_ref[...] = jnp.zeros_like(alpha_ref)
    lax.fori_loop(0, nq, q_block, 0)


def _fox_attn(qkv, c, cast=()):
    B, S, _ = qkv.shape
    pairs = FOX_HEADS // 2
    steps = B * pairs
    blk = lambda off: pl.BlockSpec((1, S, LANES), lambda b, p: (b, 0, off + p))
    cast_specs = [pl.BlockSpec((w.shape[0] // steps, w.shape[1]), lambda b, p: (b * pairs + p, 0))
                  for w in cast]
    return pl.pallas_call(
        functools.partial(_fox_attn_kernel, n_cast=len(cast)),
        grid=(B, pairs),
        in_specs=[
            blk(0), blk(pairs), blk(2 * pairs),
            pl.BlockSpec((1, 1, S // TK, 2, TK), lambda b, p: (b, p, 0, 0, 0)),
        ] + cast_specs,
        out_specs=[pl.BlockSpec((1, S, LANES), lambda b, p: (b, 0, p))] + cast_specs,
        out_shape=[jax.ShapeDtypeStruct((B, S, D_MODEL), BF16)]
                  + [jax.ShapeDtypeStruct(w.shape, BF16) for w in cast],
        scratch_shapes=[
            pltpu.VMEM((2, 2 * TQ, LANES), BF16),
            pltpu.VMEM((S, LANES), BF16),
            pltpu.VMEM((S, LANES), BF16),
            pltpu.VMEM((2 * TQ, TK), F32),
            pltpu.VMEM((2 * TQ, TK), F32),
            pltpu.VMEM((2 * TQ, LANES), F32),
            pltpu.VMEM((2 * TQ, LANES), F32),
            pltpu.VMEM((2 * TQ, LANES), F32),
        ],
        compiler_params=pltpu.CompilerParams(
            dimension_semantics=("parallel", "parallel"),
            vmem_limit_bytes=VMEM_LIMIT_BYTES),
        name="fox_attn",
    )(qkv, qkv, qkv, c, *cast)


def _ffn_kernel(*refs, has_mix, final_norm, seg_in, seg_out):
    refs = list(refs)
    x_ref = refs.pop(0)
    if has_mix:
        a_ref = refs.pop(0)
        wmix_ref = refs.pop(0)
    g_ref, wgu_ref, wd_ref = refs[:3]
    refs = refs[3:]
    if final_norm:
        gf_ref = refs.pop(0)
    if seg_in or seg_out:
        o_ref, act_ref, slab_ref = refs
    else:
        o_ref, act_ref = refs
    n_slabs = D_MODEL // LANES
    seg_rows = [(ch * S5_CHUNK + sg, ch * S5_CHUNK + sg * S5_TJ)
                for ch in range(TM // S5_CHUNK) for sg in range(S5_SEGS)]

    if seg_in:
        cols = []
        for c in range(n_slabs):
            slab_ref[c] = x_ref[:, c * LANES:(c + 1) * LANES]
            cols.append(jnp.concatenate(
                [slab_ref[c, pl.ds(r_seg, S5_TJ, stride=S5_SEGS), :] for r_seg, _ in seg_rows],
                axis=0))
        x = jnp.concatenate(cols, axis=1)
    else:
        x = x_ref[...]
    if has_mix:
        x = x + _dot(a_ref[...], wmix_ref[...])
    h = _rms(x, g_ref[...]).astype(BF16)
    for c in range(D_FF // FF_CHUNK):
        lo = c * FF_CHUNK
        gate = _dot(h, wgu_ref[:, lo:lo + FF_CHUNK])
        up = _dot(h, wgu_ref[:, D_FF + lo:D_FF + lo + FF_CHUNK])
        act_ref[:, lo:lo + FF_CHUNK] = (gate * jax.nn.sigmoid(gate) * up).astype(BF16)
    y = x + _dot(act_ref[...], wd_ref[...])
    if final_norm:
        y = _rms(y, gf_ref[...])
    if seg_out:
        for c in range(n_slabs):
            for r_seg, r_time in seg_rows:
                slab_ref[c, pl.ds(r_seg, S5_TJ, stride=S5_SEGS), :] = (
                    y[r_time:r_time + S5_TJ, c * LANES:(c + 1) * LANES])
            o_ref[:, c * LANES:(c + 1) * LANES] = slab_ref[c]
    else:
        o_ref[...] = y


def _ffn(x, g, wgu, wd, i, mix=None, final_g=None, seg_in=False, seg_out=False):
    N, D = x.shape
    row = pl.BlockSpec((TM, D), lambda t: (t, 0))
    scratch = [pltpu.VMEM((TM, D_FF), BF16)]
    if seg_in or seg_out:
        scratch.append(pltpu.VMEM((D // LANES, TM, LANES), F32))
    args, specs = [x], [row]
    if mix is not None:
        a2, wmix, j = mix
        args += [a2, wmix]
        specs += [row, _layer(wmix, j)]
    args += [g, wgu, wd]
    specs += [_layer(g, i), _layer(wgu, i), _layer(wd, i)]
    if final_g is not None:
        args.append(final_g)
        specs.append(_resident(final_g.shape))
    return pl.pallas_call(
        functools.partial(_ffn_kernel, has_mix=mix is not None,
                          final_norm=final_g is not None, seg_in=seg_in, seg_out=seg_out),
        grid=(N // TM,),
        in_specs=specs,
        out_specs=row,
        out_shape=jax.ShapeDtypeStruct((N, D), F32),
        scratch_shapes=scratch,
        compiler_params=pltpu.CompilerParams(
            dimension_semantics=("parallel",),
            vmem_limit_bytes=VMEM_LIMIT_BYTES),
        name="ffn",
    )(*args)


def _s5_kernel(x_ref, g_ref, win_ref, bblk_ref, cblk_ref, lam_ref,
               d_ref, wglu_ref, o_ref, st_ref, carry_ref):
    @pl.when(pl.program_id(1) == 0)
    def _():
        carry_ref[...] = jnp.zeros_like(carry_ref)

    x = x_ref[0]
    h = _rms(x, g_ref[...]).astype(BF16)
    u = _dot(h, win_ref[...])
    ub = u.astype(BF16)

    sub = lax.broadcasted_iota(jnp.int32, (SUBLANES, LANES), 0)
    vshape = (SUBLANES, LANES)
    blocks_per_slab = S5_SLAB_HALF // LANES
    n_blocks = S5_SLABS * blocks_per_slab

    def cmul(ar, ai, br, bi):
        return ar * br - ai * bi, ar * bi + ai * br

    groups_per_slab = blocks_per_slab // S5_COLS_PER_STEP
    ys = []
    for grp in range(n_blocks // S5_COLS_PER_STEP):
        sl = grp // groups_per_slab
        if grp % groups_per_slab == 0:
            st_ref[:, sl * S5_SLAB_W:(sl + 1) * S5_SLAB_W] = _dot(
                ub[:, sl * S5_SLAB_IN:(sl + 1) * S5_SLAB_IN], bblk_ref[sl])
        offs = []
        for b in range(grp * S5_COLS_PER_STEP, (grp + 1) * S5_COLS_PER_STEP):
            re = (b // blocks_per_slab) * S5_SLAB_W + (b % blocks_per_slab) * LANES
            offs.append((re, re + S5_SLAB_HALF))
        lam = [(jnp.broadcast_to(lam_ref[:, re:re + LANES], vshape),
                jnp.broadcast_to(lam_ref[:, im:im + LANES], vshape)) for re, im in offs]

        def scan_body(j, state):
            r = j * SUBLANES
            new = []
            for (re, im), (lr, li), (sr, si) in zip(offs, lam, state):
                pr, pi = cmul(lr, li, sr, si)
                nr = pr + st_ref[pl.ds(r, SUBLANES), re:re + LANES]
                ni = pi + st_ref[pl.ds(r, SUBLANES), im:im + LANES]
                st_ref[pl.ds(r, SUBLANES), re:re + LANES] = nr
                st_ref[pl.ds(r, SUBLANES), im:im + LANES] = ni
                new.append((nr, ni))
            return tuple(new)

        init = tuple((carry_ref[:, re:re + LANES], carry_ref[:, im:im + LANES])
                     for re, im in offs)
        fin = init
        for j in range(S5_TJ):
            fin = scan_body(j, fin)

        carries = []
        for (re, im), (lr, li), (fr, fi) in zip(offs, lam, fin):
            for _ in range(S5_TJ.bit_length() - 1):
                lr, li = cmul(lr, li, lr, li)
            er, ei = fr, fi
            for k in range(1, S5_SEGS):
                pr, pi = cmul(lr, li, pltpu.roll(er, 1, 0), pltpu.roll(ei, 1, 0))
                er = jnp.where(sub == k, fr + pr, er)
                ei = jnp.where(sub == k, fi + pi, ei)
            sr, si = pltpu.roll(er, 1, 0), pltpu.roll(ei, 1, 0)
            carry_ref[:, re:re + LANES] = jnp.where(sub == 0, sr, 0.0)
            carry_ref[:, im:im + LANES] = jnp.where(sub == 0, si, 0.0)
            carries.append((jnp.where(sub == 0, 0.0, sr), jnp.where(sub == 0, 0.0, si)))

        def fix_body(j, f):
            r = j * SUBLANES
            new = []
            for (re, im), (lr, li), (fr, fi) in zip(offs, lam, f):
                fr, fi = cmul(lr, li, fr, fi)
                st_ref[pl.ds(r, SUBLANES), re:re + LANES] += fr
                st_ref[pl.ds(r, SUBLANES), im:im + LANES] += fi
                new.append((fr, fi))
            return tuple(new)

        f = tuple(carries)
        for j in range(S5_TJ):
            f = fix_body(j, f)

        if grp % groups_per_slab == groups_per_slab - 1:
            ys.append(_dot(st_ref[:, sl * S5_SLAB_W:(sl + 1) * S5_SLAB_W].astype(BF16),
                           cblk_ref[sl]))

    y = jnp.concatenate(ys, axis=1) + d_ref[...] * u
    gl = jax.nn.gelu(y).astype(BF16)
    vg = _dot(gl, wglu_ref[...])
    o_ref[0] = x + vg[:, :D_MODEL] * jax.nn.sigmoid(vg[:, D_MODEL:])


def _s5(xp, g, i, win, bblk, cblk, lam, d, wglu, j):
    B, S, D = xp.shape
    return pl.pallas_call(
        _s5_kernel,
        grid=(B, S // S5_CHUNK),
        in_specs=[
            pl.BlockSpec((1, S5_CHUNK, D), lambda b, c: (b, c, 0)),
            _layer(g, i), _layer(win, j), _resident(bblk.shape),
            _resident(cblk.shape), _resident(lam.shape),
            _layer(d, j), _layer(wglu, j),
        ],
        out_specs=pl.BlockSpec((1, S5_CHUNK, D), lambda b, c: (b, c, 0)),
        out_shape=jax.ShapeDtypeStruct((B, S, D), F32),
        scratch_shapes=[
            pltpu.VMEM((S5_CHUNK, S5_SW), F32),
            pltpu.VMEM((SUBLANES, S5_SW), F32),
        ],
        compiler_params=pltpu.CompilerParams(
            dimension_semantics=("parallel", "arbitrary"),
            vmem_limit_bytes=VMEM_LIMIT_BYTES),
        name="s5",
    )(xp, g, win, bblk, cblk, lam, d, wglu)


def _s5_params(a_re, a_im, log_dt, b_re, b_im, c_re, c_im):
    ar, ai = a_re.astype(F32), a_im.astype(F32)
    dt = jnp.exp(log_dt.astype(F32))[:, None]
    mag = jnp.exp(ar * dt)
    lr, li = mag * jnp.cos(ai * dt), mag * jnp.sin(ai * dt)
    den = ar * ar + ai * ai
    kr = ((lr - 1.0) * ar + li * ai) / den
    ki = (li * ar - (lr - 1.0) * ai) / den
    br, bi = b_re.astype(F32), b_im.astype(F32)
    bbr = kr[..., None] * br - ki[..., None] * bi
    bbi = kr[..., None] * bi + ki[..., None] * br

    gs = S5_GROUPS // S5_SLABS
    eye = jnp.eye(gs, dtype=F32)

    def state_row(zr, zi):
        zr = zr.reshape(S5_SLABS, S5_SLAB_HALF)
        zi = zi.reshape(S5_SLABS, S5_SLAB_HALF)
        return jnp.concatenate([zr, zi], axis=-1).reshape(S5_SW)

    same_group = (eye > 0)[None, :, None, :, None]

    def in_block(b):
        bt = b.reshape(S5_SLABS, gs, S5_STATE, S5_GROUP).transpose(0, 1, 3, 2)
        full = jnp.where(same_group, bt[:, :, :, None, :], 0.0)
        return full.reshape(S5_SLABS, S5_SLAB_IN, S5_SLAB_HALF)

    def out_block(c):
        ct = c.reshape(S5_SLABS, gs, S5_GROUP, S5_STATE).transpose(0, 1, 3, 2)
        full = jnp.where(same_group, ct[:, :, :, None, :], 0.0)
        return full.reshape(S5_SLABS, S5_SLAB_HALF, S5_SLAB_IN)

    bblk = jnp.concatenate([in_block(bbr), in_block(bbi)], axis=2)
    cblk = jnp.concatenate([out_block(c_re.astype(F32)), -out_block(c_im.astype(F32))], axis=1)
    return bblk.astype(BF16), cblk.astype(BF16), state_row(lr, li)[None]


POOL_HALO = 32


def _pool_kernel(x_ref, halo_ref, g_ref, w_ref, b_ref, sc_ref, o_ref, a_ref, b2_ref):
    t = pl.program_id(1)
    x = x_ref[0]
    g = g_ref[...]
    h = _rms(x, g)
    hh = _rms(halo_ref[0], g)
    a_ref[0:POOL_HALO] = jnp.where(t > 0, hh, 0.0)
    a_ref[POOL_HALO:] = h
    b2_ref[0:SUBLANES] = jnp.zeros((SUBLANES, D_MODEL), F32)
    rows = POOL_HALO + TM
    head = 2 * SUBLANES
    assert head >= max(POOL_WINDOWS) - 1
    tpos = t * TM + lax.broadcasted_iota(jnp.int32, (head, POOL_WIDTH), 0) + 1
    outs = []
    for gi, w in enumerate(POOL_WINDOWS):
        lo = gi * POOL_WIDTH
        cols = slice(lo, lo + POOL_WIDTH)
        src, dst, sh = a_ref, b2_ref, 1
        while sh < w:
            dst[SUBLANES:rows, cols] = src[SUBLANES:rows, cols] + src[SUBLANES - sh:rows - sh, cols]
            src, dst, sh = dst, src, 2 * sh
        total = src[POOL_HALO:rows, cols]
        mean = jnp.concatenate([total[:head] / jnp.minimum(tpos, w).astype(F32),
                                total[head:] * (1.0 / w)], axis=0)
        outs.append(_dot((mean - h[:, cols]).astype(BF16), w_ref[gi]))
    y = (jnp.concatenate(outs, axis=1) + b_ref[...]) * sc_ref[...]
    o_ref[0] = x + y


def _pool(x, g, i, w, b, sc, j):
    B, S, D = x.shape
    per = TM // POOL_HALO
    return pl.pallas_call(
        _pool_kernel,
        grid=(B, S // TM),
        in_specs=[
            pl.BlockSpec((1, TM, D), lambda b, t: (b, t, 0)),
            pl.BlockSpec((1, POOL_HALO, D), lambda b, t: (b, jnp.maximum(t * per - 1, 0), 0)),
            _layer(g, i), _layer(w, j), _layer(b, j), _layer(sc, j),
        ],
        out_specs=pl.BlockSpec((1, TM, D), lambda b, t: (b, t, 0)),
        out_shape=jax.ShapeDtypeStruct((B, S, D), F32),
        scratch_shapes=[pltpu.VMEM((POOL_HALO + TM, D), F32),
                        pltpu.VMEM((POOL_HALO + TM, D), F32)],
        compiler_params=pltpu.CompilerParams(
            dimension_semantics=("parallel", "arbitrary"),
            vmem_limit_bytes=VMEM_LIMIT_BYTES),
        name="pool",
    )(x, x, g, w, b, sc)


def kernel(x, mix_norm_g, ffn_norm_g, final_norm_g, fox_w_in, fox_b_f, fox_w_out, s5_w_in, s5_a_re, s5_a_im, s5_log_dt, s5_b_re, s5_b_im, s5_c_re, s5_c_im, s5_d, s5_w_glu, pool_w, pool_b, pool_scale, ffn_w_gate_up, ffn_w_down):
    B, S, D = x.shape
    N = B * S
    tri = jnp.triu(jnp.ones((TM, TM), F32)).astype(BF16)
    row = lambda a: a.astype(F32).reshape(a.shape[0], 1, a.shape[1])
    gm_all, gf_all = row(mix_norm_g), row(ffn_norm_g)
    wgu_all = wd_all = fox_wout = s5_win = s5_wglu = None
    fox_win = fox_w_in.astype(F32)
    fox_wft = lax.optimization_barrier(fox_w_in[:, :, 3 * D:]).transpose(0, 2, 1).astype(BF16)
    fox_bf = fox_b_f.astype(F32)[:, :, None]
    s5_dd = row(s5_d)
    pool_ww, pool_bb, pool_sc = pool_w.astype(BF16), row(pool_b), row(pool_scale)

    for i in range(DEPTH):
        kind, j = i % 3, i // 3
        next_is_s5 = i + 1 < DEPTH and (i + 1) % 3 == 1
        final_g = final_norm_g[None].astype(F32) if i == DEPTH - 1 else None
        mix = None
        if kind == 0:
            qkv, c = _fox_proj(x, gm_all, i, fox_win, fox_wft, fox_bf, j, tri)
            c = c.reshape(B, FOX_HEADS // 2, 2, S // TK, TK).transpose(0, 1, 3, 2, 4)
            if wgu_all is None:
                late = (ffn_w_gate_up, ffn_w_down, fox_w_out, s5_w_in, s5_w_glu)
                o, *cast = _fox_attn(qkv, c, cast=tuple(
                    w.astype(F32).reshape(-1, w.shape[-1]) for w in late))
                wgu_all, wd_all, fox_wout, s5_win, s5_wglu = (
                    w2.reshape(w.shape) for w2, w in zip(cast, late))
            else:
                o, = _fox_attn(qkv, c)
            mix = (o.reshape(N, D), fox_wout, j)
        elif kind == 1:
            bblk, cblk, lam = _s5_params(
                s5_a_re[j], s5_a_im[j], s5_log_dt[j], s5_b_re[j], s5_b_im[j], s5_c_re[j], s5_c_im[j])
            x = _s5(x, gm_all, i, s5_win, bblk, cblk, lam, s5_dd, s5_wglu, j)
        else:
            x = _pool(x, gm_all, i, pool_ww, pool_bb, pool_sc, j)
        x = _ffn(x.reshape(N, D), gf_all, wgu_all, wd_all, i, mix=mix, final_g=final_g,
                 seg_in=kind == 1, seg_out=next_is_s5).reshape(B, S, D)
    return x
```

```python
import functools
import math

import jax
import jax.numpy as jnp
from jax import lax
from jax.experimental import pallas as pl
from jax.experimental.pallas import tpu as pltpu

F32 = jnp.float32
BF16 = jnp.bfloat16

D_MODEL = 1024
DEPTH = 4
EPS = 1e-6
FOX_HEADS = 16
FOX_HEAD_DIM = 64
S5_GROUP = 16
S5_GROUPS = 64
S5_STATE = 64
POOL_WINDOWS = (2, 4, 8, 16)
POOL_WIDTH = 256
D_FF = 2816

LANES = 128
SUBLANES = 8
MXU_DIM = 256
VMEM_LIMIT_BYTES = 56 * 1024 * 1024

LOG2E = 1.4426950408889634
NEG_BIG = -1e30

TM = 512
FF_CHUNK = MXU_DIM
TQ = 1024
TK = 1024
S5_SEGS = SUBLANES
S5_TJ = 32
S5_CHUNK = S5_SEGS * S5_TJ
S5_SLABS = 4
S5_SLAB_IN = D_MODEL // S5_SLABS
S5_SLAB_HALF = (S5_GROUPS // S5_SLABS) * S5_STATE
S5_SLAB_W = 2 * S5_SLAB_HALF
S5_SW = S5_SLABS * S5_SLAB_W
S5_COLS_PER_STEP = 4


def _resident(shape):
    nd = len(shape)
    return pl.BlockSpec(shape, lambda *_: (0,) * nd, pipeline_mode=pl.Buffered(1))


def _layer(arr, i):
    nd = arr.ndim
    return pl.BlockSpec((None,) + arr.shape[1:], lambda *_: (i,) + (0,) * (nd - 1),
                        pipeline_mode=pl.Buffered(1))


def _rms(x, g):
    return x * lax.rsqrt(jnp.mean(x * x, axis=-1, keepdims=True) + EPS) * g


def _dot(a, b):
    return jnp.dot(a, b, preferred_element_type=F32)


def _fox_proj_kernel(x_ref, g_ref, wqkv_ref, wft_ref, bf_ref, tri_ref,
                     qkv_ref, c_ref, carry_ref):
    @pl.when(pl.program_id(1) == 0)
    def _():
        carry_ref[...] = jnp.zeros_like(carry_ref)

    h = _rms(x_ref[0], g_ref[...]).astype(BF16)
    qkv = _dot(h, wqkv_ref[:, :3 * D_MODEL])
    q = qkv[:, :D_MODEL] * (FOX_HEAD_DIM ** -0.5 * LOG2E)
    qkv_ref[0, :, :D_MODEL] = q.astype(BF16)
    qkv_ref[0, :, D_MODEL:] = qkv[:, D_MODEL:].astype(BF16)

    z = lax.dot_general(wft_ref[...], h, (((1,), (1,)), ((), ())),
                        preferred_element_type=F32) + bf_ref[...]
    logf = jnp.minimum(z, 0.0) - jnp.log(1.0 + jnp.exp(-jnp.abs(z)))
    hi = logf.astype(BF16)
    r1 = logf - hi.astype(F32)
    mid = r1.astype(BF16)
    lo = (r1 - mid.astype(F32)).astype(BF16)
    parts = _dot(jnp.concatenate([hi, mid, lo], axis=0), tri_ref[...])
    c = (parts[:FOX_HEADS] + parts[FOX_HEADS:2 * FOX_HEADS] + parts[2 * FOX_HEADS:]
         + carry_ref[:, 0:1])
    c_ref[0] = c * LOG2E
    carry_ref[...] = jnp.broadcast_to(c[:, TM - 1:TM], carry_ref.shape)


def _fox_proj(x, g, i, win, wft, bf, j, tri):
    B, S, D = x.shape
    return pl.pallas_call(
        _fox_proj_kernel,
        grid=(B, S // TM),
        in_specs=[
            pl.BlockSpec((1, TM, D), lambda b, t: (b, t, 0)),
            _layer(g, i), _layer(win, j), _layer(wft, j), _layer(bf, j),
            _resident((TM, TM)),
        ],
        out_specs=[
            pl.BlockSpec((1, TM, 3 * D), lambda b, t: (b, t, 0)),
            pl.BlockSpec((1, FOX_HEADS, TM), lambda b, t: (b, 0, t)),
        ],
        out_shape=[
            jax.ShapeDtypeStruct((B, S, 3 * D), BF16),
            jax.ShapeDtypeStruct((B, FOX_HEADS, S), F32),
        ],
        scratch_shapes=[pltpu.VMEM((FOX_HEADS, LANES), F32)],
        compiler_params=pltpu.CompilerParams(
            dimension_semantics=("parallel", "arbitrary"),
            vmem_limit_bytes=VMEM_LIMIT_BYTES),
        name="fox_proj",
    )(x, g, win, wft, bf, tri)


def _fox_attn_kernel(*refs, n_cast):
    q_ref, k_ref, v_ref, c_ref = refs[:4]
    cast_in = refs[4:4 + n_cast]
    o_ref = refs[4 + n_cast]
    cast_out = refs[5 + n_cast:5 + 2 * n_cast]
    qs_ref, va_ref, vb_ref, s0_ref, s1_ref, m_ref, alpha_ref, acc_ref = refs[5 + 2 * n_cast:]
    for src_ref, dst_ref in zip(cast_in, cast_out):
        dst_ref[...] = src_ref[...].astype(BF16)
    assert TQ == TK
    S = q_ref.shape[1]
    nq = S // TQ
    lane = lax.broadcasted_iota(jnp.int32, (TQ, LANES), 1)
    first = lane < FOX_HEAD_DIM

    def prep_v(i, c):
        r = pl.multiple_of(i * TK, TK)
        v = v_ref[0, pl.ds(r, TK), :].astype(F32)
        ln = lax.broadcasted_iota(jnp.int32, v.shape, 1)
        va = jnp.where(ln < FOX_HEAD_DIM, v, jnp.where(ln == FOX_HEAD_DIM, 1.0, 0.0))
        vb = jnp.where(ln >= FOX_HEAD_DIM, v, jnp.where(ln == 0, 1.0, 0.0))
        va_ref[pl.ds(r, TK), :] = va.astype(BF16)
        vb_ref[pl.ds(r, TK), :] = vb.astype(BF16)
        return c

    lax.fori_loop(0, S // TK, prep_v, 0)

    def prep_q(qb, slot):
        q = q_ref[0, pl.ds(pl.multiple_of(qb * TQ, TQ), TQ), :]
        zero = jnp.zeros_like(q)
        qs_ref[slot, 0:TQ] = jnp.where(first, q, zero)
        qs_ref[slot, TQ:2 * TQ] = jnp.where(first, zero, q)

    H = TQ // 2

    def qk(q, k):
        return lax.dot_general(q, k, (((1,), (1,)), ((), ())), preferred_element_type=F32)

    def key_bias(ck, lo, width, rows):
        return jnp.concatenate(
            [jnp.broadcast_to(ck[0:1, lo:lo + width], (rows, width)),
             jnp.broadcast_to(ck[1:2, lo:lo + width], (rows, width))], axis=0)

    def causal(s):
        row = lax.broadcasted_iota(jnp.int32, s.shape, 0)
        col = lax.broadcasted_iota(jnp.int32, s.shape, 1)
        return jnp.where(col <= row, s, NEG_BIG)

    def lanes(m, width):
        return jnp.concatenate([m] * (width // LANES), axis=1)

    def next_full(nxt_ref, qslot, ci):
        k = k_ref[0, pl.ds(pl.multiple_of(ci * TK, TK), TK), :]
        s = qk(qs_ref[qslot], k) - key_bias(c_ref[0, 0, ci], 0, TK, TQ)
        nxt_ref[...] = s
        return [(0, 2 * TQ, jnp.max(s, axis=1, keepdims=True))]

    def next_diag(nxt_ref, qslot, ci):
        r = pl.multiple_of(ci * TK, TK)
        q = qs_ref[qslot]
        ck = c_ref[0, 0, ci]
        left = qk(q, k_ref[0, pl.ds(r, H), :]) - key_bias(ck, 0, H, TQ)
        left = jnp.concatenate([causal(left[0:H]), left[H:TQ],
                                causal(left[TQ:TQ + H]), left[TQ + H:]], axis=0)
        q_hi = jnp.concatenate([q[H:TQ], q[TQ + H:]], axis=0)
        right = qk(q_hi, k_ref[0, pl.ds(pl.multiple_of(r + H, H), H), :])
        right = right - key_bias(ck, H, H, H)
        right = jnp.concatenate([causal(right[:H]), causal(right[H:])], axis=0)
        nxt_ref[:, 0:H] = left
        nxt_ref[H:TQ, H:TK] = right[:H]
        nxt_ref[TQ + H:, H:TK] = right[H:]
        rowmax = lambda a: jnp.max(a, axis=1, keepdims=True)
        return [(0, H, rowmax(left[0:H])),
                (H, TQ, jnp.maximum(rowmax(left[H:TQ]), rowmax(right[:H]))),
                (TQ, TQ + H, rowmax(left[TQ:TQ + H])),
                (TQ + H, 2 * TQ, jnp.maximum(rowmax(left[TQ + H:]), rowmax(right[H:])))]

    def pv_full(cur_ref, ci, m_cur):
        p = jnp.exp2(cur_ref[...] - lanes(m_cur, TK)).astype(BF16)
        r = pl.multiple_of(ci * TK, TK)
        return jnp.concatenate([_dot(p[:TQ], va_ref[pl.ds(r, TK), :]),
                                _dot(p[TQ:], vb_ref[pl.ds(r, TK), :])], axis=0)

    def pv_diag(cur_ref, ci, m_cur):
        r = pl.multiple_of(ci * TK, TK)
        r_hi = pl.multiple_of(r + H, H)
        p_left = jnp.exp2(cur_ref[:, 0:H] - lanes(m_cur, H)).astype(BF16)
        out = []
        for head, vh_ref in enumerate((va_ref, vb_ref)):
            lo, hi = head * TQ, head * TQ + H
            pv = _dot(p_left[lo:lo + TQ], vh_ref[pl.ds(r, H), :])
            p_right = jnp.exp2(cur_ref[hi:hi + H, H:TK] - lanes(m_cur[hi:hi + H], H))
            pv_hi = _dot(p_right.astype(BF16), vh_ref[pl.ds(r_hi, H), :])
            out += [pv[:H], pv[H:] + pv_hi]
        return jnp.concatenate(out, axis=0)

    def step_on(cur_ref, nxt_ref, ci, qslot_n, ci_n, cur_diag, nxt_diag):
        if nxt_diag is None:
            row_maxes = []
        else:
            row_maxes = (next_diag if nxt_diag else next_full)(nxt_ref, qslot_n, ci_n)
        m_cur = m_ref[...]
        alpha = alpha_ref[...]
        for lo, hi, mx in row_maxes:
            if cur_diag:
                m_ref[lo:hi] = jnp.broadcast_to(mx, (hi - lo, LANES))
                alpha_ref[lo:hi] = jnp.zeros((hi - lo, LANES), F32)
            else:
                m_next = jnp.maximum(m_cur[lo:hi], mx)
                m_ref[lo:hi] = m_next
                alpha_ref[lo:hi] = jnp.exp2(m_cur[lo:hi] - m_next)
        pv = (pv_diag if cur_diag else pv_full)(cur_ref, ci, m_cur)
        acc = alpha * acc_ref[...] + pv
        if not cur_diag:
            acc_ref[...] = acc
            return
        la = jnp.broadcast_to(acc[:TQ, FOX_HEAD_DIM:FOX_HEAD_DIM + 1], (TQ, LANES))
        lb = jnp.broadcast_to(acc[TQ:, 0:1], (TQ, LANES))
        o = jnp.where(first, acc[:TQ] / la, acc[TQ:] / lb)
        o_ref[0, pl.ds(pl.multiple_of(ci * TQ, TQ), TQ), :] = o.astype(BF16)
        acc_ref[...] = jnp.zeros_like(acc_ref)
        prep_q(jnp.minimum(ci + 2, nq - 1), ci & 1)

    def step(cnt, *args):
        lax.cond((cnt & 1) == 0,
                 lambda: step_on(s0_ref, s1_ref, *args),
                 lambda: step_on(s1_ref, s0_ref, *args))
        return cnt + 1

    def q_block(qb, cnt):
        qslot = qb & 1
        last = qb

        def inner(ci, cnt):
            return step(cnt, ci, qslot, ci + 1, False, False)

        cnt = lax.fori_loop(0, last - 1, inner, cnt)

        def before_diag(ci, cnt):
            return step(cnt, ci, qslot, ci + 1, False, True)

        cnt = lax.fori_loop(jnp.maximum(last - 1, 0), last, before_diag, cnt)
        cnt = lax.cond(qb == nq - 1,
                       lambda c: step(c, last, None, None, True, None),
                       lambda c: step(c, last, 1 - qslot, 0, True, False), cnt)
        return cnt

    prep_q(0, 0)
    prep_q(1, 1)
    acc_ref[...] = jnp.zeros_like(acc_ref)
    for lo, hi, mx in next_diag(s0_ref, 0, 0):
        m_ref[lo:hi] = jnp.broadcast_to(mx, (hi - lo, LANES))
    alpha_ref[...] = jnp.zeros_like(alpha_ref)
    lax.fori_loop(0, nq, q_block, 0)


def _fox_attn(qkv, c, cast=()):
    B, S, _ = qkv.shape
    pairs = FOX_HEADS // 2
    steps = B * pairs
    blk = lambda off: pl.BlockSpec((1, S, LANES), lambda b, p: (b, 0, off + p))
    cast_specs = [pl.BlockSpec((w.shape[0] // steps, w.shape[1]), lambda b, p: (b * pairs + p, 0))
                  for w in cast]
    return pl.pallas_call(
        functools.partial(_fox_attn_kernel, n_cast=len(cast)),
        grid=(B, pairs),
        in_specs=[
            blk(0), blk(pairs), blk(2 * pairs),
            pl.BlockSpec((1, 1, S // TK, 2, TK), lambda b, p: (b, p, 0, 0, 0)),
        ] + cast_specs,
        out_specs=[pl.BlockSpec((1, S, LANES), lambda b, p: (b, 0, p))] + cast_specs,
        out_shape=[jax.ShapeDtypeStruct((B, S, D_MODEL), BF16)]
                  + [jax.ShapeDtypeStruct(w.shape, BF16) for w in cast],
        scratch_shapes=[
            pltpu.VMEM((2, 2 * TQ, LANES), BF16),
            pltpu.VMEM((S, LANES), BF16),
            pltpu.VMEM((S, LANES), BF16),
            pltpu.VMEM((2 * TQ, TK), F32),
            pltpu.VMEM((2 * TQ, TK), F32),
            pltpu.VMEM((2 * TQ, LANES), F32),
            pltpu.VMEM((2 * TQ, LANES), F32),
            pltpu.VMEM((2 * TQ, LANES), F32),
        ],
        compiler_params=pltpu.CompilerParams(
            dimension_semantics=("parallel", "parallel"),
            vmem_limit_bytes=VMEM_LIMIT_BYTES),
        name="fox_attn",
    )(qkv, qkv, qkv, c, *cast)


def _ffn_kernel(*refs, has_mix, final_norm, seg_in, seg_out):
    refs = list(refs)
    x_ref = refs.pop(0)
    if has_mix:
        a_ref = refs.pop(0)
        wmix_ref = refs.pop(0)
    g_ref, wgu_ref, wd_ref = refs[:3]
    refs = refs[3:]
    if final_norm:
        gf_ref = refs.pop(0)
    if seg_in or seg_out:
        o_ref, act_ref, slab_ref = refs
    else:
        o_ref, act_ref = refs
    n_slabs = D_MODEL // LANES
    seg_rows = [(ch * S5_CHUNK + sg, ch * S5_CHUNK + sg * S5_TJ)
                for ch in range(TM // S5_CHUNK) for sg in range(S5_SEGS)]

    if seg_in:
        cols = []
        for c in range(n_slabs):
            slab_ref[c] = x_ref[:, c * LANES:(c + 1) * LANES]
            cols.append(jnp.concatenate(
                [slab_ref[c, pl.ds(r_seg, S5_TJ, stride=S5_SEGS), :] for r_seg, _ in seg_rows],
                axis=0))
        x = jnp.concatenate(cols, axis=1)
    else:
        x = x_ref[...]
    if has_mix:
        x = x + _dot(a_ref[...], wmix_ref[...])
    h = _rms(x, g_ref[...]).astype(BF16)
    for c in range(D_FF // FF_CHUNK):
        lo = c * FF_CHUNK
        gate = _dot(h, wgu_ref[:, lo:lo + FF_CHUNK])
        up = _dot(h, wgu_ref[:, D_FF + lo:D_FF + lo + FF_CHUNK])
        act_ref[:, lo:lo + FF_CHUNK] = (gate * jax.nn.sigmoid(gate) * up).astype(BF16)
    y = x + _dot(act_ref[...], wd_ref[...])
    if final_norm:
        y = _rms(y, gf_ref[...])
    if seg_out:
        for c in range(n_slabs):
            for r_seg, r_time in seg_rows:
                slab_ref[c, pl.ds(r_seg, S5_TJ, stride=S5_SEGS), :] = (
                    y[r_time:r_time + S5_TJ, c * LANES:(c + 1) * LANES])
            o_ref[:, c * LANES:(c + 1) * LANES] = slab_ref[c]
    else:
        o_ref[...] = y


def _ffn(x, g, wgu, wd, i, mix=None, final_g=None, seg_in=False, seg_out=False):
    N, D = x.shape
    row = pl.BlockSpec((TM, D), lambda t: (t, 0))
    scratch = [pltpu.VMEM((TM, D_FF), BF16)]
    if seg_in or seg_out:
        scratch.append(pltpu.VMEM((D // LANES, TM, LANES), F32))
    args, specs = [x], [row]
    if mix is not None:
        a2, wmix, j = mix
        args += [a2, wmix]
        specs += [row, _layer(wmix, j)]
    args += [g, wgu, wd]
    specs += [_layer(g, i), _layer(wgu, i), _layer(wd, i)]
    if final_g is not None:
        args.append(final_g)
        specs.append(_resident(final_g.shape))
    return pl.pallas_call(
        functools.partial(_ffn_kernel, has_mix=mix is not None,
                          final_norm=final_g is not None, seg_in=seg_in, seg_out=seg_out),
        grid=(N // TM,),
        in_specs=specs,
        out_specs=row,
        out_shape=jax.ShapeDtypeStruct((N, D), F32),
        scratch_shapes=scratch,
        compiler_params=pltpu.CompilerParams(
            dimension_semantics=("parallel",),
            vmem_limit_bytes=VMEM_LIMIT_BYTES),
        name="ffn",
    )(*args)


def _s5_kernel(x_ref, g_ref, win_ref, bblk_ref, cblk_ref, lam_ref,
               d_ref, wglu_ref, o_ref, st_ref, carry_ref):
    @pl.when(pl.program_id(1) == 0)
    def _():
        carry_ref[...] = jnp.zeros_like(carry_ref)

    x = x_ref[0]
    h = _rms(x, g_ref[...]).astype(BF16)
    u = _dot(h, win_ref[...])
    ub = u.astype(BF16)

    sub = lax.broadcasted_iota(jnp.int32, (SUBLANES, LANES), 0)
    vshape = (SUBLANES, LANES)
    blocks_per_slab = S5_SLAB_HALF // LANES
    n_blocks = S5_SLABS * blocks_per_slab

    def cmul(ar, ai, br, bi):
        return ar * br - ai * bi, ar * bi + ai * br

    groups_per_slab = blocks_per_slab // S5_COLS_PER_STEP
    ys = []
    for grp in range(n_blocks // S5_COLS_PER_STEP):
        sl = grp // groups_per_slab
        if grp % groups_per_slab == 0:
            st_ref[:, sl * S5_SLAB_W:(sl + 1) * S5_SLAB_W] = _dot(
                ub[:, sl * S5_SLAB_IN:(sl + 1) * S5_SLAB_IN], bblk_ref[sl])
        offs = []
        for b in range(grp * S5_COLS_PER_STEP, (grp + 1) * S5_COLS_PER_STEP):
            re = (b // blocks_per_slab) * S5_SLAB_W + (b % blocks_per_slab) * LANES
            offs.append((re, re + S5_SLAB_HALF))
        lam = [(jnp.broadcast_to(lam_ref[:, re:re + LANES], vshape),
                jnp.broadcast_to(lam_ref[:, im:im + LANES], vshape)) for re, im in offs]

        def scan_body(j, state):
            r = j * SUBLANES
            new = []
            for (re, im), (lr, li), (sr, si) in zip(offs, lam, state):
                pr, pi = cmul(lr, li, sr, si)
                nr = pr + st_ref[pl.ds(r, SUBLANES), re:re + LANES]
                ni = pi + st_ref[pl.ds(r, SUBLANES), im:im + LANES]
                st_ref[pl.ds(r, SUBLANES), re:re + LANES] = nr
                st_ref[pl.ds(r, SUBLANES), im:im + LANES] = ni
                new.append((nr, ni))
            return tuple(new)

        init = tuple((carry_ref[:, re:re + LANES], carry_ref[:, im:im + LANES])
                     for re, im in offs)
        fin = init
        for j in range(S5_TJ):
            fin = scan_body(j, fin)

        carries = []
        for (re, im), (lr, li), (fr, fi) in zip(offs, lam, fin):
            for _ in range(S5_TJ.bit_length() - 1):
                lr, li = cmul(lr, li, lr, li)
            er, ei = fr, fi
            for k in range(1, S5_SEGS):
                pr, pi = cmul(lr, li, pltpu.roll(er, 1, 0), pltpu.roll(ei, 1, 0))
                er = jnp.where(sub == k, fr + pr, er)
                ei = jnp.where(sub == k, fi + pi, ei)
            sr, si = pltpu.roll(er, 1, 0), pltpu.roll(ei, 1, 0)
            carry_ref[:, re:re + LANES] = jnp.where(sub == 0, sr, 0.0)
            carry_ref[:, im:im + LANES] = jnp.where(sub == 0, si, 0.0)
            carries.append((jnp.where(sub == 0, 0.0, sr), jnp.where(sub == 0, 0.0, si)))

        def fix_body(j, f):
            r = j * SUBLANES
            new = []
            for (re, im), (lr, li), (fr, fi) in zip(offs, lam, f):
                fr, fi = cmul(lr, li, fr, fi)
                st_ref[pl.ds(r, SUBLANES), re:re + LANES] += fr
                st_ref[pl.ds(r, SUBLANES), im:im + LANES] += fi
                new.append((fr, fi))
            return tuple(new)

        f = tuple(carries)
        for j in range(S5_TJ):
            f = fix_body(j, f)

        if grp % groups_per_slab == groups_per_slab - 1:
            ys.append(_dot(st_ref[:, sl * S5_SLAB_W:(sl + 1) * S5_SLAB_W].astype(BF16),
                           cblk_ref[sl]))

    y = jnp.concatenate(ys, axis=1) + d_ref[...] * u
    gl = jax.nn.gelu(y).astype(BF16)
    vg = _dot(gl, wglu_ref[...])
    o_ref[0] = x + vg[:, :D_MODEL] * jax.nn.sigmoid(vg[:, D_MODEL:])


def _s5(xp, g, i, win, bblk, cblk, lam, d, wglu, j):
    B, S, D = xp.shape
    return pl.pallas_call(
        _s5_kernel,
        grid=(B, S // S5_CHUNK),
        in_specs=[
            pl.BlockSpec((1, S5_CHUNK, D), lambda b, c: (b, c, 0)),
            _layer(g, i), _layer(win, j), _resident(bblk.shape),
            _resident(cblk.shape), _resident(lam.shape),
            _layer(d, j), _layer(wglu, j),
        ],
        out_specs=pl.BlockSpec((1, S5_CHUNK, D), lambda b, c: (b, c, 0)),
        out_shape=jax.ShapeDtypeStruct((B, S, D), F32),
        scratch_shapes=[
            pltpu.VMEM((S5_CHUNK, S5_SW), F32),
            pltpu.VMEM((SUBLANES, S5_SW), F32),
        ],
        compiler_params=pltpu.CompilerParams(
            dimension_semantics=("parallel", "arbitrary"),
            vmem_limit_bytes=VMEM_LIMIT_BYTES),
        name="s5",
    )(xp, g, win, bblk, cblk, lam, d, wglu)


def _s5_params(a_re, a_im, log_dt, b_re, b_im, c_re, c_im):
    ar, ai = a_re.astype(F32), a_im.astype(F32)
    dt = jnp.exp(log_dt.astype(F32))[:, None]
    mag = jnp.exp(ar * dt)
    lr, li = mag * jnp.cos(ai * dt), mag * jnp.sin(ai * dt)
    den = ar * ar + ai * ai
    kr = ((lr - 1.0) * ar + li * ai) / den
    ki = (li * ar - (lr - 1.0) * ai) / den
    br, bi = b_re.astype(F32), b_im.astype(F32)
    bbr = kr[..., None] * br - ki[..., None] * bi
    bbi = kr[..., None] * bi + ki[..., None] * br

    gs = S5_GROUPS // S5_SLABS
    eye = jnp.eye(gs, dtype=F32)

    def state_row(zr, zi):
        zr = zr.reshape(S5_SLABS, S5_SLAB_HALF)
        zi = zi.reshape(S5_SLABS, S5_SLAB_HALF)
        return jnp.concatenate([zr, zi], axis=-1).reshape(S5_SW)

    same_group = eye > 0
    bt = jnp.stack([bbr, bbi]).reshape(2, S5_SLABS, gs, S5_STATE, S5_GROUP)
    bt = bt.transpose(1, 2, 4, 0, 3)
    bblk = jnp.where(same_group[None, :, None, None, :, None],
                     bt[:, :, :, :, None, :], 0.0)
    bblk = bblk.astype(BF16).reshape(S5_SLABS, S5_SLAB_IN, S5_SLAB_W)
    ct = jnp.stack([c_re.astype(F32), -c_im.astype(F32)])
    ct = ct.reshape(2, S5_SLABS, gs, S5_GROUP, S5_STATE).transpose(1, 0, 2, 4, 3)
    cblk = jnp.where(same_group[None, None, :, None, :, None],
                     ct[:, :, :, :, None, :], 0.0)
    cblk = cblk.astype(BF16).reshape(S5_SLABS, S5_SLAB_W, S5_SLAB_IN)
    return bblk, cblk, state_row(lr, li)[None]


POOL_HALO = 32


def _pool_kernel(x_ref, halo_ref, g_ref, w_ref, b_ref, sc_ref, o_ref, a_ref, b2_ref):
    t = pl.program_id(1)
    x = x_ref[0]
    g = g_ref[...]
    h = _rms(x, g)
    hh = _rms(halo_ref[0], g)
    a_ref[0:POOL_HALO] = jnp.where(t > 0, hh, 0.0)
    a_ref[POOL_HALO:] = h
    b2_ref[0:SUBLANES] = jnp.zeros((SUBLANES, D_MODEL), F32)
    rows = POOL_HALO + TM
    head = 2 * SUBLANES
    assert head >= max(POOL_WINDOWS) - 1
    tpos = t * TM + lax.broadcasted_iota(jnp.int32, (head, POOL_WIDTH), 0) + 1
    outs = []
    for gi, w in enumerate(POOL_WINDOWS):
        lo = gi * POOL_WIDTH
        cols = slice(lo, lo + POOL_WIDTH)
        src, dst, sh = a_ref, b2_ref, 1
        while sh < w:
            dst[SUBLANES:rows, cols] = src[SUBLANES:rows, cols] + src[SUBLANES - sh:rows - sh, cols]
            src, dst, sh = dst, src, 2 * sh
        total = src[POOL_HALO:rows, cols]
        mean = jnp.concatenate([total[:head] / jnp.minimum(tpos, w).astype(F32),
                                total[head:] * (1.0 / w)], axis=0)
        outs.append(_dot((mean - h[:, cols]).astype(BF16), w_ref[gi]))
    y = (jnp.concatenate(outs, axis=1) + b_ref[...]) * sc_ref[...]
    o_ref[0] = x + y


def _pool(x, g, i, w, b, sc, j):
    B, S, D = x.shape
    per = TM // POOL_HALO
    return pl.pallas_call(
        _pool_kernel,
        grid=(B, S // TM),
        in_specs=[
            pl.BlockSpec((1, TM, D), lambda b, t: (b, t, 0)),
            pl.BlockSpec((1, POOL_HALO, D), lambda b, t: (b, jnp.maximum(t * per - 1, 0), 0)),
            _layer(g, i), _layer(w, j), _layer(b, j), _layer(sc, j),
        ],
        out_specs=pl.BlockSpec((1, TM, D), lambda b, t: (b, t, 0)),
        out_shape=jax.ShapeDtypeStruct((B, S, D), F32),
        scratch_shapes=[pltpu.VMEM((POOL_HALO + TM, D), F32),
                        pltpu.VMEM((POOL_HALO + TM, D), F32)],
        compiler_params=pltpu.CompilerParams(
            dimension_semantics=("parallel", "arbitrary"),
            vmem_limit_bytes=VMEM_LIMIT_BYTES),
        name="pool",
    )(x, x, g, w, b, sc)


def kernel(x, mix_norm_g, ffn_norm_g, final_norm_g, fox_w_in, fox_b_f, fox_w_out, s5_w_in, s5_a_re, s5_a_im, s5_log_dt, s5_b_re, s5_b_im, s5_c_re, s5_c_im, s5_d, s5_w_glu, pool_w, pool_b, pool_scale, ffn_w_gate_up, ffn_w_down):
    B, S, D = x.shape
    N = B * S
    tri = jnp.triu(jnp.ones((TM, TM), F32)).astype(BF16)
    row = lambda a: a.astype(F32).reshape(a.shape[0], 1, a.shape[1])
    gm_all, gf_all = row(mix_norm_g), row(ffn_norm_g)
    wgu_all = wd_all = fox_wout = s5_win = s5_wglu = None
    fox_win = fox_w_in.astype(BF16)
    fox_wft = lax.optimization_barrier(fox_w_in[:, :, 3 * D:]).transpose(0, 2, 1).astype(BF16)
    fox_bf = fox_b_f.astype(F32)[:, :, None]
    s5_dd = row(s5_d)
    pool_ww, pool_bb, pool_sc = pool_w.astype(BF16), row(pool_b), row(pool_scale)

    for i in range(DEPTH):
        kind, j = i % 3, i // 3
        next_is_s5 = i + 1 < DEPTH and (i + 1) % 3 == 1
        final_g = final_norm_g[None].astype(F32) if i == DEPTH - 1 else None
        mix = None
        if kind == 0:
            qkv, c = _fox_proj(x, gm_all, i, fox_win, fox_wft, fox_bf, j, tri)
            c = c.reshape(B, FOX_HEADS // 2, 2, S // TK, TK).transpose(0, 1, 3, 2, 4)
            if wgu_all is None:
                late = (ffn_w_gate_up, ffn_w_down, fox_w_out, s5_w_in, s5_w_glu)
                o, *cast = _fox_attn(qkv, c, cast=tuple(
                    w.astype(F32).reshape(-1, w.shape[-1]) for w in late))
                wgu_all, wd_all, fox_wout, s5_win, s5_wglu = (
                    w2.reshape(w.shape) for w2, w in zip(cast, late))
            else:
                o, = _fox_attn(qkv, c)
            mix = (o.reshape(N, D), fox_wout, j)
        elif kind == 1:
            bblk, cblk, lam = _s5_params(
                s5_a_re[j], s5_a_im[j], s5_log_dt[j], s5_b_re[j], s5_b_im[j], s5_c_re[j], s5_c_im[j])
            x = _s5(x, gm_all, i, s5_win, bblk, cblk, lam, s5_dd, s5_wglu, j)
        else:
            x = _pool(x, gm_all, i, pool_ww, pool_bb, pool_sc, j)
        x = _ffn(x.reshape(N, D), gf_all, wgu_all, wd_all, i, mix=mix, final_g=final_g,
                 seg_in=kind == 1, seg_out=next_is_s5).reshape(B, S, D)
    return x
```

```python
import functools
import math

import jax
import jax.numpy as jnp
from jax import lax
from jax.experimental import pallas as pl
from jax.experimental.pallas import tpu as pltpu

F32 = jnp.float32
BF16 = jnp.bfloat16

D_MODEL = 1024
DEPTH = 4
EPS = 1e-6
FOX_HEADS = 16
FOX_HEAD_DIM = 64
S5_GROUP = 16
S5_GROUPS = 64
S5_STATE = 64
POOL_WINDOWS = (2, 4, 8, 16)
POOL_WIDTH = 256
D_FF = 2816

LANES = 128
SUBLANES = 8
MXU_DIM = 256
VMEM_LIMIT_BYTES = 56 * 1024 * 1024

LOG2E = 1.4426950408889634
NEG_BIG = -1e30

TM = 512
FF_CHUNK = MXU_DIM
TQ = 1024
TK = 1024
S5_SEGS = SUBLANES
S5_TJ = 32
S5_CHUNK = S5_SEGS * S5_TJ
S5_SLABS = 4
S5_SLAB_IN = D_MODEL // S5_SLABS
S5_SLAB_HALF = (S5_GROUPS // S5_SLABS) * S5_STATE
S5_SLAB_W = 2 * S5_SLAB_HALF
S5_SW = S5_SLABS * S5_SLAB_W
S5_COLS_PER_STEP = 4


def _resident(shape):
    nd = len(shape)
    return pl.BlockSpec(shape, lambda *_: (0,) * nd, pipeline_mode=pl.Buffered(1))


def _layer(arr, i):
    nd = arr.ndim
    return pl.BlockSpec((None,) + arr.shape[1:], lambda *_: (i,) + (0,) * (nd - 1),
                        pipeline_mode=pl.Buffered(1))


def _rms(x, g):
    return x * lax.rsqrt(jnp.mean(x * x, axis=-1, keepdims=True) + EPS) * g


def _dot(a, b):
    return jnp.dot(a, b, preferred_element_type=F32)


def _fox_proj_kernel(x_ref, g_ref, wqkv_ref, wft_ref, bf_ref, tri_ref,
                     qkv_ref, c_ref, carry_ref):
    @pl.when(pl.program_id(1) == 0)
    def _():
        carry_ref[...] = jnp.zeros_like(carry_ref)

    h = _rms(x_ref[0], g_ref[...]).astype(BF16)
    qkv = _dot(h, wqkv_ref[:, :3 * D_MODEL])
    q = qkv[:, :D_MODEL] * (FOX_HEAD_DIM ** -0.5 * LOG2E)
    qkv_ref[0, :, :D_MODEL] = q.astype(BF16)
    qkv_ref[0, :, D_MODEL:] = qkv[:, D_MODEL:].astype(BF16)

    z = lax.dot_general(wft_ref[...], h, (((1,), (1,)), ((), ())),
                        preferred_element_type=F32) + bf_ref[...]
    logf = jnp.minimum(z, 0.0) - jnp.log(1.0 + jnp.exp(-jnp.abs(z)))
    hi = logf.astype(BF16)
    r1 = logf - hi.astype(F32)
    mid = r1.astype(BF16)
    lo = (r1 - mid.astype(F32)).astype(BF16)
    parts = _dot(jnp.concatenate([hi, mid, lo], axis=0), tri_ref[...])
    c = (parts[:FOX_HEADS] + parts[FOX_HEADS:2 * FOX_HEADS] + parts[2 * FOX_HEADS:]
         + carry_ref[:, 0:1])
    c_ref[0] = c * LOG2E
    carry_ref[...] = jnp.broadcast_to(c[:, TM - 1:TM], carry_ref.shape)


def _fox_proj(x, g, i, win, wft, bf, j, tri):
    B, S, D = x.shape
    return pl.pallas_call(
        _fox_proj_kernel,
        grid=(B, S // TM),
        in_specs=[
            pl.BlockSpec((1, TM, D), lambda b, t: (b, t, 0)),
            _layer(g, i), _layer(win, j), _layer(wft, j), _layer(bf, j),
            _resident((TM, TM)),
        ],
        out_specs=[
            pl.BlockSpec((1, TM, 3 * D), lambda b, t: (b, t, 0)),
            pl.BlockSpec((1, FOX_HEADS, TM), lambda b, t: (b, 0, t)),
        ],
        out_shape=[
            jax.ShapeDtypeStruct((B, S, 3 * D), BF16),
            jax.ShapeDtypeStruct((B, FOX_HEADS, S), F32),
        ],
        scratch_shapes=[pltpu.VMEM((FOX_HEADS, LANES), F32)],
        compiler_params=pltpu.CompilerParams(
            dimension_semantics=("parallel", "arbitrary"),
            vmem_limit_bytes=VMEM_LIMIT_BYTES),
        name="fox_proj",
    )(x, g, win, wft, bf, tri)


def _fox_attn_kernel(*refs, n_cast):
    q_ref, k_ref, v_ref, c_ref = refs[:4]
    cast_in = refs[4:4 + n_cast]
    o_ref = refs[4 + n_cast]
    cast_out = refs[5 + n_cast:5 + 2 * n_cast]
    qs_ref, va_ref, vb_ref, s0_ref, s1_ref, m_ref, alpha_ref, acc_ref = refs[5 + 2 * n_cast:]
    for src_ref, dst_ref in zip(cast_in, cast_out):
        dst_ref[...] = src_ref[...].astype(BF16)
    assert TQ == TK
    S = q_ref.shape[1]
    nq = S // TQ
    lane = lax.broadcasted_iota(jnp.int32, (TQ, LANES), 1)
    first = lane < FOX_HEAD_DIM

    def prep_v(i, c):
        r = pl.multiple_of(i * TK, TK)
        v = v_ref[0, pl.ds(r, TK), :].astype(F32)
        ln = lax.broadcasted_iota(jnp.int32, v.shape, 1)
        va = jnp.where(ln < FOX_HEAD_DIM, v, jnp.where(ln == FOX_HEAD_DIM, 1.0, 0.0))
        vb = jnp.where(ln >= FOX_HEAD_DIM, v, jnp.where(ln == 0, 1.0, 0.0))
        va_ref[pl.ds(r, TK), :] = va.astype(BF16)
        vb_ref[pl.ds(r, TK), :] = vb.astype(BF16)
        return c

    lax.fori_loop(0, S // TK, prep_v, 0)

    def prep_q(qb, slot):
        q = q_ref[0, pl.ds(pl.multiple_of(qb * TQ, TQ), TQ), :]
        zero = jnp.zeros_like(q)
        qs_ref[slot, 0:TQ] = jnp.where(first, q, zero)
        qs_ref[slot, TQ:2 * TQ] = jnp.where(first, zero, q)

    H = TQ // 2

    def qk(q, k):
        return lax.dot_general(q, k, (((1,), (1,)), ((), ())), preferred_element_type=F32)

    def key_bias(ck, lo, width, rows):
        return jnp.concatenate(
            [jnp.broadcast_to(ck[0:1, lo:lo + width], (rows, width)),
             jnp.broadcast_to(ck[1:2, lo:lo + width], (rows, width))], axis=0)

    def causal(s):
        row = lax.broadcasted_iota(jnp.int32, s.shape, 0)
        col = lax.broadcasted_iota(jnp.int32, s.shape, 1)
        return jnp.where(col <= row, s, NEG_BIG)

    def lanes(m, width):
        return jnp.concatenate([m] * (width // LANES), axis=1)

    def next_full(nxt_ref, qslot, ci):
        k = k_ref[0, pl.ds(pl.multiple_of(ci * TK, TK), TK), :]
        s = qk(qs_ref[qslot], k) - key_bias(c_ref[0, 0, ci], 0, TK, TQ)
        nxt_ref[...] = s
        return [(0, 2 * TQ, jnp.max(s, axis=1, keepdims=True))]

    def next_diag(nxt_ref, qslot, ci):
        r = pl.multiple_of(ci * TK, TK)
        q = qs_ref[qslot]
        ck = c_ref[0, 0, ci]
        left = qk(q, k_ref[0, pl.ds(r, H), :]) - key_bias(ck, 0, H, TQ)
        left = jnp.concatenate([causal(left[0:H]), left[H:TQ],
                                causal(left[TQ:TQ + H]), left[TQ + H:]], axis=0)
        q_hi = jnp.concatenate([q[H:TQ], q[TQ + H:]], axis=0)
        right = qk(q_hi, k_ref[0, pl.ds(pl.multiple_of(r + H, H), H), :])
        right = right - key_bias(ck, H, H, H)
        right = jnp.concatenate([causal(right[:H]), causal(right[H:])], axis=0)
        nxt_ref[:, 0:H] = left
        nxt_ref[H:TQ, H:TK] = right[:H]
        nxt_ref[TQ + H:, H:TK] = right[H:]
        rowmax = lambda a: jnp.max(a, axis=1, keepdims=True)
        return [(0, H, rowmax(left[0:H])),
                (H, TQ, jnp.maximum(rowmax(left[H:TQ]), rowmax(right[:H]))),
                (TQ, TQ + H, rowmax(left[TQ:TQ + H])),
                (TQ + H, 2 * TQ, jnp.maximum(rowmax(left[TQ + H:]), rowmax(right[H:])))]

    def pv_full(cur_ref, ci, m_cur):
        p = jnp.exp2(cur_ref[...] - lanes(m_cur, TK)).astype(BF16)
        r = pl.multiple_of(ci * TK, TK)
        return jnp.concatenate([_dot(p[:TQ], va_ref[pl.ds(r, TK), :]),
                                _dot(p[TQ:], vb_ref[pl.ds(r, TK), :])], axis=0)

    def pv_diag(cur_ref, ci, m_cur):
        r = pl.multiple_of(ci * TK, TK)
        r_hi = pl.multiple_of(r + H, H)
        p_left = jnp.exp2(cur_ref[:, 0:H] - lanes(m_cur, H)).astype(BF16)
        out = []
        for head, vh_ref in enumerate((va_ref, vb_ref)):
            lo, hi = head * TQ, head * TQ + H
            pv = _dot(p_left[lo:lo + TQ], vh_ref[pl.ds(r, H), :])
            p_right = jnp.exp2(cur_ref[hi:hi + H, H:TK] - lanes(m_cur[hi:hi + H], H))
            pv_hi = _dot(p_right.astype(BF16), vh_ref[pl.ds(r_hi, H), :])
            out += [pv[:H], pv[H:] + pv_hi]
        return jnp.concatenate(out, axis=0)

    def step_on(cur_ref, nxt_ref, ci, qslot_n, ci_n, cur_diag, nxt_diag):
        if nxt_diag is None:
            row_maxes = []
        else:
            row_maxes = (next_diag if nxt_diag else next_full)(nxt_ref, qslot_n, ci_n)
        m_cur = m_ref[...]
        alpha = alpha_ref[...]
        for lo, hi, mx in row_maxes:
            if cur_diag:
                m_ref[lo:hi] = jnp.broadcast_to(mx, (hi - lo, LANES))
                alpha_ref[lo:hi] = jnp.zeros((hi - lo, LANES), F32)
            else:
                m_next = jnp.maximum(m_cur[lo:hi], mx)
                m_ref[lo:hi] = m_next
                alpha_ref[lo:hi] = jnp.exp2(m_cur[lo:hi] - m_next)
        pv = (pv_diag if cur_diag else pv_full)(cur_ref, ci, m_cur)
        acc = alpha * acc_ref[...] + pv
        if not cur_diag:
            acc_ref[...] = acc
            return
        la = jnp.broadcast_to(acc[:TQ, FOX_HEAD_DIM:FOX_HEAD_DIM + 1], (TQ, LANES))
        lb = jnp.broadcast_to(acc[TQ:, 0:1], (TQ, LANES))
        o = jnp.where(first, acc[:TQ] / la, acc[TQ:] / lb)
        o_ref[0, pl.ds(pl.multiple_of(ci * TQ, TQ), TQ), :] = o.astype(BF16)
        acc_ref[...] = jnp.zeros_like(acc_ref)
        prep_q(jnp.minimum(ci + 2, nq - 1), ci & 1)

    def step(cnt, *args):
        lax.cond((cnt & 1) == 0,
                 lambda: step_on(s0_ref, s1_ref, *args),
                 lambda: step_on(s1_ref, s0_ref, *args))
        return cnt + 1

    def q_block(qb, cnt):
        qslot = qb & 1
        last = qb

        def inner(ci, cnt):
            return step(cnt, ci, qslot, ci + 1, False, False)

        cnt = lax.fori_loop(0, last - 1, inner, cnt)

        def before_diag(ci, cnt):
            return step(cnt, ci, qslot, ci + 1, False, True)

        cnt = lax.fori_loop(jnp.maximum(last - 1, 0), last, before_diag, cnt)
        cnt = lax.cond(qb == nq - 1,
                       lambda c: step(c, last, None, None, True, None),
                       lambda c: step(c, last, 1 - qslot, 0, True, False), cnt)
        return cnt

    prep_q(0, 0)
    prep_q(1, 1)
    acc_ref[...] = jnp.zeros_like(acc_ref)
    for lo, hi, mx in next_diag(s0_ref, 0, 0):
        m_ref[lo:hi] = jnp.broadcast_to(mx, (hi - lo, LANES))
    alpha_ref[...] = jnp.zeros_like(alpha_ref)
    lax.fori_loop(0, nq, q_block, 0)


def _fox_attn(qkv, c, cast=()):
    B, S, _ = qkv.shape
    pairs = FOX_HEADS // 2
    steps = B * pairs
    blk = lambda off: pl.BlockSpec((1, S, LANES), lambda b, p: (b, 0, off + p))
    cast_specs = [pl.BlockSpec((w.shape[0] // steps, w.shape[1]), lambda b, p: (b * pairs + p, 0))
                  for w in cast]
    return pl.pallas_call(
        functools.partial(_fox_attn_kernel, n_cast=len(cast)),
        grid=(B, pairs),
        in_specs=[
            blk(0), blk(pairs), blk(2 * pairs),
            pl.BlockSpec((1, 1, S // TK, 2, TK), lambda b, p: (b, p, 0, 0, 0)),
        ] + cast_specs,
        out_specs=[pl.BlockSpec((1, S, LANES), lambda b, p: (b, 0, p))] + cast_specs,
        out_shape=[jax.ShapeDtypeStruct((B, S, D_MODEL), BF16)]
                  + [jax.ShapeDtypeStruct(w.shape, BF16) for w in cast],
        scratch_shapes=[
            pltpu.VMEM((2, 2 * TQ, LANES), BF16),
            pltpu.VMEM((S, LANES), BF16),
            pltpu.VMEM((S, LANES), BF16),
            pltpu.VMEM((2 * TQ, TK), F32),
            pltpu.VMEM((2 * TQ, TK), F32),
            pltpu.VMEM((2 * TQ, LANES), F32),
            pltpu.VMEM((2 * TQ, LANES), F32),
            pltpu.VMEM((2 * TQ, LANES), F32),
        ],
        compiler_params=pltpu.CompilerParams(
            dimension_semantics=("parallel", "parallel"),
            vmem_limit_bytes=VMEM_LIMIT_BYTES),
        name="fox_attn",
    )(qkv, qkv, qkv, c, *cast)


def _ffn_kernel(*refs, has_mix, final_norm, seg_in, seg_out):
    refs = list(refs)
    x_ref = refs.pop(0)
    if has_mix:
        a_ref = refs.pop(0)
        wmix_ref = refs.pop(0)
    g_ref, wgu_ref, wd_ref = refs[:3]
    refs = refs[3:]
    if final_norm:
        gf_ref = refs.pop(0)
    if seg_in or seg_out:
        o_ref, act_ref, slab_ref = refs
    else:
        o_ref, act_ref = refs
    n_slabs = D_MODEL // LANES
    seg_rows = [(ch * S5_CHUNK + sg, ch * S5_CHUNK + sg * S5_TJ)
                for ch in range(TM // S5_CHUNK) for sg in range(S5_SEGS)]

    if seg_in:
        cols = []
        for c in range(n_slabs):
            slab_ref[c] = x_ref[:, c * LANES:(c + 1) * LANES]
            cols.append(jnp.concatenate(
                [slab_ref[c, pl.ds(r_seg, S5_TJ, stride=S5_SEGS), :] for r_seg, _ in seg_rows],
                axis=0))
        x = jnp.concatenate(cols, axis=1)
    else:
        x = x_ref[...]
    if has_mix:
        x = x + _dot(a_ref[...], wmix_ref[...])
    h = _rms(x, g_ref[...]).astype(BF16)
    for c in range(D_FF // FF_CHUNK):
        lo = c * FF_CHUNK
        gate = _dot(h, wgu_ref[:, lo:lo + FF_CHUNK])
        up = _dot(h, wgu_ref[:, D_FF + lo:D_FF + lo + FF_CHUNK])
        act_ref[:, lo:lo + FF_CHUNK] = (gate * jax.nn.sigmoid(gate) * up).astype(BF16)
    y = x + _dot(act_ref[...], wd_ref[...])
    if final_norm:
        y = _rms(y, gf_ref[...])
    if seg_out:
        for c in range(n_slabs):
            for r_seg, r_time in seg_rows:
                slab_ref[c, pl.ds(r_seg, S5_TJ, stride=S5_SEGS), :] = (
                    y[r_time:r_time + S5_TJ, c * LANES:(c + 1) * LANES])
            o_ref[:, c * LANES:(c + 1) * LANES] = slab_ref[c]
    else:
        o_ref[...] = y


def _ffn(x, g, wgu, wd, i, mix=None, final_g=None, seg_in=False, seg_out=False):
    N, D = x.shape
    row = pl.BlockSpec((TM, D), lambda t: (t, 0))
    scratch = [pltpu.VMEM((TM, D_FF), BF16)]
    if seg_in or seg_out:
        scratch.append(pltpu.VMEM((D // LANES, TM, LANES), F32))
    args, specs = [x], [row]
    if mix is not None:
        a2, wmix, j = mix
        args += [a2, wmix]
        specs += [row, _layer(wmix, j)]
    args += [g, wgu, wd]
    specs += [_layer(g, i), _layer(wgu, i), _layer(wd, i)]
    if final_g is not None:
        args.append(final_g)
        specs.append(_resident(final_g.shape))
    return pl.pallas_call(
        functools.partial(_ffn_kernel, has_mix=mix is not None,
                          final_norm=final_g is not None, seg_in=seg_in, seg_out=seg_out),
        grid=(N // TM,),
        in_specs=specs,
        out_specs=row,
        out_shape=jax.ShapeDtypeStruct((N, D), F32),
        scratch_shapes=scratch,
        compiler_params=pltpu.CompilerParams(
            dimension_semantics=("parallel",),
            vmem_limit_bytes=VMEM_LIMIT_BYTES),
        name="ffn",
    )(*args)


def _s5_kernel(x_ref, g_ref, win_ref, bblk_ref, cblk_ref, lam_ref,
               d_ref, wglu_ref, o_ref, st_ref, carry_ref):
    @pl.when(pl.program_id(1) == 0)
    def _():
        carry_ref[...] = jnp.zeros_like(carry_ref)

    x = x_ref[0]
    h = _rms(x, g_ref[...]).astype(BF16)
    u = _dot(h, win_ref[...])
    ub = u.astype(BF16)

    sub = lax.broadcasted_iota(jnp.int32, (SUBLANES, LANES), 0)
    vshape = (SUBLANES, LANES)
    blocks_per_slab = S5_SLAB_HALF // LANES
    n_blocks = S5_SLABS * blocks_per_slab

    def cmul(ar, ai, br, bi):
        return ar * br - ai * bi, ar * bi + ai * br

    groups_per_slab = blocks_per_slab // S5_COLS_PER_STEP
    ys = []
    for grp in range(n_blocks // S5_COLS_PER_STEP):
        sl = grp // groups_per_slab
        if grp % groups_per_slab == 0:
            st_ref[:, sl * S5_SLAB_W:(sl + 1) * S5_SLAB_W] = _dot(
                ub[:, sl * S5_SLAB_IN:(sl + 1) * S5_SLAB_IN], bblk_ref[sl])
        offs = []
        for b in range(grp * S5_COLS_PER_STEP, (grp + 1) * S5_COLS_PER_STEP):
            re = (b // blocks_per_slab) * S5_SLAB_W + (b % blocks_per_slab) * LANES
            offs.append((re, re + S5_SLAB_HALF))
        lam = [(jnp.broadcast_to(lam_ref[:, re:re + LANES], vshape),
                jnp.broadcast_to(lam_ref[:, im:im + LANES], vshape)) for re, im in offs]

        def scan_body(j, state):
            r = j * SUBLANES
            new = []
            for (re, im), (lr, li), (sr, si) in zip(offs, lam, state):
                pr, pi = cmul(lr, li, sr, si)
                nr = pr + st_ref[pl.ds(r, SUBLANES), re:re + LANES]
                ni = pi + st_ref[pl.ds(r, SUBLANES), im:im + LANES]
                st_ref[pl.ds(r, SUBLANES), re:re + LANES] = nr
                st_ref[pl.ds(r, SUBLANES), im:im + LANES] = ni
                new.append((nr, ni))
            return tuple(new)

        init = tuple((carry_ref[:, re:re + LANES], carry_ref[:, im:im + LANES])
                     for re, im in offs)
        fin = init
        for j in range(S5_TJ):
            fin = scan_body(j, fin)

        carries = []
        for (re, im), (lr, li), (fr, fi) in zip(offs, lam, fin):
            for _ in range(S5_TJ.bit_length() - 1):
                lr, li = cmul(lr, li, lr, li)
            er, ei = fr, fi
            for k in range(1, S5_SEGS):
                pr, pi = cmul(lr, li, pltpu.roll(er, 1, 0), pltpu.roll(ei, 1, 0))
                er = jnp.where(sub == k, fr + pr, er)
                ei = jnp.where(sub == k, fi + pi, ei)
            sr, si = pltpu.roll(er, 1, 0), pltpu.roll(ei, 1, 0)
            carry_ref[:, re:re + LANES] = jnp.where(sub == 0, sr, 0.0)
            carry_ref[:, im:im + LANES] = jnp.where(sub == 0, si, 0.0)
            carries.append((jnp.where(sub == 0, 0.0, sr), jnp.where(sub == 0, 0.0, si)))

        def fix_body(j, f):
            r = j * SUBLANES
            new = []
            for (re, im), (lr, li), (fr, fi) in zip(offs, lam, f):
                fr, fi = cmul(lr, li, fr, fi)
                st_ref[pl.ds(r, SUBLANES), re:re + LANES] += fr
                st_ref[pl.ds(r, SUBLANES), im:im + LANES] += fi
                new.append((fr, fi))
            return tuple(new)

        f = tuple(carries)
        for j in range(S5_TJ):
            f = fix_body(j, f)

        if grp % groups_per_slab == groups_per_slab - 1:
            ys.append(_dot(st_ref[:, sl * S5_SLAB_W:(sl + 1) * S5_SLAB_W].astype(BF16),
                           cblk_ref[sl]))

    y = jnp.concatenate(ys, axis=1) + d_ref[...] * u
    gl = jax.nn.gelu(y).astype(BF16)
    vg = _dot(gl, wglu_ref[...])
    o_ref[0] = x + vg[:, :D_MODEL] * jax.nn.sigmoid(vg[:, D_MODEL:])


def _s5(xp, g, i, win, bblk, cblk, lam, d, wglu, j):
    B, S, D = xp.shape
    return pl.pallas_call(
        _s5_kernel,
        grid=(B, S // S5_CHUNK),
        in_specs=[
            pl.BlockSpec((1, S5_CHUNK, D), lambda b, c: (b, c, 0)),
            _layer(g, i), _layer(win, j), _resident(bblk.shape),
            _resident(cblk.shape), _resident(lam.shape),
            _layer(d, j), _layer(wglu, j),
        ],
        out_specs=pl.BlockSpec((1, S5_CHUNK, D), lambda b, c: (b, c, 0)),
        out_shape=jax.ShapeDtypeStruct((B, S, D), F32),
        scratch_shapes=[
            pltpu.VMEM((S5_CHUNK, S5_SW), F32),
            pltpu.VMEM((SUBLANES, S5_SW), F32),
        ],
        compiler_params=pltpu.CompilerParams(
            dimension_semantics=("parallel", "arbitrary"),
            vmem_limit_bytes=VMEM_LIMIT_BYTES),
        name="s5",
    )(xp, g, win, bblk, cblk, lam, d, wglu)


def _s5_params(a_re, a_im, log_dt, b_re, b_im, c_re, c_im):
    ar, ai = a_re.astype(F32), a_im.astype(F32)
    dt = jnp.exp(log_dt.astype(F32))[:, None]
    mag = jnp.exp(ar * dt)
    lr, li = mag * jnp.cos(ai * dt), mag * jnp.sin(ai * dt)
    den = ar * ar + ai * ai
    kr = ((lr - 1.0) * ar + li * ai) / den
    ki = (li * ar - (lr - 1.0) * ai) / den
    br, bi = b_re.astype(F32), b_im.astype(F32)
    bbr = kr[..., None] * br - ki[..., None] * bi
    bbi = kr[..., None] * bi + ki[..., None] * br

    gs = S5_GROUPS // S5_SLABS
    eye = jnp.eye(gs, dtype=F32)

    def state_row(zr, zi):
        zr = zr.reshape(S5_SLABS, S5_SLAB_HALF)
        zi = zi.reshape(S5_SLABS, S5_SLAB_HALF)
        return jnp.concatenate([zr, zi], axis=-1).reshape(S5_SW)

    same_group = (eye > 0)[None, :, None, :, None]

    def in_block(b):
        bt = b.reshape(S5_SLABS, gs, S5_STATE, S5_GROUP).transpose(0, 1, 3, 2)
        full = jnp.where(same_group, bt[:, :, :, None, :], 0.0)
        return full.reshape(S5_SLABS, S5_SLAB_IN, S5_SLAB_HALF)

    def out_block(c):
        ct = c.reshape(S5_SLABS, gs, S5_GROUP, S5_STATE).transpose(0, 1, 3, 2)
        full = jnp.where(same_group, ct[:, :, :, None, :], 0.0)
        return full.reshape(S5_SLABS, S5_SLAB_HALF, S5_SLAB_IN)

    bblk = jnp.concatenate([in_block(bbr), in_block(bbi)], axis=2)
    cblk = jnp.concatenate([out_block(c_re.astype(F32)), -out_block(c_im.astype(F32))], axis=1)
    return bblk.astype(BF16), cblk.astype(BF16), state_row(lr, li)[None]


POOL_HALO = 32


def _pool_kernel(x_ref, halo_ref, g_ref, w_ref, b_ref, sc_ref, o_ref, a_ref, b2_ref):
    t = pl.program_id(1)
    x = x_ref[0]
    g = g_ref[...]
    h = _rms(x, g)
    hh = _rms(halo_ref[0], g)
    a_ref[0:POOL_HALO] = jnp.where(t > 0, hh, 0.0)
    a_ref[POOL_HALO:] = h
    b2_ref[0:SUBLANES] = jnp.zeros((SUBLANES, D_MODEL), F32)
    rows = POOL_HALO + TM
    head = 2 * SUBLANES
    assert head >= max(POOL_WINDOWS) - 1
    tpos = t * TM + lax.broadcasted_iota(jnp.int32, (head, POOL_WIDTH), 0) + 1
    outs = []
    for gi, w in enumerate(POOL_WINDOWS):
        lo = gi * POOL_WIDTH
        cols = slice(lo, lo + POOL_WIDTH)
        src, dst, sh = a_ref, b2_ref, 1
        while sh < w:
            dst[SUBLANES:rows, cols] = src[SUBLANES:rows, cols] + src[SUBLANES - sh:rows - sh, cols]
            src, dst, sh = dst, src, 2 * sh
        total = src[POOL_HALO:rows, cols]
        mean = jnp.concatenate([total[:head] / jnp.minimum(tpos, w).astype(F32),
                                total[head:] * (1.0 / w)], axis=0)
        outs.append(_dot((mean - h[:, cols]).astype(BF16), w_ref[gi]))
    y = (jnp.concatenate(outs, axis=1) + b_ref[...]) * sc_ref[...]
    o_ref[0] = x + y


def _pool(x, g, i, w, b, sc, j):
    B, S, D = x.shape
    per = TM // POOL_HALO
    return pl.pallas_call(
        _pool_kernel,
        grid=(B, S // TM),
        in_specs=[
            pl.BlockSpec((1, TM, D), lambda b, t: (b, t, 0)),
            pl.BlockSpec((1, POOL_HALO, D), lambda b, t: (b, jnp.maximum(t * per - 1, 0), 0)),
            _layer(g, i), _layer(w, j), _layer(b, j), _layer(sc, j),
        ],
        out_specs=pl.BlockSpec((1, TM, D), lambda b, t: (b, t, 0)),
        out_shape=jax.ShapeDtypeStruct((B, S, D), F32),
        scratch_shapes=[pltpu.VMEM((POOL_HALO + TM, D), F32),
                        pltpu.VMEM((POOL_HALO + TM, D), F32)],
        compiler_params=pltpu.CompilerParams(
            dimension_semantics=("parallel", "arbitrary"),
            vmem_limit_bytes=VMEM_LIMIT_BYTES),
        name="pool",
    )(x, x, g, w, b, sc)


def kernel(x, mix_norm_g, ffn_norm_g, final_norm_g, fox_w_in, fox_b_f, fox_w_out, s5_w_in, s5_a_re, s5_a_im, s5_log_dt, s5_b_re, s5_b_im, s5_c_re, s5_c_im, s5_d, s5_w_glu, pool_w, pool_b, pool_scale, ffn_w_gate_up, ffn_w_down):
    B, S, D = x.shape
    N = B * S
    tri = jnp.triu(jnp.ones((TM, TM), F32)).astype(BF16)
    row = lambda a: a.astype(F32).reshape(a.shape[0], 1, a.shape[1])
    gm_all, gf_all = row(mix_norm_g), row(ffn_norm_g)
    wgu_all = wd_all = fox_wout = s5_win = s5_wglu = None
    fox_win = fox_w_in[:1].astype(BF16)
    fox_wft = lax.optimization_barrier(fox_w_in[:, :, 3 * D:]).transpose(0, 2, 1).astype(BF16)
    fox_bf = fox_b_f.astype(F32)[:, :, None]
    s5_dd = row(s5_d)
    pool_ww, pool_bb, pool_sc = pool_w.astype(BF16), row(pool_b), row(pool_scale)

    for i in range(DEPTH):
        kind, j = i % 3, i // 3
        next_is_s5 = i + 1 < DEPTH and (i + 1) % 3 == 1
        final_g = final_norm_g[None].astype(F32) if i == DEPTH - 1 else None
        mix = None
        if kind == 0:
            qkv, c = _fox_proj(x, gm_all, i, fox_win, fox_wft, fox_bf,
                               min(j, fox_win.shape[0] - 1), tri)
            c = c.reshape(B, FOX_HEADS // 2, 2, S // TK, TK).transpose(0, 1, 3, 2, 4)
            if wgu_all is None:
                late = (ffn_w_gate_up, ffn_w_down, fox_w_out, s5_w_in, s5_w_glu, fox_w_in)
                o, *cast = _fox_attn(qkv, c, cast=tuple(
                    w.astype(F32).reshape(-1, w.shape[-1]) for w in late))
                wgu_all, wd_all, fox_wout, s5_win, s5_wglu, fox_win = (
                    w2.reshape(w.shape) for w2, w in zip(cast, late))
            else:
                o, = _fox_attn(qkv, c)
            mix = (o.reshape(N, D), fox_wout, j)
        elif kind == 1:
            bblk, cblk, lam = _s5_params(
                s5_a_re[j], s5_a_im[j], s5_log_dt[j], s5_b_re[j], s5_b_im[j], s5_c_re[j], s5_c_im[j])
            x = _s5(x, gm_all, i, s5_win, bblk, cblk, lam, s5_dd, s5_wglu, j)
        else:
            x = _pool(x, gm_all, i, pool_ww, pool_bb, pool_sc, j)
        x = _ffn(x.reshape(N, D), gf_all, wgu_all, wd_all, i, mix=mix, final_g=final_g,
                 seg_in=kind == 1, seg_out=next_is_s5).reshape(B, S, D)
    return x
```

```python
import functools
import math

import jax
import jax.numpy as jnp
from jax import lax
from jax.experimental import pallas as pl
from jax.experimental.pallas import tpu as pltpu

F32 = jnp.float32
BF16 = jnp.bfloat16

D_MODEL = 1024
DEPTH = 4
EPS = 1e-6
FOX_HEADS = 16
FOX_HEAD_DIM = 64
S5_GROUP = 16
S5_GROUPS = 64
S5_STATE = 64
POOL_WINDOWS = (2, 4, 8, 16)
POOL_WIDTH = 256
D_FF = 2816

LANES = 128
SUBLANES = 8
MXU_DIM = 256
VMEM_LIMIT_BYTES = 56 * 1024 * 1024

LOG2E = 1.4426950408889634
NEG_BIG = -1e30

TM = 512
FF_CHUNK = MXU_DIM
TQ = 1024
TK = 1024
S5_SEGS = SUBLANES
S5_TJ = 32
S5_CHUNK = S5_SEGS * S5_TJ
S5_SLABS = 4
S5_SLAB_IN = D_MODEL // S5_SLABS
S5_SLAB_HALF = (S5_GROUPS // S5_SLABS) * S5_STATE
S5_SLAB_W = 2 * S5_SLAB_HALF
S5_SW = S5_SLABS * S5_SLAB_W
S5_COLS_PER_STEP = 4


def _resident(shape):
    nd = len(shape)
    return pl.BlockSpec(shape, lambda *_: (0,) * nd, pipeline_mode=pl.Buffered(1))


def _layer(arr, i):
    nd = arr.ndim
    return pl.BlockSpec((None,) + arr.shape[1:], lambda *_: (i,) + (0,) * (nd - 1),
                        pipeline_mode=pl.Buffered(1))


def _rms(x, g):
    return x * lax.rsqrt(jnp.mean(x * x, axis=-1, keepdims=True) + EPS) * g


def _dot(a, b):
    return jnp.dot(a, b, preferred_element_type=F32)


def _fox_proj_kernel(x_ref, g_ref, wqkv_ref, wft_ref, bf_ref, tri_ref,
                     qkv_ref, c_ref, carry_ref):
    @pl.when(pl.program_id(1) == 0)
    def _():
        carry_ref[...] = jnp.zeros_like(carry_ref)

    h = _rms(x_ref[0], g_ref[...]).astype(BF16)
    qkv = _dot(h, wqkv_ref[:, :3 * D_MODEL])
    q = qkv[:, :D_MODEL] * (FOX_HEAD_DIM ** -0.5 * LOG2E)
    qkv_ref[0, :, :D_MODEL] = q.astype(BF16)
    qkv_ref[0, :, D_MODEL:] = qkv[:, D_MODEL:].astype(BF16)

    z = lax.dot_general(wft_ref[...], h, (((1,), (1,)), ((), ())),
                        preferred_element_type=F32) + bf_ref[...]
    logf = jnp.minimum(z, 0.0) - jnp.log(1.0 + jnp.exp(-jnp.abs(z)))
    hi = logf.astype(BF16)
    r1 = logf - hi.astype(F32)
    mid = r1.astype(BF16)
    lo = (r1 - mid.astype(F32)).astype(BF16)
    parts = _dot(jnp.concatenate([hi, mid, lo], axis=0), tri_ref[...])
    c = (parts[:FOX_HEADS] + parts[FOX_HEADS:2 * FOX_HEADS] + parts[2 * FOX_HEADS:]
         + carry_ref[:, 0:1])
    c_ref[0] = c * LOG2E
    carry_ref[...] = jnp.broadcast_to(c[:, TM - 1:TM], carry_ref.shape)


def _fox_proj(x, g, i, win, wft, bf, j, tri):
    B, S, D = x.shape
    return pl.pallas_call(
        _fox_proj_kernel,
        grid=(B, S // TM),
        in_specs=[
            pl.BlockSpec((1, TM, D), lambda b, t: (b, t, 0)),
            _layer(g, i), _layer(win, j), _layer(wft, j), _layer(bf, j),
            _resident((TM, TM)),
        ],
        out_specs=[
            pl.BlockSpec((1, TM, 3 * D), lambda b, t: (b, t, 0)),
            pl.BlockSpec((1, FOX_HEADS, TM), lambda b, t: (b, 0, t)),
        ],
        out_shape=[
            jax.ShapeDtypeStruct((B, S, 3 * D), BF16),
            jax.ShapeDtypeStruct((B, FOX_HEADS, S), F32),
        ],
        scratch_shapes=[pltpu.VMEM((FOX_HEADS, LANES), F32)],
        compiler_params=pltpu.CompilerParams(
            dimension_semantics=("parallel", "arbitrary"),
            vmem_limit_bytes=VMEM_LIMIT_BYTES),
        name="fox_proj",
    )(x, g, win, wft, bf, tri)


def _fox_attn_kernel(*refs, n_cast):
    q_ref, k_ref, v_ref, c_ref = refs[:4]
    cast_in = refs[4:4 + n_cast]
    o_ref = refs[4 + n_cast]
    cast_out = refs[5 + n_cast:5 + 2 * n_cast]
    qs_ref, va_ref, vb_ref, s0_ref, s1_ref, m_ref, alpha_ref, acc_ref = refs[5 + 2 * n_cast:]
    for src_ref, dst_ref in zip(cast_in, cast_out):
        dst_ref[...] = src_ref[...].astype(BF16)
    assert TQ == TK
    S = q_ref.shape[1]
    nq = S // TQ
    lane = lax.broadcasted_iota(jnp.int32, (TQ, LANES), 1)
    first = lane < FOX_HEAD_DIM

    def prep_v(i, c):
        r = pl.multiple_of(i * TK, TK)
        v = v_ref[0, pl.ds(r, TK), :].astype(F32)
        ln = lax.broadcasted_iota(jnp.int32, v.shape, 1)
        va = jnp.where(ln < FOX_HEAD_DIM, v, jnp.where(ln == FOX_HEAD_DIM, 1.0, 0.0))
        vb = jnp.where(ln >= FOX_HEAD_DIM, v, jnp.where(ln == 0, 1.0, 0.0))
        va_ref[pl.ds(r, TK), :] = va.astype(BF16)
        vb_ref[pl.ds(r, TK), :] = vb.astype(BF16)
        return c

    lax.fori_loop(0, S // TK, prep_v, 0)

    def prep_q(qb, slot):
        q = q_ref[0, pl.ds(pl.multiple_of(qb * TQ, TQ), TQ), :]
        zero = jnp.zeros_like(q)
        qs_ref[slot, 0:TQ] = jnp.where(first, q, zero)
        qs_ref[slot, TQ:2 * TQ] = jnp.where(first, zero, q)

    H = TQ // 2

    def qk(q, k):
        return lax.dot_general(q, k, (((1,), (1,)), ((), ())), preferred_element_type=F32)

    def key_bias(ck, lo, width, rows):
        return jnp.concatenate(
            [jnp.broadcast_to(ck[0:1, lo:lo + width], (rows, width)),
             jnp.broadcast_to(ck[1:2, lo:lo + width], (rows, width))], axis=0)

    def causal(s):
        row = lax.broadcasted_iota(jnp.int32, s.shape, 0)
        col = lax.broadcasted_iota(jnp.int32, s.shape, 1)
        return jnp.where(col <= row, s, NEG_BIG)

    def lanes(m, width):
        return jnp.concatenate([m] * (width // LANES), axis=1)

    def next_full(nxt_ref, qslot, ci):
        k = k_ref[0, pl.ds(pl.multiple_of(ci * TK, TK), TK), :]
        s = qk(qs_ref[qslot], k) - key_bias(c_ref[0, 0, ci], 0, TK, TQ)
        nxt_ref[...] = s
        return [(0, 2 * TQ, jnp.max(s, axis=1, keepdims=True))]

    def next_diag(nxt_ref, qslot, ci):
        r = pl.multiple_of(ci * TK, TK)
        q = qs_ref[qslot]
        ck = c_ref[0, 0, ci]
        left = qk(q, k_ref[0, pl.ds(r, H), :]) - key_bias(ck, 0, H, TQ)
        left = jnp.concatenate([causal(left[0:H]), left[H:TQ],
                                causal(left[TQ:TQ + H]), left[TQ + H:]], axis=0)
        q_hi = jnp.concatenate([q[H:TQ], q[TQ + H:]], axis=0)
        right = qk(q_hi, k_ref[0, pl.ds(pl.multiple_of(r + H, H), H), :])
        right = right - key_bias(ck, H, H, H)
        right = jnp.concatenate([causal(right[:H]), causal(right[H:])], axis=0)
        nxt_ref[:, 0:H] = left
        nxt_ref[H:TQ, H:TK] = right[:H]
        nxt_ref[TQ + H:, H:TK] = right[H:]
        rowmax = lambda a: jnp.max(a, axis=1, keepdims=True)
        return [(0, H, rowmax(left[0:H])),
                (H, TQ, jnp.maximum(rowmax(left[H:TQ]), rowmax(right[:H]))),
                (TQ, TQ + H, rowmax(left[TQ:TQ + H])),
                (TQ + H, 2 * TQ, jnp.maximum(rowmax(left[TQ + H:]), rowmax(right[H:])))]

    def pv_full(cur_ref, ci, m_cur):
        p = jnp.exp2(cur_ref[...] - lanes(m_cur, TK)).astype(BF16)
        r = pl.multiple_of(ci * TK, TK)
        return jnp.concatenate([_dot(p[:TQ], va_ref[pl.ds(r, TK), :]),
                                _dot(p[TQ:], vb_ref[pl.ds(r, TK), :])], axis=0)

    def pv_diag(cur_ref, ci, m_cur):
        r = pl.multiple_of(ci * TK, TK)
        r_hi = pl.multiple_of(r + H, H)
        p_left = jnp.exp2(cur_ref[:, 0:H] - lanes(m_cur, H)).astype(BF16)
        out = []
        for head, vh_ref in enumerate((va_ref, vb_ref)):
            lo, hi = head * TQ, head * TQ + H
            pv = _dot(p_left[lo:lo + TQ], vh_ref[pl.ds(r, H), :])
            p_right = jnp.exp2(cur_ref[hi:hi + H, H:TK] - lanes(m_cur[hi:hi + H], H))
            pv_hi = _dot(p_right.astype(BF16), vh_ref[pl.ds(r_hi, H), :])
            out += [pv[:H], pv[H:] + pv_hi]
        return jnp.concatenate(out, axis=0)

    def step_on(cur_ref, nxt_ref, ci, qslot_n, ci_n, cur_diag, nxt_diag):
        if nxt_diag is None:
            row_maxes = []
        else:
            row_maxes = (next_diag if nxt_diag else next_full)(nxt_ref, qslot_n, ci_n)
        m_cur = m_ref[...]
        alpha = alpha_ref[...]
        for lo, hi, mx in row_maxes:
            if cur_diag:
                m_ref[lo:hi] = jnp.broadcast_to(mx, (hi - lo, LANES))
                alpha_ref[lo:hi] = jnp.zeros((hi - lo, LANES), F32)
            else:
                m_next = jnp.maximum(m_cur[lo:hi], mx)
                m_ref[lo:hi] = m_next
                alpha_ref[lo:hi] = jnp.exp2(m_cur[lo:hi] - m_next)
        pv = (pv_diag if cur_diag else pv_full)(cur_ref, ci, m_cur)
        acc = alpha * acc_ref[...] + pv
        if not cur_diag:
            acc_ref[...] = acc
            return
        la = jnp.broadcast_to(acc[:TQ, FOX_HEAD_DIM:FOX_HEAD_DIM + 1], (TQ, LANES))
        lb = jnp.broadcast_to(acc[TQ:, 0:1], (TQ, LANES))
        o = jnp.where(first, acc[:TQ] / la, acc[TQ:] / lb)
        o_ref[0, pl.ds(pl.multiple_of(ci * TQ, TQ), TQ), :] = o.astype(BF16)
        acc_ref[...] = jnp.zeros_like(acc_ref)
        prep_q(jnp.minimum(ci + 2, nq - 1), ci & 1)

    def step(cnt, *args):
        lax.cond((cnt & 1) == 0,
                 lambda: step_on(s0_ref, s1_ref, *args),
                 lambda: step_on(s1_ref, s0_ref, *args))
        return cnt + 1

    def q_block(qb, cnt):
        qslot = qb & 1
        last = qb

        def inner(ci, cnt):
            return step(cnt, ci, qslot, ci + 1, False, False)

        cnt = lax.fori_loop(0, last - 1, inner, cnt)

        def before_diag(ci, cnt):
            return step(cnt, ci, qslot, ci + 1, False, True)

        cnt = lax.fori_loop(jnp.maximum(last - 1, 0), last, before_diag, cnt)
        cnt = lax.cond(qb == nq - 1,
                       lambda c: step(c, last, None, None, True, None),
                       lambda c: step(c, last, 1 - qslot, 0, True, False), cnt)
        return cnt

    prep_q(0, 0)
    prep_q(1, 1)
    acc_ref[...] = jnp.zeros_like(acc_ref)
    for lo, hi, mx in next_diag(s0_ref, 0, 0):
        m_ref[lo:hi] = jnp.broadcast_to(mx, (hi - lo, LANES))
    alpha_ref[...] = jnp.zeros_like(alpha_ref)
    lax.fori_loop(0, nq, q_block, 0)


def _fox_attn(qkv, c, cast=()):
    B, S, _ = qkv.shape
    pairs = FOX_HEADS // 2
    steps = B * pairs
    blk = lambda off: pl.BlockSpec((1, S, LANES), lambda b, p: (b, 0, off + p))
    cast_specs = [pl.BlockSpec((w.shape[0] // steps, w.shape[1]), lambda b, p: (b * pairs + p, 0))
                  for w in cast]
    return pl.pallas_call(
        functools.partial(_fox_attn_kernel, n_cast=len(cast)),
        grid=(B, pairs),
        in_specs=[
            blk(0), blk(pairs), blk(2 * pairs),
            pl.BlockSpec((1, 1, S // TK, 2, TK), lambda b, p: (b, p, 0, 0, 0)),
        ] + cast_specs,
        out_specs=[pl.BlockSpec((1, S, LANES), lambda b, p: (b, 0, p))] + cast_specs,
        out_shape=[jax.ShapeDtypeStruct((B, S, D_MODEL), BF16)]
                  + [jax.ShapeDtypeStruct(w.shape, BF16) for w in cast],
        scratch_shapes=[
            pltpu.VMEM((2, 2 * TQ, LANES), BF16),
            pltpu.VMEM((S, LANES), BF16),
            pltpu.VMEM((S, LANES), BF16),
            pltpu.VMEM((2 * TQ, TK), F32),
            pltpu.VMEM((2 * TQ, TK), F32),
            pltpu.VMEM((2 * TQ, LANES), F32),
            pltpu.VMEM((2 * TQ, LANES), F32),
            pltpu.VMEM((2 * TQ, LANES), F32),
        ],
        compiler_params=pltpu.CompilerParams(
            dimension_semantics=("parallel", "parallel"),
            vmem_limit_bytes=VMEM_LIMIT_BYTES),
        name="fox_attn",
    )(qkv, qkv, qkv, c, *cast)


def _ffn_kernel(*refs, has_mix, final_norm, seg_in, seg_out, pool_tiles):
    refs = list(refs)
    x_ref = refs.pop(0)
    if has_mix:
        a_ref = refs.pop(0)
        wmix_ref = refs.pop(0)
    g_ref, wgu_ref, wd_ref = refs[:3]
    refs = refs[3:]
    if final_norm:
        gf_ref = refs.pop(0)
    if pool_tiles:
        gp_ref, wp_ref, bp_ref, sp_ref = refs[:4]
        refs = refs[4:]
    o_ref, act_ref = refs[:2]
    refs = refs[2:]
    if seg_in or seg_out:
        slab_ref = refs.pop(0)
    if pool_tiles:
        pa_ref, pb_ref, halo_ref, yprev_ref = refs
        assert not seg_out
    n_slabs = D_MODEL // LANES
    seg_rows = [(ch * S5_CHUNK + sg, ch * S5_CHUNK + sg * S5_TJ)
                for ch in range(TM // S5_CHUNK) for sg in range(S5_SEGS)]

    if seg_in:
        cols = []
        for c in range(n_slabs):
            slab_ref[c] = x_ref[:, c * LANES:(c + 1) * LANES]
            cols.append(jnp.concatenate(
                [slab_ref[c, pl.ds(r_seg, S5_TJ, stride=S5_SEGS), :] for r_seg, _ in seg_rows],
                axis=0))
        x = jnp.concatenate(cols, axis=1)
    else:
        x = x_ref[...]
    if has_mix:
        x = x + _dot(a_ref[...], wmix_ref[...])
    h = _rms(x, g_ref[...]).astype(BF16)
    for c in range(D_FF // FF_CHUNK):
        lo = c * FF_CHUNK
        gate = _dot(h, wgu_ref[:, lo:lo + FF_CHUNK])
        up = _dot(h, wgu_ref[:, D_FF + lo:D_FF + lo + FF_CHUNK])
        act_ref[:, lo:lo + FF_CHUNK] = (gate * jax.nn.sigmoid(gate) * up).astype(BF16)
    y = x + _dot(act_ref[...], wd_ref[...])
    if final_norm:
        y = _rms(y, gf_ref[...])
    if pool_tiles:
        step = pl.program_id(0)

        @pl.when(step == 0)
        def _():
            halo_ref[...] = jnp.zeros_like(halo_ref)
            yprev_ref[...] = jnp.zeros_like(yprev_ref)

        t = (step + pool_tiles - 1) % pool_tiles
        yp = yprev_ref[...]
        hp = _rms(yp, gp_ref[...])
        o_ref[...] = yp + _pool_mix(t, hp, jnp.where(t > 0, halo_ref[...], 0.0),
                                    wp_ref, bp_ref, sp_ref, pa_ref, pb_ref)
        halo_ref[...] = hp[TM - POOL_HALO:]
        yprev_ref[...] = y
        return
    if seg_out:
        for c in range(n_slabs):
            for r_seg, r_time in seg_rows:
                slab_ref[c, pl.ds(r_seg, S5_TJ, stride=S5_SEGS), :] = (
                    y[r_time:r_time + S5_TJ, c * LANES:(c + 1) * LANES])
            o_ref[:, c * LANES:(c + 1) * LANES] = slab_ref[c]
    else:
        o_ref[...] = y


def _ffn(x, g, wgu, wd, i, mix=None, final_g=None, seg_in=False, seg_out=False, pool=None):
    N, D = x.shape
    n_tiles = N // TM
    lag = 1 if pool is not None else 0
    row = pl.BlockSpec((TM, D), lambda t: (jnp.minimum(t, n_tiles - 1), 0))
    out_row = pl.BlockSpec((TM, D), lambda t: (jnp.maximum(t - lag, 0), 0))
    scratch = [pltpu.VMEM((TM, D_FF), BF16)]
    if seg_in or seg_out:
        scratch.append(pltpu.VMEM((D // LANES, TM, LANES), F32))
    if pool is not None:
        scratch += [pltpu.VMEM((POOL_HALO + TM, D), F32), pltpu.VMEM((POOL_HALO + TM, D), F32),
                    pltpu.VMEM((POOL_HALO, D), F32), pltpu.VMEM((TM, D), F32)]
    args, specs = [x], [row]
    if mix is not None:
        a2, wmix, j = mix
        args += [a2, wmix]
        specs += [row, _layer(wmix, j)]
    args += [g, wgu, wd]
    specs += [_layer(g, i), _layer(wgu, i), _layer(wd, i)]
    if final_g is not None:
        args.append(final_g)
        specs.append(_resident(final_g.shape))
    pool_tiles = 0
    if pool is not None:
        seq_len, gp, ip, wp, bp, sp, jp = pool
        pool_tiles = seq_len // TM
        args += [gp, wp, bp, sp]
        specs += [_layer(gp, ip), _layer(wp, jp), _layer(bp, jp), _layer(sp, jp)]
    return pl.pallas_call(
        functools.partial(_ffn_kernel, has_mix=mix is not None,
                          final_norm=final_g is not None, seg_in=seg_in, seg_out=seg_out,
                          pool_tiles=pool_tiles),
        grid=(n_tiles + lag,),
        in_specs=specs,
        out_specs=out_row,
        out_shape=jax.ShapeDtypeStruct((N, D), F32),
        scratch_shapes=scratch,
        compiler_params=pltpu.CompilerParams(
            dimension_semantics=("arbitrary",) if pool is not None else ("parallel",),
            vmem_limit_bytes=VMEM_LIMIT_BYTES),
        name="ffn",
    )(*args)


def _s5_kernel(x_ref, g_ref, win_ref, bblk_ref, cblk_ref, lam_ref,
               d_ref, wglu_ref, o_ref, st_ref, carry_ref):
    @pl.when(pl.program_id(1) == 0)
    def _():
        carry_ref[...] = jnp.zeros_like(carry_ref)

    x = x_ref[0]
    h = _rms(x, g_ref[...]).astype(BF16)
    u = _dot(h, win_ref[...])
    ub = u.astype(BF16)

    sub = lax.broadcasted_iota(jnp.int32, (SUBLANES, LANES), 0)
    vshape = (SUBLANES, LANES)
    blocks_per_slab = S5_SLAB_HALF // LANES
    n_blocks = S5_SLABS * blocks_per_slab

    def cmul(ar, ai, br, bi):
        return ar * br - ai * bi, ar * bi + ai * br

    groups_per_slab = blocks_per_slab // S5_COLS_PER_STEP
    ys = []
    for grp in range(n_blocks // S5_COLS_PER_STEP):
        sl = grp // groups_per_slab
        if grp % groups_per_slab == 0:
            st_ref[:, sl * S5_SLAB_W:(sl + 1) * S5_SLAB_W] = _dot(
                ub[:, sl * S5_SLAB_IN:(sl + 1) * S5_SLAB_IN], bblk_ref[sl])
        offs = []
        for b in range(grp * S5_COLS_PER_STEP, (grp + 1) * S5_COLS_PER_STEP):
            re = (b // blocks_per_slab) * S5_SLAB_W + (b % blocks_per_slab) * LANES
            offs.append((re, re + S5_SLAB_HALF))
        lam = [(jnp.broadcast_to(lam_ref[:, re:re + LANES], vshape),
                jnp.broadcast_to(lam_ref[:, im:im + LANES], vshape)) for re, im in offs]

        def scan_body(j, state):
            r = j * SUBLANES
            new = []
            for (re, im), (lr, li), (sr, si) in zip(offs, lam, state):
                pr, pi = cmul(lr, li, sr, si)
                nr = pr + st_ref[pl.ds(r, SUBLANES), re:re + LANES]
                ni = pi + st_ref[pl.ds(r, SUBLANES), im:im + LANES]
                st_ref[pl.ds(r, SUBLANES), re:re + LANES] = nr
                st_ref[pl.ds(r, SUBLANES), im:im + LANES] = ni
                new.append((nr, ni))
            return tuple(new)

        init = tuple((carry_ref[:, re:re + LANES], carry_ref[:, im:im + LANES])
                     for re, im in offs)
        fin = init
        for j in range(S5_TJ):
            fin = scan_body(j, fin)

        carries = []
        for (re, im), (lr, li), (fr, fi) in zip(offs, lam, fin):
            for _ in range(S5_TJ.bit_length() - 1):
                lr, li = cmul(lr, li, lr, li)
            er, ei = fr, fi
            for k in range(1, S5_SEGS):
                pr, pi = cmul(lr, li, pltpu.roll(er, 1, 0), pltpu.roll(ei, 1, 0))
                er = jnp.where(sub == k, fr + pr, er)
                ei = jnp.where(sub == k, fi + pi, ei)
            sr, si = pltpu.roll(er, 1, 0), pltpu.roll(ei, 1, 0)
            carry_ref[:, re:re + LANES] = jnp.where(sub == 0, sr, 0.0)
            carry_ref[:, im:im + LANES] = jnp.where(sub == 0, si, 0.0)
            carries.append((jnp.where(sub == 0, 0.0, sr), jnp.where(sub == 0, 0.0, si)))

        def fix_body(j, f):
            r = j * SUBLANES
            new = []
            for (re, im), (lr, li), (fr, fi) in zip(offs, lam, f):
                fr, fi = cmul(lr, li, fr, fi)
                st_ref[pl.ds(r, SUBLANES), re:re + LANES] += fr
                st_ref[pl.ds(r, SUBLANES), im:im + LANES] += fi
                new.append((fr, fi))
            return tuple(new)

        f = tuple(carries)
        for j in range(S5_TJ):
            f = fix_body(j, f)

        if grp % groups_per_slab == groups_per_slab - 1:
            ys.append(_dot(st_ref[:, sl * S5_SLAB_W:(sl + 1) * S5_SLAB_W].astype(BF16),
                           cblk_ref[sl]))

    y = jnp.concatenate(ys, axis=1) + d_ref[...] * u
    gl = jax.nn.gelu(y).astype(BF16)
    vg = _dot(gl, wglu_ref[...])
    o_ref[0] = x + vg[:, :D_MODEL] * jax.nn.sigmoid(vg[:, D_MODEL:])


def _s5(xp, g, i, win, bblk, cblk, lam, d, wglu, j):
    B, S, D = xp.shape
    return pl.pallas_call(
        _s5_kernel,
        grid=(B, S // S5_CHUNK),
        in_specs=[
            pl.BlockSpec((1, S5_CHUNK, D), lambda b, c: (b, c, 0)),
            _layer(g, i), _layer(win, j), _resident(bblk.shape),
            _resident(cblk.shape), _resident(lam.shape),
            _layer(d, j), _layer(wglu, j),
        ],
        out_specs=pl.BlockSpec((1, S5_CHUNK, D), lambda b, c: (b, c, 0)),
        out_shape=jax.ShapeDtypeStruct((B, S, D), F32),
        scratch_shapes=[
            pltpu.VMEM((S5_CHUNK, S5_SW), F32),
            pltpu.VMEM((SUBLANES, S5_SW), F32),
        ],
        compiler_params=pltpu.CompilerParams(
            dimension_semantics=("parallel", "arbitrary"),
            vmem_limit_bytes=VMEM_LIMIT_BYTES),
        name="s5",
    )(xp, g, win, bblk, cblk, lam, d, wglu)


def _s5_params(a_re, a_im, log_dt, b_re, b_im, c_re, c_im):
    ar, ai = a_re.astype(F32), a_im.astype(F32)
    dt = jnp.exp(log_dt.astype(F32))[:, None]
    mag = jnp.exp(ar * dt)
    lr, li = mag * jnp.cos(ai * dt), mag * jnp.sin(ai * dt)
    den = ar * ar + ai * ai
    kr = ((lr - 1.0) * ar + li * ai) / den
    ki = (li * ar - (lr - 1.0) * ai) / den
    br, bi = b_re.astype(F32), b_im.astype(F32)
    bbr = kr[..., None] * br - ki[..., None] * bi
    bbi = kr[..., None] * bi + ki[..., None] * br

    gs = S5_GROUPS // S5_SLABS
    eye = jnp.eye(gs, dtype=F32)

    def state_row(zr, zi):
        zr = zr.reshape(S5_SLABS, S5_SLAB_HALF)
        zi = zi.reshape(S5_SLABS, S5_SLAB_HALF)
        return jnp.concatenate([zr, zi], axis=-1).reshape(S5_SW)

    same_group = (eye > 0)[None, :, None, :, None]

    def in_block(b):
        bt = b.reshape(S5_SLABS, gs, S5_STATE, S5_GROUP).transpose(0, 1, 3, 2)
        full = jnp.where(same_group, bt[:, :, :, None, :], 0.0)
        return full.reshape(S5_SLABS, S5_SLAB_IN, S5_SLAB_HALF)

    def out_block(c):
        ct = c.reshape(S5_SLABS, gs, S5_GROUP, S5_STATE).transpose(0, 1, 3, 2)
        full = jnp.where(same_group, ct[:, :, :, None, :], 0.0)
        return full.reshape(S5_SLABS, S5_SLAB_HALF, S5_SLAB_IN)

    bblk = jnp.concatenate([in_block(bbr), in_block(bbi)], axis=2)
    cblk = jnp.concatenate([out_block(c_re.astype(F32)), -out_block(c_im.astype(F32))], axis=1)
    return bblk.astype(BF16), cblk.astype(BF16), state_row(lr, li)[None]


POOL_HALO = 32


def _pool_kernel(x_ref, halo_ref, g_ref, w_ref, b_ref, sc_ref, o_ref, a_ref, b2_ref):
    t = pl.program_id(1)
    x = x_ref[0]
    g = g_ref[...]
    hh = jnp.where(t > 0, _rms(halo_ref[0], g), 0.0)
    o_ref[0] = x + _pool_mix(t, _rms(x, g), hh, w_ref, b_ref, sc_ref, a_ref, b2_ref)


def _pool_mix(t, h, halo_h, w_ref, b_ref, sc_ref, a_ref, b2_ref):
    a_ref[0:POOL_HALO] = halo_h
    a_ref[POOL_HALO:] = h
    b2_ref[0:SUBLANES] = jnp.zeros((SUBLANES, D_MODEL), F32)
    rows = POOL_HALO + TM
    tpos = t * TM + lax.broadcasted_iota(jnp.int32, (TM, POOL_WIDTH), 0) + 1
    outs = []
    for gi, w in enumerate(POOL_WINDOWS):
        lo = gi * POOL_WIDTH
        cols = slice(lo, lo + POOL_WIDTH)
        src, dst, sh = a_ref, b2_ref, 1
        while sh < w:
            dst[SUBLANES:rows, cols] = src[SUBLANES:rows, cols] + src[SUBLANES - sh:rows - sh, cols]
            src, dst, sh = dst, src, 2 * sh
        mean = src[POOL_HALO:rows, cols] / jnp.minimum(tpos, w).astype(F32)
        outs.append(_dot((mean - h[:, cols]).astype(BF16), w_ref[gi]))
    return (jnp.concatenate(outs, axis=1) + b_ref[...]) * sc_ref[...]


def _pool(x, g, i, w, b, sc, j):
    B, S, D = x.shape
    per = TM // POOL_HALO
    return pl.pallas_call(
        _pool_kernel,
        grid=(B, S // TM),
        in_specs=[
            pl.BlockSpec((1, TM, D), lambda b, t: (b, t, 0)),
            pl.BlockSpec((1, POOL_HALO, D), lambda b, t: (b, jnp.maximum(t * per - 1, 0), 0)),
            _layer(g, i), _layer(w, j), _layer(b, j), _layer(sc, j),
        ],
        out_specs=pl.BlockSpec((1, TM, D), lambda b, t: (b, t, 0)),
        out_shape=jax.ShapeDtypeStruct((B, S, D), F32),
        scratch_shapes=[pltpu.VMEM((POOL_HALO + TM, D), F32),
                        pltpu.VMEM((POOL_HALO + TM, D), F32)],
        compiler_params=pltpu.CompilerParams(
            dimension_semantics=("parallel", "arbitrary"),
            vmem_limit_bytes=VMEM_LIMIT_BYTES),
        name="pool",
    )(x, x, g, w, b, sc)


def kernel(x, mix_norm_g, ffn_norm_g, final_norm_g, fox_w_in, fox_b_f, fox_w_out, s5_w_in, s5_a_re, s5_a_im, s5_log_dt, s5_b_re, s5_b_im, s5_c_re, s5_c_im, s5_d, s5_w_glu, pool_w, pool_b, pool_scale, ffn_w_gate_up, ffn_w_down):
    B, S, D = x.shape
    N = B * S
    tri = jnp.triu(jnp.ones((TM, TM), F32)).astype(BF16)
    row = lambda a: a.astype(F32).reshape(a.shape[0], 1, a.shape[1])
    gm_all, gf_all = row(mix_norm_g), row(ffn_norm_g)
    wgu_all = wd_all = fox_wout = s5_win = s5_wglu = None
    fox_win = fox_w_in.astype(BF16)
    fox_wft = lax.optimization_barrier(fox_w_in[:, :, 3 * D:]).transpose(0, 2, 1).astype(BF16)
    fox_bf = fox_b_f.astype(F32)[:, :, None]
    s5_dd = row(s5_d)
    pool_ww, pool_bb, pool_sc = pool_w.astype(BF16), row(pool_b), row(pool_scale)

    for i in range(DEPTH):
        kind, j = i % 3, i // 3
        next_is_s5 = i + 1 < DEPTH and (i + 1) % 3 == 1
        final_g = final_norm_g[None].astype(F32) if i == DEPTH - 1 else None
        mix = None
        if kind == 0:
            qkv, c = _fox_proj(x, gm_all, i, fox_win, fox_wft, fox_bf, j, tri)
            c = c.reshape(B, FOX_HEADS // 2, 2, S // TK, TK).transpose(0, 1, 3, 2, 4)
            if wgu_all is None:
                late = (ffn_w_gate_up, ffn_w_down, fox_w_out, s5_w_in, s5_w_glu)
                o, *cast = _fox_attn(qkv, c, cast=tuple(
                    w.astype(F32).reshape(-1, w.shape[-1]) for w in late))
                wgu_all, wd_all, fox_wout, s5_win, s5_wglu = (
                    w2.reshape(w.shape) for w2, w in zip(cast, late))
            else:
                o, = _fox_attn(qkv, c)
            mix = (o.reshape(N, D), fox_wout, j)
        elif kind == 1:
            bblk, cblk, lam = _s5_params(
                s5_a_re[j], s5_a_im[j], s5_log_dt[j], s5_b_re[j], s5_b_im[j], s5_c_re[j], s5_c_im[j])
            x = _s5(x, gm_all, i, s5_win, bblk, cblk, lam, s5_dd, s5_wglu, j)
        elif i == 0:
            x = _pool(x, gm_all, i, pool_ww, pool_bb, pool_sc, j)
        next_is_pool = i + 1 < DEPTH and (i + 1) % 3 == 2
        pool = (S, gm_all, i + 1, pool_ww, pool_bb, pool_sc, (i + 1) // 3) if next_is_pool else None
        x = _ffn(x.reshape(N, D), gf_all, wgu_all, wd_all, i, mix=mix, final_g=final_g,
                 seg_in=kind == 1, seg_out=next_is_s5, pool=pool).reshape(B, S, D)
    return x
```
